```python
import math
import jax
import jax.numpy as jnp
from jax import lax
import numpy as np

D_MODEL = 1024
BATCH = 16
SEQ = 4096
DEPTH = 1

NORM_EPS = 1e-5

ATT_HEAD_DIM = 128
ATT_HEADS_PER_GROUP = 4
DILATED_GROUPS = ((128, 1), (512, 4), (2048, 16))
N_ATT_GROUPS = len(DILATED_GROUPS)
N_ATT_HEADS = N_ATT_GROUPS * ATT_HEADS_PER_GROUP
ATT_OUT_WIDTH = ATT_HEADS_PER_GROUP * ATT_HEAD_DIM
REL_BUCKETS = 32
REL_MAX_DIST = 2048

RET_HEADS = 4
RET_DK = 128
RET_DV = 256
RET_CHUNK = 128
ROPE_BASE = 10000.0

PROJ_SPLITS = (
    N_ATT_HEADS * ATT_HEAD_DIM, N_ATT_HEADS * ATT_HEAD_DIM, N_ATT_HEADS * ATT_HEAD_DIM,
    RET_HEADS * RET_DK, RET_HEADS * RET_DK,
    RET_HEADS * RET_DV, RET_HEADS * RET_DV,
    D_MODEL, D_MODEL,
)
PROJ_WIDTH = sum(PROJ_SPLITS)
PROJ_OFFSETS = tuple(int(v) for v in np.cumsum(PROJ_SPLITS)[:-1])

N_EXPERTS = 32
TOP_K = 4
D_EXPERT = 1024
SWIGLU_LIMIT = 7.0
SWIGLU_ALPHA = 1.702
MOE_BLOCK = 256

kernel_name = "hybrid_dilated_retention_moe_block"


def rmsnorm(x, g):
    xf = x.astype(jnp.float32)
    y = xf * lax.rsqrt(jnp.mean(xf * xf, axis=-1, keepdims=True) + NORM_EPS)
    return (y * g.astype(jnp.float32)).astype(x.dtype)


def t5_bucket(dist):
    max_exact = REL_BUCKETS // 2
    d_f = jnp.maximum(dist, 1).astype(jnp.float32)
    large = max_exact + (jnp.log(d_f / max_exact) / math.log(REL_MAX_DIST / max_exact)
                         * (REL_BUCKETS - max_exact)).astype(jnp.int32)
    large = jnp.minimum(large, REL_BUCKETS - 1)
    return jnp.where(dist < max_exact, dist, large)


def dilated_window_attention(q, k, v, rel_bias, dilation, n_steps):
    B, S, H, E = q.shape
    L = S // dilation
    nb = -(-L // n_steps)
    Lp = nb * n_steps

    def to_blocks(t):
        t = t.reshape(B, L, dilation, H, E).transpose(0, 2, 1, 3, 4)
        t = jnp.pad(t, ((0, 0), (0, 0), (0, Lp - L), (0, 0), (0, 0)))
        return t.reshape(B, dilation, nb, n_steps, H, E)

    def with_prev(t):
        prev = jnp.pad(t, ((0, 0), (0, 0), (1, 0), (0, 0), (0, 0), (0, 0)))[:, :, :-1]
        return jnp.concatenate([prev, t], axis=3)

    qb = to_blocks(q)
    kw = with_prev(to_blocks(k))
    vw = with_prev(to_blocks(v))

    qi = jnp.arange(n_steps)[:, None]
    kj = jnp.arange(2 * n_steps)[None, :]
    step_dist = qi + n_steps - kj
    band = (step_dist >= 0) & (step_dist <= n_steps)
    key_step = jnp.arange(nb)[:, None] * n_steps + kj - n_steps
    valid = band[None] & (key_step >= 0)[:, None, :]
    bias = rel_bias[t5_bucket(jnp.maximum(step_dist, 0) * dilation)]
    bias = bias.astype(jnp.float32).transpose(2, 0, 1)

    scale = E ** -0.5
    logits = jnp.einsum('brnqhe,brnkhe->brnhqk', qb, kw).astype(jnp.float32) * scale + bias
    logits = jnp.where(valid[None, None, :, None], logits, -jnp.inf)
    m = jnp.max(logits, axis=-1, keepdims=True)
    p = jnp.exp(logits - m)
    denom = jnp.sum(p, axis=-1)
    o = jnp.einsum('brnhqk,brnkhe->brnqhe', p, vw.astype(jnp.float32))
    o = o / denom.transpose(0, 1, 2, 4, 3)[..., None]
    lse = (m[..., 0] + jnp.log(denom)).transpose(0, 1, 2, 4, 3)

    o = o.reshape(B, dilation, Lp, H, E)[:, :, :L].transpose(0, 2, 1, 3, 4).reshape(B, S, H, E)
    lse = lse.reshape(B, dilation, Lp, H)[:, :, :L].transpose(0, 2, 1, 3).reshape(B, S, H)
    return o, lse


def rotary(t):
    S, E = t.shape[1], t.shape[-1]
    inv = ROPE_BASE ** (-jnp.arange(0, E, 2, dtype=jnp.float32) / E)
    ang = jnp.arange(S, dtype=jnp.float32)[:, None] * inv[None]
    cos = jnp.cos(ang)[None, :, None, :]
    sin = jnp.sin(ang)[None, :, None, :]
    tf = t.astype(jnp.float32)
    t1, t2 = tf[..., 0::2], tf[..., 1::2]
    return jnp.stack([t1 * cos - t2 * sin, t1 * sin + t2 * cos], axis=-1).reshape(t.shape)


def retention_chunkwise(q, k, v):
    B, S, H, dk = q.shape
    dv = v.shape[-1]
    C = RET_CHUNK
    nc = S // C
    log_g = jnp.log(1.0 - 2.0 ** (-5.0 - jnp.arange(H, dtype=jnp.float32)))
    idx = jnp.arange(C, dtype=jnp.float32)
    diff = idx[:, None] - idx[None, :]
    decay = jnp.where(diff >= 0, jnp.exp(jnp.maximum(diff, 0.0)[None] * log_g[:, None, None]), 0.0)
    xi = jnp.exp((idx + 1.0)[:, None] * log_g[None])
    zeta = jnp.exp((C - 1.0 - idx)[:, None] * log_g[None])
    g_chunk = jnp.exp(C * log_g)

    qc = q.reshape(B, nc, C, H, dk)
    kc = k.reshape(B, nc, C, H, dk)
    vc = v.reshape(B, nc, C, H, dv)
    scores = jnp.einsum('bnihd,bnjhd->bnhij', qc, kc) * decay
    inner = jnp.einsum('bnhij,bnjhe->bnihe', scores, vc)

    def step(R, xs):
        q_n, k_n, v_n = xs
        cross = jnp.einsum('bihd,bhde->bihe', q_n, R) * xi[None, :, :, None]
        R = R * g_chunk[None, :, None, None] + jnp.einsum('bjhd,bjhe->bhde', k_n * zeta[None, :, :, None], v_n)
        return R, cross

    R0 = jnp.zeros((B, H, dk, dv), jnp.float32)
    _, cross = lax.scan(step, R0, (qc.transpose(1, 0, 2, 3, 4), kc.transpose(1, 0, 2, 3, 4),
                                   vc.transpose(1, 0, 2, 3, 4)))
    cross = cross.transpose(1, 0, 2, 3, 4)
    return (inner + cross).reshape(B, S, H, dv)


def moe_ffn(h, w_router, b_router, w1, b1, w2, b2):
    B, S, D = h.shape
    T = B * S
    M = MOE_BLOCK
    F = D_EXPERT
    xt = h.reshape(T, D)
    logits = (xt @ w_router + b_router).astype(jnp.float32)
    top_vals, top_idx = lax.top_k(logits, TOP_K)
    gates = jax.nn.softmax(top_vals, axis=-1)

    A = T * TOP_K
    e_flat = top_idx.reshape(A)
    tok_flat = jnp.broadcast_to(jnp.arange(T, dtype=jnp.int32)[:, None], (T, TOP_K)).reshape(A)
    w_flat = gates.reshape(A)
    order = jnp.argsort(e_flat)
    e_sorted = e_flat[order]
    tok_sorted = tok_flat[order]
    w_sorted = w_flat[order]

    counts = jnp.bincount(e_flat, length=N_EXPERTS)
    padded = ((counts + M - 1) // M) * M
    start = jnp.cumsum(counts) - counts
    pend = jnp.cumsum(padded)
    pstart = pend - padded
    pos = pstart[e_sorted] + jnp.arange(A) - start[e_sorted]
    nblk = -(-A // M) + N_EXPERTS
    P = nblk * M
    buf_tok = jnp.full((P,), T, jnp.int32).at[pos].set(tok_sorted)
    buf_w = jnp.zeros((P,), jnp.float32).at[pos].set(w_sorted)
    blk_expert = jnp.minimum(jnp.searchsorted(pend, jnp.arange(nblk) * M, side='right'), N_EXPERTS - 1)

    x_pad = jnp.concatenate([xt, jnp.zeros((1, D), xt.dtype)], axis=0)

    def expert_block(args):
        tok, e = args
        xb = x_pad[tok]
        gu = xb @ w1[e] + b1[e]
        gate = jnp.minimum(gu[:, :F], SWIGLU_LIMIT)
        up = jnp.clip(gu[:, F:], -SWIGLU_LIMIT, SWIGLU_LIMIT)
        act = (up + 1.0) * gate * jax.nn.sigmoid(SWIGLU_ALPHA * gate)
        return act @ w2[e] + b2[e]

    yb = lax.map(expert_block, (buf_tok.reshape(nblk, M), blk_expert))
    y = yb.reshape(P, D) * buf_w[:, None].astype(yb.dtype)
    out = jnp.zeros((T + 1, D), yb.dtype).at[buf_tok].add(y)[:T]
    return out.reshape(B, S, D).astype(h.dtype)


def setup_inputs(seed: int = 0) -> dict:
    key = jax.random.key(seed)
    ks = jax.random.split(key, 16)
    D = D_MODEL
    nrm = jax.random.normal
    ret_w = RET_HEADS * RET_DV
    return {
        "x": nrm(ks[0], (BATCH, SEQ, D), jnp.float32),
        "norm_mix_g": 1.0 + 0.02 * nrm(ks[1], (DEPTH, D), jnp.float32),
        "w_in": nrm(ks[2], (DEPTH, D, PROJ_WIDTH), jnp.float32) * D ** -0.5,
        "rel_bias": 0.5 * nrm(ks[3], (REL_BUCKETS, N_ATT_HEADS), jnp.float32),
        "w_att_branch": nrm(ks[4], (DEPTH, ATT_OUT_WIDTH, D), jnp.float32) * ATT_OUT_WIDTH ** -0.5,
        "w_ret_branch": nrm(ks[5], (DEPTH, ret_w, D), jnp.float32) * ret_w ** -0.5,
        "w_out": nrm(ks[6], (DEPTH, D, D), jnp.float32) * D ** -0.5,
        "norm_ffn_g": 1.0 + 0.02 * nrm(ks[7], (DEPTH, D), jnp.float32),
        "w_router": nrm(ks[8], (DEPTH, D, N_EXPERTS), jnp.float32) * D ** -0.5,
        "b_router": 0.01 * nrm(ks[9], (DEPTH, N_EXPERTS), jnp.float32),
        "w1": nrm(ks[10], (DEPTH, N_EXPERTS, D, 2 * D_EXPERT), jnp.float32) * D ** -0.5,
        "b1": 0.01 * nrm(ks[11], (DEPTH, N_EXPERTS, 2 * D_EXPERT), jnp.float32),
        "w2": nrm(ks[12], (DEPTH, N_EXPERTS, D_EXPERT, D), jnp.float32) * D_EXPERT ** -0.5,
        "b2": 0.01 * nrm(ks[13], (DEPTH, N_EXPERTS, D), jnp.float32),
        "norm_final_g": 1.0 + 0.02 * nrm(ks[14], (D,), jnp.float32),
    }


def reference(x, norm_mix_g, w_in, rel_bias, w_att_branch, w_ret_branch, w_out, norm_ffn_g,
              w_router, b_router, w1, b1, w2, b2, norm_final_g):
    B, S, D = x.shape
    for layer in range(DEPTH):
        h = rmsnorm(x, norm_mix_g[layer])
        proj = h @ w_in[layer]
        q_a, k_a, v_a, q_r, k_r, v_r, g_r, gate_a, gate_r = jnp.split(proj, PROJ_OFFSETS, axis=-1)

        q_a = q_a.reshape(B, S, N_ATT_HEADS, ATT_HEAD_DIM)
        k_a = k_a.reshape(B, S, N_ATT_HEADS, ATT_HEAD_DIM)
        v_a = v_a.reshape(B, S, N_ATT_HEADS, ATT_HEAD_DIM)
        outs, lses = [], []
        for g, (window, dilation) in enumerate(DILATED_GROUPS):
            hs = slice(g * ATT_HEADS_PER_GROUP, (g + 1) * ATT_HEADS_PER_GROUP)
            o_g, lse_g = dilated_window_attention(q_a[:, :, hs], k_a[:, :, hs], v_a[:, :, hs],
                                                  rel_bias[:, hs], dilation, window // dilation)
            outs.append(o_g)
            lses.append(lse_g)
        wts = jax.nn.softmax(jnp.stack(lses, axis=0), axis=0)
        att = jnp.sum(wts[..., None] * jnp.stack(outs, axis=0), axis=0)
        att = att.reshape(B, S, ATT_OUT_WIDTH).astype(x.dtype)

        q_r = rotary(q_r.reshape(B, S, RET_HEADS, RET_DK))
        k_r = rotary(k_r.reshape(B, S, RET_HEADS, RET_DK)) * RET_DK ** -0.5
        v_r = v_r.reshape(B, S, RET_HEADS, RET_DV).astype(jnp.float32)
        ret = retention_chunkwise(q_r, k_r, v_r)
        mu = jnp.mean(ret, axis=-1, keepdims=True)
        var = jnp.mean(jnp.square(ret - mu), axis=-1, keepdims=True)
        ret = (ret - mu) * lax.rsqrt(var + NORM_EPS)
        ret = (jax.nn.silu(g_r.astype(jnp.float32)) * ret.reshape(B, S, RET_HEADS * RET_DV)).astype(x.dtype)

        merged = (jax.nn.sigmoid(gate_a) * (att @ w_att_branch[layer])
                  + jax.nn.sigmoid(gate_r) * (ret @ w_ret_branch[layer]))
        x = x + merged @ w_out[layer]

        hf = rmsnorm(x, norm_ffn_g[layer])
        x = x + moe_ffn(hf, w_router[layer], b_router[layer], w1[layer], b1[layer], w2[layer], b2[layer])
    return rmsnorm(x, norm_final_g)
```

```python
import functools
import math

import numpy as np
import jax
import jax.numpy as jnp
from jax import lax
from jax.experimental import pallas as pl
from jax.experimental.pallas import tpu as pltpu

F32 = jnp.float32
BF16 = jnp.bfloat16

NORM_EPS = 1e-5

HEAD_DIM = 128
HEADS_PER_GROUP = 4
GROUP_WIDTH = HEADS_PER_GROUP * HEAD_DIM
DILATIONS = (1, 4, 16)
N_STEPS = 128
N_GROUPS = len(DILATIONS)
REL_BUCKETS = 32
REL_MAX_DIST = 2048

RET_HEADS = 4
RET_DK = 128
RET_DV = 256
RET_CHUNK = 128
ROPE_BASE = 10000.0

N_EXPERTS = 32
TOP_K = 4
SWIGLU_LIMIT = 7.0
SWIGLU_ALPHA = 1.702
EXPERT_BLOCK = 256

LANES = 128
COL_BLOCK = 512
NEG_BIG = -1e30

VMEM_LIMIT = 48 * 1024 * 1024


def _cparams(sem, vmem=VMEM_LIMIT):
    return pltpu.CompilerParams(dimension_semantics=sem, vmem_limit_bytes=vmem)


def _inproj_kernel(x_ref, g_ref, w_ref, o_ref, h_ref):
    x = x_ref[...]
    ms = jnp.mean(x * x, axis=-1, keepdims=True)
    h_ref[...] = (x * lax.rsqrt(ms + NORM_EPS) * g_ref[...]).astype(BF16)
    n_chunks = o_ref.shape[1] // COL_BLOCK
    for c in range(n_chunks):
        cols = slice(c * COL_BLOCK, (c + 1) * COL_BLOCK)
        o_ref[:, cols] = jnp.dot(h_ref[...], w_ref[:, cols], preferred_element_type=F32).astype(BF16)


def _inproj(x2d, g, w_bf16, tm=512):
    T, D = x2d.shape
    N = w_bf16.shape[1]
    return pl.pallas_call(
        _inproj_kernel,
        out_shape=jax.ShapeDtypeStruct((T, N), BF16),
        grid=(T // tm,),
        in_specs=[
            pl.BlockSpec((tm, D), lambda i: (i, 0)),
            pl.BlockSpec((1, D), lambda i: (0, 0)),
            pl.BlockSpec((D, N), lambda i: (0, 0), pipeline_mode=pl.Buffered(1)),
        ],
        out_specs=pl.BlockSpec((tm, N), lambda i: (i, 0)),
        scratch_shapes=[pltpu.VMEM((tm, D), BF16)],
        compiler_params=_cparams(("arbitrary",), 56 * 1024 * 1024),
        name="inproj",
    )(x2d, g.reshape(1, D), w_bf16)


def _attn_block(q, kp, kc, vp, vc, bias_ref, has_prev):
    scale = HEAD_DIM ** -0.5
    lane = lax.broadcasted_iota(jnp.int32, (N_STEPS, LANES), 1)
    outs = []
    lse = jnp.zeros((N_STEPS, LANES), F32)
    dn = (((1,), (1,)), ((), ()))
    for h in range(HEADS_PER_GROUP):
        hs = slice(h * HEAD_DIM, (h + 1) * HEAD_DIM)
        qh = q[:, hs]
        sp = lax.dot_general(qh, kp[:, hs], dn, preferred_element_type=F32) * scale + bias_ref[h, :, :N_STEPS]
        sc = lax.dot_general(qh, kc[:, hs], dn, preferred_element_type=F32) * scale + bias_ref[h, :, N_STEPS:]
        sp = jnp.where(has_prev, sp, NEG_BIG)
        m = jnp.maximum(jnp.max(sp, axis=-1, keepdims=True), jnp.max(sc, axis=-1, keepdims=True))
        pp = jnp.exp(sp - m)
        pc = jnp.exp(sc - m)
        l = jnp.sum(pp, axis=-1, keepdims=True) + jnp.sum(pc, axis=-1, keepdims=True)
        acc = (jnp.dot(pp.astype(BF16), vp[:, hs], preferred_element_type=F32)
               + jnp.dot(pc.astype(BF16), vc[:, hs], preferred_element_type=F32))
        outs.append(acc / l)
        lse = jnp.where(lane == h, m + jnp.log(l), lse)
    return jnp.concatenate(outs, axis=-1), lse


def _attn_kernel(q_ref, k_ref, v_ref, kprev_ref, vprev_ref, bias_ref, o_ref, lse_ref, *, blocks):
    n = pl.program_id(2)

    def run(j, kp, vp, has_prev):
        rows = pl.ds(pl.multiple_of(j * N_STEPS, N_STEPS), N_STEPS)
        o, lse = _attn_block(q_ref[rows, :], kp, k_ref[rows, :], vp, v_ref[rows, :], bias_ref, has_prev)
        o_ref[rows, :] = o.astype(o_ref.dtype)
        lse_ref[rows, :] = lse

    run(0, kprev_ref[...], vprev_ref[...], n > 0)

    def body(j, carry):
        prev = pl.ds(pl.multiple_of((j - 1) * N_STEPS, N_STEPS), N_STEPS)
        run(j, k_ref[prev, :], v_ref[prev, :], True)
        return carry

    if blocks > 1:
        lax.fori_loop(1, blocks, body, 0)


def _attention_group(q, k, v, bias, blocks):
    B, d, L, W = q.shape
    rows = blocks * N_STEPS
    steps = L // rows
    cur = pl.BlockSpec((None, None, rows, W), lambda b, r, n: (b, r, n, 0))
    prev = pl.BlockSpec((None, None, N_STEPS, W), lambda b, r, n: (b, r, jnp.maximum(n * blocks - 1, 0), 0))
    return pl.pallas_call(
        functools.partial(_attn_kernel, blocks=blocks),
        out_shape=(jax.ShapeDtypeStruct((B, d, L, W), BF16), jax.ShapeDtypeStruct((B, d, L, LANES), F32)),
        grid=(B, d, steps),
        in_specs=[cur, cur, cur, prev, prev,
                  pl.BlockSpec((HEADS_PER_GROUP, N_STEPS, 2 * N_STEPS), lambda b, r, n: (0, 0, 0))],
        out_specs=(pl.BlockSpec((None, None, rows, W), lambda b, r, n: (b, r, n, 0)),
                   pl.BlockSpec((None, None, rows, LANES), lambda b, r, n: (b, r, n, 0))),
        compiler_params=_cparams(("arbitrary", "arbitrary", "arbitrary")),
        name=f"attn_d{d}",
    )(q, k, v, k, v, bias)


def _t5_bucket(dist):
    max_exact = REL_BUCKETS // 2
    d_f = jnp.maximum(dist, 1).astype(F32)
    large = max_exact + (jnp.log(d_f / max_exact) / math.log(REL_MAX_DIST / max_exact)
                         * (REL_BUCKETS - max_exact)).astype(jnp.int32)
    large = jnp.minimum(large, REL_BUCKETS - 1)
    return jnp.where(dist < max_exact, dist, large)


def _band_bias(rel_bias_g, dilation):
    qi = jnp.arange(N_STEPS)[:, None]
    kj = jnp.arange(2 * N_STEPS)[None, :]
    step_dist = qi + N_STEPS - kj
    band = (step_dist >= 0) & (step_dist <= N_STEPS)
    bias = rel_bias_g[_t5_bucket(jnp.maximum(step_dist, 0) * dilation)]
    bias = bias.astype(F32).transpose(2, 0, 1)
    return jnp.where(band[None], bias, NEG_BIG)


def _ret_kernel(q_ref, k_ref, v0_ref, v1_ref, g0_ref, g1_ref, cos_ref, sin_ref, decay_ref, xi_ref, zeta_ref,
                o_ref, state_ref, *, g_chunk):
    n = pl.program_id(1)

    @pl.when(n == 0)
    def _():
        state_ref[...] = jnp.zeros_like(state_ref)

    cos = cos_ref[...]
    sin = sin_ref[...]
    lane = lax.broadcasted_iota(jnp.int32, (RET_CHUNK, RET_DK), 1)
    even = (lane % 2) == 0

    def rotary(t):
        partner = jnp.where(even, pltpu.roll(t, RET_DK - 1, 1), pltpu.roll(t, 1, 1))
        return t * cos + partner * sin

    for h in range(RET_HEADS):
        ks = slice(h * RET_DK, (h + 1) * RET_DK)
        qr = rotary(q_ref[:, ks].astype(F32))
        kr = rotary(k_ref[:, ks].astype(F32)) * (RET_DK ** -0.5)
        v_ref = v0_ref if h < 2 else v1_ref
        g_ref = g0_ref if h < 2 else g1_ref
        vs = slice((h % 2) * RET_DV, (h % 2 + 1) * RET_DV)
        v = v_ref[:, vs]
        qb = qr.astype(BF16)
        kb = kr.astype(BF16)
        scores = lax.dot_general(qb, kb, (((1,), (1,)), ((), ())), preferred_element_type=F32) * decay_ref[h]
        inner = jnp.dot(scores.astype(BF16), v, preferred_element_type=F32)
        state = state_ref[h]
        cross = jnp.dot(qb, state.astype(BF16), preferred_element_type=F32) * xi_ref[h]
        kz = (kr * zeta_ref[h]).astype(BF16)
        state_ref[h] = state * g_chunk[h] + lax.dot_general(kz, v, (((0,), (0,)), ((), ())),
                                                            preferred_element_type=F32)
        ret = inner + cross
        mu = jnp.mean(ret, axis=-1, keepdims=True)
        cen = ret - mu
        var = jnp.mean(cen * cen, axis=-1, keepdims=True)
        normed = cen * lax.rsqrt(var + NORM_EPS)
        g = g_ref[:, vs].astype(F32)
        o_ref[:, h * RET_DV:(h + 1) * RET_DV] = (g * jax.nn.sigmoid(g) * normed).astype(o_ref.dtype)


def _retention(proj, B, S, col0):
    T = B * S
    nc = S // RET_CHUNK
    C = RET_CHUNK
    log_g = np.log(1.0 - 2.0 ** (-5.0 - np.arange(RET_HEADS, dtype=np.float64)))
    idx = np.arange(C, dtype=np.float64)
    diff = idx[:, None] - idx[None, :]
    decay = np.where(diff >= 0, np.exp(np.maximum(diff, 0.0)[None] * log_g[:, None, None]), 0.0)
    xi = np.exp((idx + 1.0)[None, :] * log_g[:, None])
    zeta = np.exp((C - 1.0 - idx)[None, :] * log_g[:, None])
    g_chunk = tuple(float(v) for v in np.exp(C * log_g))
    xi_b = np.broadcast_to(xi[:, :, None], (RET_HEADS, C, RET_DV)).astype(np.float32)
    zeta_b = np.broadcast_to(zeta[:, :, None], (RET_HEADS, C, RET_DK)).astype(np.float32)

    inv = ROPE_BASE ** (-np.arange(0, RET_DK, 2, dtype=np.float64) / RET_DK)
    ang = np.arange(S, dtype=np.float64)[:, None] * inv[None]
    cos_t = np.repeat(np.cos(ang), 2, axis=1).astype(np.float32)
    sin_t = np.stack([-np.sin(ang), np.sin(ang)], axis=-1).reshape(S, RET_DK).astype(np.float32)

    def col(c):
        return pl.BlockSpec((C, COL_BLOCK), lambda b, n, c=c: (b * nc + n, c))

    const3 = lambda shape: pl.BlockSpec(shape, lambda b, n: (0, 0, 0))
    return pl.pallas_call(
        functools.partial(_ret_kernel, g_chunk=g_chunk),
        out_shape=jax.ShapeDtypeStruct((T, RET_HEADS * RET_DV), BF16),
        grid=(B, nc),
        in_specs=[col(col0), col(col0 + 1), col(col0 + 2), col(col0 + 3), col(col0 + 4), col(col0 + 5),
                  pl.BlockSpec((C, RET_DK), lambda b, n: (n, 0)),
                  pl.BlockSpec((C, RET_DK), lambda b, n: (n, 0)),
                  const3((RET_HEADS, C, C)), const3((RET_HEADS, C, RET_DV)), const3((RET_HEADS, C, RET_DK))],
        out_specs=pl.BlockSpec((C, RET_HEADS * RET_DV), lambda b, n: (b * nc + n, 0)),
        scratch_shapes=[pltpu.VMEM((RET_HEADS, RET_DK, RET_DV), F32)],
        compiler_params=_cparams(("arbitrary", "arbitrary")),
        name="retention",
    )(proj, proj, proj, proj, proj, proj, jnp.asarray(cos_t), jnp.asarray(sin_t),
      jnp.asarray(decay.astype(np.float32)), jnp.asarray(xi_b), jnp.asarray(zeta_b))


ROUTE_IDX, ROUTE_RANK, ROUTE_GATE = 0, 4, 8


def _merge_kernel(x_ref, o1_ref, o2_ref, o3_ref, l1_ref, l2_ref, l3_ref, ret_ref,
                  ga0_ref, ga1_ref, gr0_ref, gr1_ref, watt_ref, wret_ref, wout_ref, gffn_ref, wr_ref, br_ref,
                  x1_ref, hf_ref, route_ref, counts_ref):
    tm = x_ref.shape[0]
    o_refs = (o1_ref, o2_ref, o3_ref)
    lses = [r[...] for r in (l1_ref, l2_ref, l3_ref)]

    att_parts = []
    for h in range(HEADS_PER_GROUP):
        ls = [l[:, h:h + 1] for l in lses]
        m = jnp.maximum(jnp.maximum(ls[0], ls[1]), ls[2])
        ws = [jnp.exp(l - m) for l in ls]
        wsum = ws[0] + ws[1] + ws[2]
        hs = slice(h * HEAD_DIM, (h + 1) * HEAD_DIM)
        acc = (ws[0] / wsum) * o_refs[0][:, hs].astype(F32)
        acc = acc + (ws[1] / wsum) * o_refs[1][:, hs].astype(F32)
        acc = acc + (ws[2] / wsum) * o_refs[2][:, hs].astype(F32)
        att_parts.append(acc)
    att = jnp.concatenate(att_parts, axis=-1).astype(BF16)

    a_proj = jnp.dot(att, watt_ref[...], preferred_element_type=F32)
    r_proj = jnp.dot(ret_ref[...], wret_ref[...], preferred_element_type=F32)
    gate_a = jnp.concatenate([ga0_ref[...], ga1_ref[...]], axis=-1).astype(F32)
    gate_r = jnp.concatenate([gr0_ref[...], gr1_ref[...]], axis=-1).astype(F32)
    merged = jax.nn.sigmoid(gate_a) * a_proj + jax.nn.sigmoid(gate_r) * r_proj
    x1 = x_ref[...] + jnp.dot(merged.astype(BF16), wout_ref[...], preferred_element_type=F32)
    x1_ref[...] = x1

    ms = jnp.mean(x1 * x1, axis=-1, keepdims=True)
    hf = x1 * lax.rsqrt(ms + NORM_EPS) * gffn_ref[...]
    hf_ref[...] = hf

    logits = jnp.dot(hf.astype(BF16), wr_ref[...], preferred_element_type=F32) + br_ref[...]
    lane = lax.broadcasted_iota(jnp.int32, (tm, LANES), 1)
    lane_f = lane.astype(F32)
    work = logits
    vals, idxs, hits = [], [], []
    for _ in range(TOP_K):
        mk = jnp.max(work, axis=-1, keepdims=True)
        ik = jnp.min(jnp.where(work == mk, lane_f, float(LANES)), axis=-1, keepdims=True)
        hit = lane_f == ik
        work = jnp.where(hit, -jnp.inf, work)
        vals.append(mk)
        idxs.append(ik)
        hits.append(hit)
    es = [jnp.exp(v - vals[0]) for v in vals]
    esum = es[0] + es[1] + es[2] + es[3]
    gates = [e / esum for e in es]

    sel = jnp.zeros((tm, LANES), F32)
    for hit in hits:
        sel = jnp.where(hit, 1.0, sel)

    @pl.when(pl.program_id(0) == 0)
    def _():
        counts_ref[...] = jnp.zeros_like(counts_ref)

    row = lax.broadcasted_iota(jnp.int32, (tm, tm), 0)
    colm = lax.broadcasted_iota(jnp.int32, (tm, tm), 1)
    below = jnp.where(colm < row, 1.0, 0.0).astype(BF16)
    rank = jnp.dot(below, sel.astype(BF16), preferred_element_type=F32) + counts_ref[...]
    counts_ref[...] = counts_ref[...] + jnp.sum(sel, axis=0, keepdims=True)

    route = jnp.zeros((tm, LANES), F32)
    for k in range(TOP_K):
        rank_k = jnp.sum(jnp.where(hits[k], rank, 0.0), axis=-1, keepdims=True)
        route = jnp.where(lane == ROUTE_IDX + k, idxs[k], route)
        route = jnp.where(lane == ROUTE_RANK + k, rank_k, route)
        route = jnp.where(lane == ROUTE_GATE + k, gates[k], route)
    route_ref[...] = route


def _merge(x2d, o_list, lse_list, ret, proj, gate_col0, w_att, w_ret, w_out, g_ffn, w_router_p, b_router_p, tm=256):
    T, D = x2d.shape
    row = lambda w: pl.BlockSpec((tm, w), lambda i: (i, 0))
    full = lambda a: pl.BlockSpec(a.shape, lambda i: (0,) * a.ndim)
    gcol = lambda c: pl.BlockSpec((tm, COL_BLOCK), lambda i, c=c: (i, c))
    return pl.pallas_call(
        _merge_kernel,
        out_shape=(jax.ShapeDtypeStruct((T, D), F32), jax.ShapeDtypeStruct((T, D), F32),
                   jax.ShapeDtypeStruct((T, LANES), F32), jax.ShapeDtypeStruct((1, LANES), F32)),
        grid=(T // tm,),
        in_specs=[row(D), row(GROUP_WIDTH), row(GROUP_WIDTH), row(GROUP_WIDTH), row(LANES), row(LANES), row(LANES),
                  row(RET_HEADS * RET_DV), gcol(gate_col0), gcol(gate_col0 + 1), gcol(gate_col0 + 2),
                  gcol(gate_col0 + 3), full(w_att), full(w_ret), full(w_out), full(g_ffn), full(w_router_p),
                  full(b_router_p)],
        out_specs=(row(D), row(D), row(LANES), pl.BlockSpec((1, LANES), lambda i: (0, 0))),
        compiler_params=_cparams(("arbitrary",)),
        name="merge",
    )(x2d, *o_list, *lse_list, ret, proj, proj, proj, proj, w_att, w_ret, w_out, g_ffn, w_router_p, b_router_p)


def _row_copy(src_hbm, dst_hbm, src_row, dst_row, sem):
    return pltpu.make_async_copy(src_hbm.at[pl.ds(src_row, 1)], dst_hbm.at[pl.ds(dst_row, 1)], sem)


def _dispatch_kernel(pos_ref, hf_hbm, xs_in_hbm, xs_hbm, sem, *, tile):
    del xs_in_hbm
    base = pl.program_id(0) * tile

    def issue(i, carry):
        for k in range(TOP_K):
            _row_copy(hf_hbm, xs_hbm, base + i, pos_ref[0, k, i], sem).start()
        return carry

    lax.fori_loop(0, tile, issue, 0)

    def drain(i, carry):
        for k in range(TOP_K):
            _row_copy(hf_hbm, xs_hbm, 0, 0, sem).wait()
        return carry

    lax.fori_loop(0, tile, drain, 0)


def _dispatch(hf, pos_tiles, n_rows, tile):
    T, D = hf.shape
    xs0 = jnp.zeros((n_rows, D), hf.dtype)
    return pl.pallas_call(
        functools.partial(_dispatch_kernel, tile=tile),
        out_shape=jax.ShapeDtypeStruct((n_rows, D), hf.dtype),
        grid=(T // tile,),
        in_specs=[pl.BlockSpec((1, TOP_K, tile), lambda i: (i, 0, 0), memory_space=pltpu.SMEM),
                  pl.BlockSpec(memory_space=pl.ANY), pl.BlockSpec(memory_space=pl.ANY)],
        out_specs=pl.BlockSpec(memory_space=pl.ANY),
        scratch_shapes=[pltpu.SemaphoreType.DMA(())],
        input_output_aliases={2: 0},
        compiler_params=pltpu.CompilerParams(dimension_semantics=("arbitrary",), has_side_effects=True),
        name="dispatch",
    )(pos_tiles, hf, xs0)


def _expert_kernel(be_ref, nu_ref, x_ref, w1_ref, b1_ref, w2_ref, b2_ref, y_ref):
    del be_ref
    F = w2_ref.shape[0]

    @pl.when(pl.program_id(0) < nu_ref[0])
    def _():
        gu = jnp.dot(x_ref[...].astype(BF16), w1_ref[...], preferred_element_type=F32) + b1_ref[...]
        gate = jnp.minimum(gu[:, :F], SWIGLU_LIMIT)
        up = jnp.clip(gu[:, F:], -SWIGLU_LIMIT, SWIGLU_LIMIT)
        act = (up + 1.0) * gate * jax.nn.sigmoid(SWIGLU_ALPHA * gate)
        y_ref[...] = jnp.dot(act.astype(BF16), w2_ref[...], preferred_element_type=F32) + b2_ref[...]


def _experts(xs, blk_expert, n_used, w1, b1, w2, b2):
    P, D = xs.shape
    E, _, F2 = w1.shape
    F = w2.shape[1]
    M = EXPERT_BLOCK
    blk = lambda i, be, nu: (jnp.minimum(i, nu[0] - 1), 0)
    grid_spec = pltpu.PrefetchScalarGridSpec(
        num_scalar_prefetch=2,
        grid=(P // M,),
        in_specs=[pl.BlockSpec((M, D), blk),
                  pl.BlockSpec((None, D, F2), lambda i, be, nu: (be[i], 0, 0)),
                  pl.BlockSpec((None, 1, F2), lambda i, be, nu: (be[i], 0, 0)),
                  pl.BlockSpec((None, F, D), lambda i, be, nu: (be[i], 0, 0)),
                  pl.BlockSpec((None, 1, D), lambda i, be, nu: (be[i], 0, 0))],
        out_specs=pl.BlockSpec((M, D), blk),
    )
    return pl.pallas_call(
        _expert_kernel,
        out_shape=jax.ShapeDtypeStruct((P, D), F32),
        grid_spec=grid_spec,
        compiler_params=_cparams(("arbitrary",)),
        name="experts",
    )(blk_expert, n_used, xs, w1, b1.reshape(E, 1, F2), w2, b2.reshape(E, 1, D))


def _combine_kernel(pos_ref, route_ref, x1_ref, g_ref, ys_hbm, o_ref, buf_ref, sem, *, tile):
    def issue(i, carry):
        for k in range(TOP_K):
            pltpu.make_async_copy(ys_hbm.at[pl.ds(pos_ref[0, k, i], 1)], buf_ref.at[k, pl.ds(i, 1)], sem).start()
        return carry

    lax.fori_loop(0, tile, issue, 0)

    def drain(i, carry):
        for k in range(TOP_K):
            pltpu.make_async_copy(ys_hbm.at[pl.ds(0, 1)], buf_ref.at[k, pl.ds(0, 1)], sem).wait()
        return carry

    lax.fori_loop(0, tile, drain, 0)

    route = route_ref[...]
    acc = x1_ref[...]
    for k in range(TOP_K):
        acc = acc + route[:, ROUTE_GATE + k:ROUTE_GATE + k + 1] * buf_ref[k]
    ms = jnp.mean(acc * acc, axis=-1, keepdims=True)
    o_ref[...] = acc * lax.rsqrt(ms + NORM_EPS) * g_ref[...]


def _combine(pos_tiles, route, x1, g_final, ys, tile):
    T, D = x1.shape
    return pl.pallas_call(
        functools.partial(_combine_kernel, tile=tile),
        out_shape=jax.ShapeDtypeStruct((T, D), F32),
        grid=(T // tile,),
        in_specs=[pl.BlockSpec((1, TOP_K, tile), lambda i: (i, 0, 0), memory_space=pltpu.SMEM),
                  pl.BlockSpec((tile, LANES), lambda i: (i, 0)),
                  pl.BlockSpec((tile, D), lambda i: (i, 0)),
                  pl.BlockSpec((1, D), lambda i: (0, 0)),
                  pl.BlockSpec(memory_space=pl.ANY)],
        out_specs=pl.BlockSpec((tile, D), lambda i: (i, 0)),
        scratch_shapes=[pltpu.VMEM((TOP_K, tile, D), F32), pltpu.SemaphoreType.DMA(())],
        compiler_params=_cparams(("arbitrary",)),
        name="combine",
    )(pos_tiles, route, x1, g_final.reshape(1, D), ys)


def _layer(x2d, B, S, norm_mix_g, w_in, rel_bias, w_att, w_ret, w_out, norm_ffn_g, w_router, b_router,
           w1, b1, w2, b2):
    T, D = x2d.shape
    proj = _inproj(x2d, norm_mix_g, w_in.astype(BF16))
    n_att = N_GROUPS * GROUP_WIDTH
    proj3 = proj.reshape(B, S, proj.shape[1])

    o_list, lse_list = [], []
    for g, d in enumerate(DILATIONS):
        L = S // d
        nb = L // N_STEPS

        def gather(off, g=g, d=d, L=L):
            t = proj3[:, :, off + g * GROUP_WIDTH: off + (g + 1) * GROUP_WIDTH]
            return t.reshape(B, L, d, GROUP_WIDTH).transpose(0, 2, 1, 3)

        bias = _band_bias(rel_bias[:, g * HEADS_PER_GROUP:(g + 1) * HEADS_PER_GROUP], d)
        o, lse = _attention_group(gather(0), gather(n_att), gather(2 * n_att), bias, blocks=min(nb, 8))
        o_list.append(o.transpose(0, 2, 1, 3).reshape(T, GROUP_WIDTH))
        lse_list.append(lse.transpose(0, 2, 1, 3).reshape(T, LANES))

    ret_col0 = 3 * n_att // COL_BLOCK
    ret = _retention(proj, B, S, ret_col0)
    gate_col0 = ret_col0 + (2 * RET_HEADS * RET_DK + 2 * RET_HEADS * RET_DV) // COL_BLOCK

    w_router_p = jnp.zeros((D, LANES), BF16).at[:, :N_EXPERTS].set(w_router.astype(BF16))
    b_router_p = jnp.full((1, LANES), NEG_BIG, F32).at[0, :N_EXPERTS].set(b_router.astype(F32))
    x1, hf, route, counts = _merge(x2d, o_list, lse_list, ret, proj, gate_col0, w_att.astype(BF16),
                                   w_ret.astype(BF16), w_out.astype(BF16), norm_ffn_g.reshape(1, D),
                                   w_router_p, b_router_p)

    M = EXPERT_BLOCK
    n_blocks = (T * TOP_K) // M + N_EXPERTS
    cnt = counts[0, :N_EXPERTS].astype(jnp.int32)
    padded = ((cnt + M - 1) // M) * M
    pend = jnp.cumsum(padded)
    pstart = pend - padded
    idx = route[:, ROUTE_IDX:ROUTE_IDX + TOP_K].astype(jnp.int32)
    rank = route[:, ROUTE_RANK:ROUTE_RANK + TOP_K].astype(jnp.int32)
    pos = pstart[idx] + rank
    n_used = (pend[-1] // M).astype(jnp.int32)
    blk_expert = jnp.minimum(jnp.searchsorted(pend, jnp.arange(n_blocks) * M, side='right'), N_EXPERTS - 1)
    last_expert = blk_expert[jnp.maximum(n_used - 1, 0)]
    blk_expert = jnp.where(jnp.arange(n_blocks) < n_used, blk_expert, last_expert).astype(jnp.int32)

    tile = 256
    pos_tiles = pos.reshape(T // tile, tile, TOP_K).transpose(0, 2, 1)
    xs = _dispatch(hf, pos_tiles, n_blocks * M, tile)
    ys = _experts(xs, blk_expert, n_used.reshape(1), w1.astype(BF16), b1, w2.astype(BF16), b2)
    return pos_tiles, route, x1, ys


def kernel(x, norm_mix_g, w_in, rel_bias, w_att_branch, w_ret_branch, w_out, norm_ffn_g, w_router, b_router,
           w1, b1, w2, b2, norm_final_g):
    B, S, D = x.shape
    depth = w_in.shape[0]
    assert depth == 1, "the combine stage applies the final norm; a deeper stack needs a separate norm pass"
    x2d = x.reshape(B * S, D)
    pos_tiles, route, x1, ys = _layer(x2d, B, S, norm_mix_g[0], w_in[0], rel_bias, w_att_branch[0],
                                      w_ret_branch[0], w_out[0], norm_ffn_g[0], w_router[0], b_router[0],
                                      w1[0], b1[0], w2[0], b2[0])
    out = _combine(pos_tiles, route, x1, norm_final_g, ys, tile=256)
    return out.reshape(B, S, D)
```

```python
import functools
import math

import numpy as np
import jax
import jax.numpy as jnp
from jax import lax
from jax.experimental import pallas as pl
from jax.experimental.pallas import tpu as pltpu

F32 = jnp.float32
BF16 = jnp.bfloat16

NORM_EPS = 1e-5

HEAD_DIM = 128
HEADS_PER_GROUP = 4
GROUP_WIDTH = HEADS_PER_GROUP * HEAD_DIM
DILATIONS = (1, 4, 16)
N_STEPS = 128
N_GROUPS = len(DILATIONS)
REL_BUCKETS = 32
REL_MAX_DIST = 2048

RET_HEADS = 4
RET_DK = 128
RET_DV = 256
RET_CHUNK = 128
ROPE_BASE = 10000.0

N_EXPERTS = 32
TOP_K = 4
SWIGLU_LIMIT = 7.0
SWIGLU_ALPHA = 1.702
EXPERT_BLOCK = 256

LANES = 128
COL_BLOCK = 512
NEG_BIG = -1e30

VMEM_LIMIT = 48 * 1024 * 1024


def _cparams(sem, vmem=VMEM_LIMIT):
    return pltpu.CompilerParams(dimension_semantics=sem, vmem_limit_bytes=vmem)


def _inproj_kernel(x_ref, g_ref, w_ref, o_ref, h_ref):
    x = x_ref[...]
    ms = jnp.mean(x * x, axis=-1, keepdims=True)
    h_ref[...] = (x * lax.rsqrt(ms + NORM_EPS) * g_ref[...]).astype(BF16)
    n_chunks = o_ref.shape[1] // COL_BLOCK
    for c in range(n_chunks):
        cols = slice(c * COL_BLOCK, (c + 1) * COL_BLOCK)
        o_ref[:, cols] = jnp.dot(h_ref[...], w_ref[:, cols], preferred_element_type=F32).astype(BF16)


def _inproj(x2d, g, w_bf16, tm=512):
    T, D = x2d.shape
    N = w_bf16.shape[1]
    return pl.pallas_call(
        _inproj_kernel,
        out_shape=jax.ShapeDtypeStruct((T, N), BF16),
        grid=(T // tm,),
        in_specs=[
            pl.BlockSpec((tm, D), lambda i: (i, 0)),
            pl.BlockSpec((1, D), lambda i: (0, 0)),
            pl.BlockSpec((D, N), lambda i: (0, 0), pipeline_mode=pl.Buffered(1)),
        ],
        out_specs=pl.BlockSpec((tm, N), lambda i: (i, 0)),
        scratch_shapes=[pltpu.VMEM((tm, D), BF16)],
        compiler_params=_cparams(("arbitrary",), 56 * 1024 * 1024),
        name="inproj",
    )(x2d, g.reshape(1, D), w_bf16)


def _attn_block(q, kp, kc, vp, vc, bias_ref, has_prev):
    scale = HEAD_DIM ** -0.5
    lane = lax.broadcasted_iota(jnp.int32, (N_STEPS, LANES), 1)
    outs = []
    lse = jnp.zeros((N_STEPS, LANES), F32)
    dn = (((1,), (1,)), ((), ()))
    for h in range(HEADS_PER_GROUP):
        hs = slice(h * HEAD_DIM, (h + 1) * HEAD_DIM)
        qh = q[:, hs]
        sp = lax.dot_general(qh, kp[:, hs], dn, preferred_element_type=F32) * scale + bias_ref[h, :, :N_STEPS]
        sc = lax.dot_general(qh, kc[:, hs], dn, preferred_element_type=F32) * scale + bias_ref[h, :, N_STEPS:]
        sp = jnp.where(has_prev, sp, NEG_BIG)
        m = jnp.maximum(jnp.max(sp, axis=-1, keepdims=True), jnp.max(sc, axis=-1, keepdims=True))
        pp = jnp.exp(sp - m)
        pc = jnp.exp(sc - m)
        l = jnp.sum(pp, axis=-1, keepdims=True) + jnp.sum(pc, axis=-1, keepdims=True)
        acc = (jnp.dot(pp.astype(BF16), vp[:, hs], preferred_element_type=F32)
               + jnp.dot(pc.astype(BF16), vc[:, hs], preferred_element_type=F32))
        outs.append(acc / l)
        lse = jnp.where(lane == h, m + jnp.log(l), lse)
    return jnp.concatenate(outs, axis=-1), lse


def _attn_kernel(q_ref, k_ref, v_ref, kprev_ref, vprev_ref, bias_ref, o_ref, lse_ref, *, blocks):
    n = pl.program_id(2)

    def run(j, kp, vp, has_prev):
        rows = pl.ds(pl.multiple_of(j * N_STEPS, N_STEPS), N_STEPS)
        o, lse = _attn_block(q_ref[rows, :], kp, k_ref[rows, :], vp, v_ref[rows, :], bias_ref, has_prev)
        o_ref[rows, :] = o.astype(o_ref.dtype)
        lse_ref[rows, :] = lse

    run(0, kprev_ref[...], vprev_ref[...], n > 0)

    def body(j, carry):
        prev = pl.ds(pl.multiple_of((j - 1) * N_STEPS, N_STEPS), N_STEPS)
        run(j, k_ref[prev, :], v_ref[prev, :], True)
        return carry

    if blocks > 1:
        lax.fori_loop(1, blocks, body, 0)


def _attention_group(q, k, v, bias, blocks):
    B, d, L, W = q.shape
    rows = blocks * N_STEPS
    steps = L // rows
    cur = pl.BlockSpec((None, None, rows, W), lambda b, r, n: (b, r, n, 0))
    prev = pl.BlockSpec((None, None, N_STEPS, W), lambda b, r, n: (b, r, jnp.maximum(n * blocks - 1, 0), 0))
    return pl.pallas_call(
        functools.partial(_attn_kernel, blocks=blocks),
        out_shape=(jax.ShapeDtypeStruct((B, d, L, W), BF16), jax.ShapeDtypeStruct((B, d, L, LANES), F32)),
        grid=(B, d, steps),
        in_specs=[cur, cur, cur, prev, prev,
                  pl.BlockSpec((HEADS_PER_GROUP, N_STEPS, 2 * N_STEPS), lambda b, r, n: (0, 0, 0))],
        out_specs=(pl.BlockSpec((None, None, rows, W), lambda b, r, n: (b, r, n, 0)),
                   pl.BlockSpec((None, None, rows, LANES), lambda b, r, n: (b, r, n, 0))),
        compiler_params=_cparams(("arbitrary", "arbitrary", "arbitrary")),
        name=f"attn_d{d}",
    )(q, k, v, k, v, bias)


def _t5_bucket(dist):
    max_exact = REL_BUCKETS // 2
    d_f = jnp.maximum(dist, 1).astype(F32)
    large = max_exact + (jnp.log(d_f / max_exact) / math.log(REL_MAX_DIST / max_exact)
                         * (REL_BUCKETS - max_exact)).astype(jnp.int32)
    large = jnp.minimum(large, REL_BUCKETS - 1)
    return jnp.where(dist < max_exact, dist, large)


def _band_bias(rel_bias_g, dilation):
    qi = jnp.arange(N_STEPS)[:, None]
    kj = jnp.arange(2 * N_STEPS)[None, :]
    step_dist = qi + N_STEPS - kj
    band = (step_dist >= 0) & (step_dist <= N_STEPS)
    bias = rel_bias_g[_t5_bucket(jnp.maximum(step_dist, 0) * dilation)]
    bias = bias.astype(F32).transpose(2, 0, 1)
    return jnp.where(band[None], bias, NEG_BIG)


def _ret_kernel(q_ref, k_ref, v0_ref, v1_ref, g0_ref, g1_ref, cos_ref, sin_ref, decay_ref, xi_ref, zeta_ref,
                o_ref, state_ref, *, g_chunk):
    n = pl.program_id(1)

    @pl.when(n == 0)
    def _():
        state_ref[...] = jnp.zeros_like(state_ref)

    cos = cos_ref[...]
    sin = sin_ref[...]
    lane = lax.broadcasted_iota(jnp.int32, (RET_CHUNK, RET_DK), 1)
    even = (lane % 2) == 0

    def rotary(t):
        partner = jnp.where(even, pltpu.roll(t, RET_DK - 1, 1), pltpu.roll(t, 1, 1))
        return t * cos + partner * sin

    for h in range(RET_HEADS):
        ks = slice(h * RET_DK, (h + 1) * RET_DK)
        qr = rotary(q_ref[:, ks].astype(F32))
        kr = rotary(k_ref[:, ks].astype(F32)) * (RET_DK ** -0.5)
        v_ref = v0_ref if h < 2 else v1_ref
        g_ref = g0_ref if h < 2 else g1_ref
        vs = slice((h % 2) * RET_DV, (h % 2 + 1) * RET_DV)
        v = v_ref[:, vs]
        qb = qr.astype(BF16)
        kb = kr.astype(BF16)
        scores = lax.dot_general(qb, kb, (((1,), (1,)), ((), ())), preferred_element_type=F32) * decay_ref[h]
        inner = jnp.dot(scores.astype(BF16), v, preferred_element_type=F32)
        state = state_ref[h]
        cross = jnp.dot(qb, state.astype(BF16), preferred_element_type=F32) * xi_ref[h]
        kz = (kr * zeta_ref[h]).astype(BF16)
        state_ref[h] = state * g_chunk[h] + lax.dot_general(kz, v, (((0,), (0,)), ((), ())),
                                                            preferred_element_type=F32)
        ret = inner + cross
        mu = jnp.mean(ret, axis=-1, keepdims=True)
        cen = ret - mu
        var = jnp.mean(cen * cen, axis=-1, keepdims=True)
        normed = cen * lax.rsqrt(var + NORM_EPS)
        g = g_ref[:, vs].astype(F32)
        o_ref[:, h * RET_DV:(h + 1) * RET_DV] = (g * jax.nn.sigmoid(g) * normed).astype(o_ref.dtype)


def _retention(proj, B, S, col0):
    T = B * S
    nc = S // RET_CHUNK
    C = RET_CHUNK
    log_g = np.log(1.0 - 2.0 ** (-5.0 - np.arange(RET_HEADS, dtype=np.float64)))
    idx = np.arange(C, dtype=np.float64)
    diff = idx[:, None] - idx[None, :]
    decay = np.where(diff >= 0, np.exp(np.maximum(diff, 0.0)[None] * log_g[:, None, None]), 0.0)
    xi = np.exp((idx + 1.0)[None, :] * log_g[:, None])
    zeta = np.exp((C - 1.0 - idx)[None, :] * log_g[:, None])
    g_chunk = tuple(float(v) for v in np.exp(C * log_g))
    xi_b = np.broadcast_to(xi[:, :, None], (RET_HEADS, C, RET_DV)).astype(np.float32)
    zeta_b = np.broadcast_to(zeta[:, :, None], (RET_HEADS, C, RET_DK)).astype(np.float32)

    inv = ROPE_BASE ** (-np.arange(0, RET_DK, 2, dtype=np.float64) / RET_DK)
    ang = np.arange(S, dtype=np.float64)[:, None] * inv[None]
    cos_t = np.repeat(np.cos(ang), 2, axis=1).astype(np.float32)
    sin_t = np.stack([-np.sin(ang), np.sin(ang)], axis=-1).reshape(S, RET_DK).astype(np.float32)

    def col(c):
        return pl.BlockSpec((C, COL_BLOCK), lambda b, n, c=c: (b * nc + n, c))

    const3 = lambda shape: pl.BlockSpec(shape, lambda b, n: (0, 0, 0))
    return pl.pallas_call(
        functools.partial(_ret_kernel, g_chunk=g_chunk),
        out_shape=jax.ShapeDtypeStruct((T, RET_HEADS * RET_DV), BF16),
        grid=(B, nc),
        in_specs=[col(col0), col(col0 + 1), col(col0 + 2), col(col0 + 3), col(col0 + 4), col(col0 + 5),
                  pl.BlockSpec((C, RET_DK), lambda b, n: (n, 0)),
                  pl.BlockSpec((C, RET_DK), lambda b, n: (n, 0)),
                  const3((RET_HEADS, C, C)), const3((RET_HEADS, C, RET_DV)), const3((RET_HEADS, C, RET_DK))],
        out_specs=pl.BlockSpec((C, RET_HEADS * RET_DV), lambda b, n: (b * nc + n, 0)),
        scratch_shapes=[pltpu.VMEM((RET_HEADS, RET_DK, RET_DV), F32)],
        compiler_params=_cparams(("arbitrary", "arbitrary")),
        name="retention",
    )(proj, proj, proj, proj, proj, proj, jnp.asarray(cos_t), jnp.asarray(sin_t),
      jnp.asarray(decay.astype(np.float32)), jnp.asarray(xi_b), jnp.asarray(zeta_b))


ROUTE_IDX, ROUTE_RANK, ROUTE_GATE = 0, 4, 8


def _merge_kernel(x_ref, o1_ref, o2_ref, o3_ref, l1_ref, l2_ref, l3_ref, ret_ref,
                  ga0_ref, ga1_ref, gr0_ref, gr1_ref, watt_ref, wret_ref, wout_ref, gffn_ref, wr_ref, br_ref,
                  x1_ref, hf_ref, route_ref, counts_ref):
    tm = x_ref.shape[0]
    o_refs = (o1_ref, o2_ref, o3_ref)
    lses = [r[...] for r in (l1_ref, l2_ref, l3_ref)]

    att_parts = []
    for h in range(HEADS_PER_GROUP):
        ls = [l[:, h:h + 1] for l in lses]
        m = jnp.maximum(jnp.maximum(ls[0], ls[1]), ls[2])
        ws = [jnp.exp(l - m) for l in ls]
        wsum = ws[0] + ws[1] + ws[2]
        hs = slice(h * HEAD_DIM, (h + 1) * HEAD_DIM)
        acc = (ws[0] / wsum) * o_refs[0][:, hs].astype(F32)
        acc = acc + (ws[1] / wsum) * o_refs[1][:, hs].astype(F32)
        acc = acc + (ws[2] / wsum) * o_refs[2][:, hs].astype(F32)
        att_parts.append(acc)
    att = jnp.concatenate(att_parts, axis=-1).astype(BF16)

    a_proj = jnp.dot(att, watt_ref[...], preferred_element_type=F32)
    r_proj = jnp.dot(ret_ref[...], wret_ref[...], preferred_element_type=F32)
    gate_a = jnp.concatenate([ga0_ref[...], ga1_ref[...]], axis=-1).astype(F32)
    gate_r = jnp.concatenate([gr0_ref[...], gr1_ref[...]], axis=-1).astype(F32)
    merged = jax.nn.sigmoid(gate_a) * a_proj + jax.nn.sigmoid(gate_r) * r_proj
    x1 = x_ref[...] + jnp.dot(merged.astype(BF16), wout_ref[...], preferred_element_type=F32)
    x1_ref[...] = x1

    ms = jnp.mean(x1 * x1, axis=-1, keepdims=True)
    hf = x1 * lax.rsqrt(ms + NORM_EPS) * gffn_ref[...]
    hf_ref[...] = hf

    logits = jnp.dot(hf.astype(BF16), wr_ref[...], preferred_element_type=F32) + br_ref[...]
    lane = lax.broadcasted_iota(jnp.int32, (tm, LANES), 1)
    lane_f = lane.astype(F32)
    work = logits
    vals, idxs, hits = [], [], []
    for _ in range(TOP_K):
        mk = jnp.max(work, axis=-1, keepdims=True)
        ik = jnp.min(jnp.where(work == mk, lane_f, float(LANES)), axis=-1, keepdims=True)
        hit = lane_f == ik
        work = jnp.where(hit, -jnp.inf, work)
        vals.append(mk)
        idxs.append(ik)
        hits.append(hit)
    es = [jnp.exp(v - vals[0]) for v in vals]
    esum = es[0] + es[1] + es[2] + es[3]
    gates = [e / esum for e in es]

    sel = jnp.zeros((tm, LANES), F32)
    for hit in hits:
        sel = jnp.where(hit, 1.0, sel)

    @pl.when(pl.program_id(0) == 0)
    def _():
        counts_ref[...] = jnp.zeros_like(counts_ref)

    row = lax.broadcasted_iota(jnp.int32, (tm, tm), 0)
    colm = lax.broadcasted_iota(jnp.int32, (tm, tm), 1)
    below = jnp.where(colm < row, 1.0, 0.0).astype(BF16)
    rank = jnp.dot(below, sel.astype(BF16), preferred_element_type=F32) + counts_ref[...]
    counts_ref[...] = counts_ref[...] + jnp.sum(sel, axis=0, keepdims=True)

    route = jnp.zeros((tm, LANES), F32)
    for k in range(TOP_K):
        rank_k = jnp.sum(jnp.where(hits[k], rank, 0.0), axis=-1, keepdims=True)
        route = jnp.where(lane == ROUTE_IDX + k, idxs[k], route)
        route = jnp.where(lane == ROUTE_RANK + k, rank_k, route)
        route = jnp.where(lane == ROUTE_GATE + k, gates[k], route)
    route_ref[...] = route


def _merge(x2d, o_list, lse_list, ret, proj, gate_col0, w_att, w_ret, w_out, g_ffn, w_router_p, b_router_p, tm=256):
    T, D = x2d.shape
    row = lambda w: pl.BlockSpec((tm, w), lambda i: (i, 0))
    full = lambda a: pl.BlockSpec(a.shape, lambda i: (0,) * a.ndim)
    gcol = lambda c: pl.BlockSpec((tm, COL_BLOCK), lambda i, c=c: (i, c))
    return pl.pallas_call(
        _merge_kernel,
        out_shape=(jax.ShapeDtypeStruct((T, D), F32), jax.ShapeDtypeStruct((T, D), F32),
                   jax.ShapeDtypeStruct((T, LANES), F32), jax.ShapeDtypeStruct((1, LANES), F32)),
        grid=(T // tm,),
        in_specs=[row(D), row(GROUP_WIDTH), row(GROUP_WIDTH), row(GROUP_WIDTH), row(LANES), row(LANES), row(LANES),
                  row(RET_HEADS * RET_DV), gcol(gate_col0), gcol(gate_col0 + 1), gcol(gate_col0 + 2),
                  gcol(gate_col0 + 3), full(w_att), full(w_ret), full(w_out), full(g_ffn), full(w_router_p),
                  full(b_router_p)],
        out_specs=(row(D), row(D), row(LANES), pl.BlockSpec((1, LANES), lambda i: (0, 0))),
        compiler_params=_cparams(("arbitrary",)),
        name="merge",
    )(x2d, *o_list, *lse_list, ret, proj, proj, proj, proj, w_att, w_ret, w_out, g_ffn, w_router_p, b_router_p)


def _row_copy(src, dst, src_row, dst_row, sem):
    return pltpu.make_async_copy(src.at[pl.ds(src_row, 1)], dst.at[pl.ds(dst_row, 1)], sem)


def _dispatch_kernel(pos_ref, hf_ref, xs_in_hbm, xs_hbm, sem, *, tile):
    del xs_in_hbm

    def issue(i, carry):
        for k in range(TOP_K):
            _row_copy(hf_ref, xs_hbm, i, pos_ref[0, k, i], sem).start(priority=k % 2)
        return carry

    lax.fori_loop(0, tile, issue, 0, unroll=8)

    def drain(i, carry):
        for k in range(TOP_K):
            _row_copy(hf_ref, xs_hbm, 0, 0, sem).wait()
        return carry

    lax.fori_loop(0, tile, drain, 0, unroll=8)


def _dispatch(hf, pos_tiles, n_rows, tile):
    T, D = hf.shape
    xs0 = jnp.zeros((n_rows, D), hf.dtype)
    return pl.pallas_call(
        functools.partial(_dispatch_kernel, tile=tile),
        out_shape=jax.ShapeDtypeStruct((n_rows, D), hf.dtype),
        grid=(T // tile,),
        in_specs=[pl.BlockSpec((1, TOP_K, tile), lambda i: (i, 0, 0), memory_space=pltpu.SMEM),
                  pl.BlockSpec((tile, D), lambda i: (i, 0)), pl.BlockSpec(memory_space=pl.ANY)],
        out_specs=pl.BlockSpec(memory_space=pl.ANY),
        scratch_shapes=[pltpu.SemaphoreType.DMA(())],
        input_output_aliases={2: 0},
        compiler_params=_cparams(("arbitrary",)),
        name="dispatch",
    )(pos_tiles, hf, xs0)


def _expert_kernel(be_ref, nu_ref, x_ref, w1_ref, b1_ref, w2_ref, b2_ref, y_ref):
    del be_ref
    F = w2_ref.shape[0]

    @pl.when(pl.program_id(0) < nu_ref[0])
    def _():
        gu = jnp.dot(x_ref[...].astype(BF16), w1_ref[...], preferred_element_type=F32) + b1_ref[...]
        gate = jnp.minimum(gu[:, :F], SWIGLU_LIMIT)
        up = jnp.clip(gu[:, F:], -SWIGLU_LIMIT, SWIGLU_LIMIT)
        act = (up + 1.0) * gate * jax.nn.sigmoid(SWIGLU_ALPHA * gate)
        y_ref[...] = jnp.dot(act.astype(BF16), w2_ref[...], preferred_element_type=F32) + b2_ref[...]


def _experts(xs, blk_expert, n_used, w1, b1, w2, b2):
    P, D = xs.shape
    E, _, F2 = w1.shape
    F = w2.shape[1]
    M = EXPERT_BLOCK
    blk = lambda i, be, nu: (jnp.minimum(i, nu[0] - 1), 0)
    grid_spec = pltpu.PrefetchScalarGridSpec(
        num_scalar_prefetch=2,
        grid=(P // M,),
        in_specs=[pl.BlockSpec((M, D), blk),
                  pl.BlockSpec((None, D, F2), lambda i, be, nu: (be[i], 0, 0)),
                  pl.BlockSpec((None, 1, F2), lambda i, be, nu: (be[i], 0, 0)),
                  pl.BlockSpec((None, F, D), lambda i, be, nu: (be[i], 0, 0)),
                  pl.BlockSpec((None, 1, D), lambda i, be, nu: (be[i], 0, 0))],
        out_specs=pl.BlockSpec((M, D), blk),
    )
    return pl.pallas_call(
        _expert_kernel,
        out_shape=jax.ShapeDtypeStruct((P, D), F32),
        grid_spec=grid_spec,
        compiler_params=_cparams(("arbitrary",)),
        name="experts",
    )(blk_expert, n_used, xs, w1, b1.reshape(E, 1, F2), w2, b2.reshape(E, 1, D))


def _combine_kernel(pos_ref, route_ref, x1_ref, g_ref, ys_hbm, o_ref, buf_ref, sem, *, tile):
    def issue(i, carry):
        for k in range(TOP_K):
            pltpu.make_async_copy(ys_hbm.at[pl.ds(pos_ref[0, k, i], 1)], buf_ref.at[k, pl.ds(i, 1)],
                                  sem).start(priority=k % 2)
        return carry

    lax.fori_loop(0, tile, issue, 0, unroll=8)

    def drain(i, carry):
        for k in range(TOP_K):
            pltpu.make_async_copy(ys_hbm.at[pl.ds(0, 1)], buf_ref.at[k, pl.ds(0, 1)], sem).wait()
        return carry

    lax.fori_loop(0, tile, drain, 0, unroll=8)

    route = route_ref[...]
    acc = x1_ref[...]
    for k in range(TOP_K):
        acc = acc + route[:, ROUTE_GATE + k:ROUTE_GATE + k + 1] * buf_ref[k]
    ms = jnp.mean(acc * acc, axis=-1, keepdims=True)
    o_ref[...] = acc * lax.rsqrt(ms + NORM_EPS) * g_ref[...]


def _combine(pos_tiles, route, x1, g_final, ys, tile):
    T, D = x1.shape
    return pl.pallas_call(
        functools.partial(_combine_kernel, tile=tile),
        out_shape=jax.ShapeDtypeStruct((T, D), F32),
        grid=(T // tile,),
        in_specs=[pl.BlockSpec((1, TOP_K, tile), lambda i: (i, 0, 0), memory_space=pltpu.SMEM),
                  pl.BlockSpec((tile, LANES), lambda i: (i, 0)),
                  pl.BlockSpec((tile, D), lambda i: (i, 0)),
                  pl.BlockSpec((1, D), lambda i: (0, 0)),
                  pl.BlockSpec(memory_space=pl.ANY)],
        out_specs=pl.BlockSpec((tile, D), lambda i: (i, 0)),
        scratch_shapes=[pltpu.VMEM((TOP_K, tile, D), F32), pltpu.SemaphoreType.DMA(())],
        compiler_params=_cparams(("arbitrary",)),
        name="combine",
    )(pos_tiles, route, x1, g_final.reshape(1, D), ys)


def _layer(x2d, B, S, norm_mix_g, w_in, rel_bias, w_att, w_ret, w_out, norm_ffn_g, w_router, b_router,
           w1, b1, w2, b2):
    T, D = x2d.shape
    proj = _inproj(x2d, norm_mix_g, w_in.astype(BF16))
    n_att = N_GROUPS * GROUP_WIDTH
    proj3 = proj.reshape(B, S, proj.shape[1])

    o_list, lse_list = [], []
    for g, d in enumerate(DILATIONS):
        L = S // d
        nb = L // N_STEPS

        def gather(off, g=g, d=d, L=L):
            t = proj3[:, :, off + g * GROUP_WIDTH: off + (g + 1) * GROUP_WIDTH]
            return t.reshape(B, L, d, GROUP_WIDTH).transpose(0, 2, 1, 3)

        bias = _band_bias(rel_bias[:, g * HEADS_PER_GROUP:(g + 1) * HEADS_PER_GROUP], d)
        o, lse = _attention_group(gather(0), gather(n_att), gather(2 * n_att), bias, blocks=min(nb, 8))
        o_list.append(o.transpose(0, 2, 1, 3).reshape(T, GROUP_WIDTH))
        lse_list.append(lse.transpose(0, 2, 1, 3).reshape(T, LANES))

    ret_col0 = 3 * n_att // COL_BLOCK
    ret = _retention(proj, B, S, ret_col0)
    gate_col0 = ret_col0 + (2 * RET_HEADS * RET_DK + 2 * RET_HEADS * RET_DV) // COL_BLOCK

    w_router_p = jnp.zeros((D, LANES), BF16).at[:, :N_EXPERTS].set(w_router.astype(BF16))
    b_router_p = jnp.full((1, LANES), NEG_BIG, F32).at[0, :N_EXPERTS].set(b_router.astype(F32))
    x1, hf, route, counts = _merge(x2d, o_list, lse_list, ret, proj, gate_col0, w_att.astype(BF16),
                                   w_ret.astype(BF16), w_out.astype(BF16), norm_ffn_g.reshape(1, D),
                                   w_router_p, b_router_p)

    M = EXPERT_BLOCK
    n_blocks = (T * TOP_K) // M + N_EXPERTS
    cnt = counts[0, :N_EXPERTS].astype(jnp.int32)
    padded = ((cnt + M - 1) // M) * M
    pend = jnp.cumsum(padded)
    pstart = pend - padded
    idx = route[:, ROUTE_IDX:ROUTE_IDX + TOP_K].astype(jnp.int32)
    rank = route[:, ROUTE_RANK:ROUTE_RANK + TOP_K].astype(jnp.int32)
    pos = pstart[idx] + rank
    n_used = (pend[-1] // M).astype(jnp.int32)
    blk_expert = jnp.minimum(jnp.searchsorted(pend, jnp.arange(n_blocks) * M, side='right'), N_EXPERTS - 1)
    last_expert = blk_expert[jnp.maximum(n_used - 1, 0)]
    blk_expert = jnp.where(jnp.arange(n_blocks) < n_used, blk_expert, last_expert).astype(jnp.int32)

    tile = 256
    pos_tiles = pos.reshape(T // tile, tile, TOP_K).transpose(0, 2, 1)
    xs = _dispatch(hf, pos_tiles, n_blocks * M, tile)
    ys = _experts(xs, blk_expert, n_used.reshape(1), w1.astype(BF16), b1, w2.astype(BF16), b2)
    return pos_tiles, route, x1, ys


def kernel(x, norm_mix_g, w_in, rel_bias, w_att_branch, w_ret_branch, w_out, norm_ffn_g, w_router, b_router,
           w1, b1, w2, b2, norm_final_g):
    B, S, D = x.shape
    depth = w_in.shape[0]
    assert depth == 1, "the combine stage applies the final norm; a deeper stack needs a separate norm pass"
    x2d = x.reshape(B * S, D)
    pos_tiles, route, x1, ys = _layer(x2d, B, S, norm_mix_g[0], w_in[0], rel_bias, w_att_branch[0],
                                      w_ret_branch[0], w_out[0], norm_ffn_g[0], w_router[0], b_router[0],
                                      w1[0], b1[0], w2[0], b2[0])
    out = _combine(pos_tiles, route, x1, norm_final_g, ys, tile=256)
    return out.reshape(B, S, D)
```

```python
import functools
import math

import numpy as np
import jax
import jax.numpy as jnp
from jax import lax
from jax.experimental import pallas as pl
from jax.experimental.pallas import tpu as pltpu

F32 = jnp.float32
BF16 = jnp.bfloat16

NORM_EPS = 1e-5

HEAD_DIM = 128
HEADS_PER_GROUP = 4
GROUP_WIDTH = HEADS_PER_GROUP * HEAD_DIM
DILATIONS = (1, 4, 16)
N_STEPS = 128
N_GROUPS = len(DILATIONS)
REL_BUCKETS = 32
REL_MAX_DIST = 2048

RET_HEADS = 4
RET_DK = 128
RET_DV = 256
RET_CHUNK = 128
ROPE_BASE = 10000.0

N_EXPERTS = 32
TOP_K = 4
SWIGLU_LIMIT = 7.0
SWIGLU_ALPHA = 1.702
EXPERT_BLOCK = 256

LANES = 128
COL_BLOCK = 512
NEG_BIG = -1e30

VMEM_LIMIT = 48 * 1024 * 1024


def _cparams(sem, vmem=VMEM_LIMIT):
    return pltpu.CompilerParams(dimension_semantics=sem, vmem_limit_bytes=vmem)


def _inproj_kernel(x_ref, g_ref, w_ref, o_ref, h_ref):
    x = x_ref[...]
    ms = jnp.mean(x * x, axis=-1, keepdims=True)
    h_ref[...] = (x * lax.rsqrt(ms + NORM_EPS) * g_ref[...]).astype(BF16)
    n_chunks = o_ref.shape[1] // COL_BLOCK
    for c in range(n_chunks):
        cols = slice(c * COL_BLOCK, (c + 1) * COL_BLOCK)
        o_ref[:, cols] = jnp.dot(h_ref[...], w_ref[:, cols], preferred_element_type=F32).astype(BF16)


def _inproj(x2d, g, w_bf16, tm=512):
    T, D = x2d.shape
    N = w_bf16.shape[1]
    return pl.pallas_call(
        _inproj_kernel,
        out_shape=jax.ShapeDtypeStruct((T, N), BF16),
        grid=(T // tm,),
        in_specs=[
            pl.BlockSpec((tm, D), lambda i: (i, 0)),
            pl.BlockSpec((1, D), lambda i: (0, 0)),
            pl.BlockSpec((D, N), lambda i: (0, 0), pipeline_mode=pl.Buffered(1)),
        ],
        out_specs=pl.BlockSpec((tm, N), lambda i: (i, 0)),
        scratch_shapes=[pltpu.VMEM((tm, D), BF16)],
        compiler_params=_cparams(("arbitrary",), 56 * 1024 * 1024),
        name="inproj",
    )(x2d, g.reshape(1, D), w_bf16)


def _attn_block(q, kp, kc, vp, vc, bias_ref, has_prev):
    scale = HEAD_DIM ** -0.5
    heads = range(HEADS_PER_GROUP)
    hs = [slice(h * HEAD_DIM, (h + 1) * HEAD_DIM) for h in heads]
    rs = [slice(h * N_STEPS, (h + 1) * N_STEPS) for h in heads]
    k = jnp.concatenate([kp, kc], axis=0)
    v = jnp.concatenate([vp, vc], axis=0)
    dn = (((1,), (1,)), ((), ()))
    s = jnp.concatenate([lax.dot_general(q[:, hs[h]], k[:, hs[h]], dn, preferred_element_type=F32)
                         for h in heads], axis=0)
    s = s * scale + bias_ref[...]
    if has_prev is not True:
        col = lax.broadcasted_iota(jnp.int32, s.shape, 1)
        s = jnp.where((col >= N_STEPS) | has_prev, s, NEG_BIG)
    m = jnp.max(s, axis=-1, keepdims=True)
    p = jnp.exp(s - m)
    l = jnp.sum(p, axis=-1, keepdims=True)
    pb = p.astype(BF16)
    lse_rows = m + jnp.log(l)
    lane = lax.broadcasted_iota(jnp.int32, (N_STEPS, LANES), 1)
    outs = []
    lse = jnp.zeros((N_STEPS, LANES), F32)
    for h in heads:
        acc = jnp.dot(pb[rs[h], :], v[:, hs[h]], preferred_element_type=F32)
        outs.append(acc / l[rs[h], :])
        lse = jnp.where(lane == h, lse_rows[rs[h], :], lse)
    return jnp.concatenate(outs, axis=-1), lse


def _attn_kernel(q_ref, k_ref, v_ref, kprev_ref, vprev_ref, bias_ref, o_ref, lse_ref, *, blocks):
    n = pl.program_id(2)

    def run(j, kp, vp, has_prev):
        rows = pl.ds(pl.multiple_of(j * N_STEPS, N_STEPS), N_STEPS)
        o, lse = _attn_block(q_ref[rows, :], kp, k_ref[rows, :], vp, v_ref[rows, :], bias_ref, has_prev)
        o_ref[rows, :] = o.astype(o_ref.dtype)
        lse_ref[rows, :] = lse

    run(0, kprev_ref[...], vprev_ref[...], n > 0)

    def body(j, carry):
        prev = pl.ds(pl.multiple_of((j - 1) * N_STEPS, N_STEPS), N_STEPS)
        run(j, k_ref[prev, :], v_ref[prev, :], True)
        return carry

    if blocks > 1:
        lax.fori_loop(1, blocks, body, 0, unroll=True)


def _attention_group(q, k, v, cols, bias, blocks):
    B, d, L, _ = q.shape
    W = GROUP_WIDTH
    rows = blocks * N_STEPS
    steps = L // rows

    def cur(c):
        return pl.BlockSpec((None, None, rows, W), lambda b, r, n: (b, r, n, c))

    def prev(c):
        return pl.BlockSpec((None, None, N_STEPS, W), lambda b, r, n: (b, r, jnp.maximum(n * blocks - 1, 0), c))

    return pl.pallas_call(
        functools.partial(_attn_kernel, blocks=blocks),
        out_shape=(jax.ShapeDtypeStruct((B, d, L, W), BF16), jax.ShapeDtypeStruct((B, d, L, LANES), F32)),
        grid=(B, d, steps),
        in_specs=[cur(cols[0]), cur(cols[1]), cur(cols[2]), prev(cols[1]), prev(cols[2]),
                  pl.BlockSpec((HEADS_PER_GROUP * N_STEPS, 2 * N_STEPS), lambda b, r, n: (0, 0))],
        out_specs=(pl.BlockSpec((None, None, rows, W), lambda b, r, n: (b, r, n, 0)),
                   pl.BlockSpec((None, None, rows, LANES), lambda b, r, n: (b, r, n, 0))),
        compiler_params=_cparams(("arbitrary", "arbitrary", "arbitrary")),
        name=f"attn_d{d}",
    )(q, k, v, k, v, bias)


def _t5_bucket(dist):
    max_exact = REL_BUCKETS // 2
    d_f = jnp.maximum(dist, 1).astype(F32)
    large = max_exact + (jnp.log(d_f / max_exact) / math.log(REL_MAX_DIST / max_exact)
                         * (REL_BUCKETS - max_exact)).astype(jnp.int32)
    large = jnp.minimum(large, REL_BUCKETS - 1)
    return jnp.where(dist < max_exact, dist, large)


def _band_bias(rel_bias_g, dilation):
    qi = jnp.arange(N_STEPS)[:, None]
    kj = jnp.arange(2 * N_STEPS)[None, :]
    step_dist = qi + N_STEPS - kj
    band = (step_dist >= 0) & (step_dist <= N_STEPS)
    bias = rel_bias_g[_t5_bucket(jnp.maximum(step_dist, 0) * dilation)]
    bias = bias.astype(F32).transpose(2, 0, 1)
    return jnp.where(band[None], bias, NEG_BIG).reshape(HEADS_PER_GROUP * N_STEPS, 2 * N_STEPS)


def _ret_kernel(q_ref, k_ref, v0_ref, v1_ref, g0_ref, g1_ref, cos_ref, sin_ref, decay_ref, xi_ref, zeta_ref,
                o_ref, state_ref, *, g_chunk):
    n = pl.program_id(1)

    @pl.when(n == 0)
    def _():
        state_ref[...] = jnp.zeros_like(state_ref)

    cos = cos_ref[...]
    sin = sin_ref[...]
    lane = lax.broadcasted_iota(jnp.int32, (RET_CHUNK, RET_DK), 1)
    even = (lane % 2) == 0

    def rotary(t):
        partner = jnp.where(even, pltpu.roll(t, RET_DK - 1, 1), pltpu.roll(t, 1, 1))
        return t * cos + partner * sin

    for h in range(RET_HEADS):
        ks = slice(h * RET_DK, (h + 1) * RET_DK)
        qr = rotary(q_ref[:, ks].astype(F32))
        kr = rotary(k_ref[:, ks].astype(F32)) * (RET_DK ** -0.5)
        v_ref = v0_ref if h < 2 else v1_ref
        g_ref = g0_ref if h < 2 else g1_ref
        vs = slice((h % 2) * RET_DV, (h % 2 + 1) * RET_DV)
        v = v_ref[:, vs]
        qb = qr.astype(BF16)
        kb = kr.astype(BF16)
        scores = lax.dot_general(qb, kb, (((1,), (1,)), ((), ())), preferred_element_type=F32) * decay_ref[h]
        inner = jnp.dot(scores.astype(BF16), v, preferred_element_type=F32)
        state = state_ref[h]
        cross = jnp.dot(qb, state.astype(BF16), preferred_element_type=F32) * xi_ref[h]
        kz = (kr * zeta_ref[h]).astype(BF16)
        state_ref[h] = state * g_chunk[h] + lax.dot_general(kz, v, (((0,), (0,)), ((), ())),
                                                            preferred_element_type=F32)
        ret = inner + cross
        mu = jnp.mean(ret, axis=-1, keepdims=True)
        cen = ret - mu
        var = jnp.mean(cen * cen, axis=-1, keepdims=True)
        normed = cen * lax.rsqrt(var + NORM_EPS)
        g = g_ref[:, vs].astype(F32)
        o_ref[:, h * RET_DV:(h + 1) * RET_DV] = (g * jax.nn.sigmoid(g) * normed).astype(o_ref.dtype)


def _retention(proj, B, S, col0):
    T = B * S
    nc = S // RET_CHUNK
    C = RET_CHUNK
    log_g = np.log(1.0 - 2.0 ** (-5.0 - np.arange(RET_HEADS, dtype=np.float64)))
    idx = np.arange(C, dtype=np.float64)
    diff = idx[:, None] - idx[None, :]
    decay = np.where(diff >= 0, np.exp(np.maximum(diff, 0.0)[None] * log_g[:, None, None]), 0.0)
    xi = np.exp((idx + 1.0)[None, :] * log_g[:, None])
    zeta = np.exp((C - 1.0 - idx)[None, :] * log_g[:, None])
    g_chunk = tuple(float(v) for v in np.exp(C * log_g))
    xi_b = np.broadcast_to(xi[:, :, None], (RET_HEADS, C, RET_DV)).astype(np.float32)
    zeta_b = np.broadcast_to(zeta[:, :, None], (RET_HEADS, C, RET_DK)).astype(np.float32)

    inv = ROPE_BASE ** (-np.arange(0, RET_DK, 2, dtype=np.float64) / RET_DK)
    ang = np.arange(S, dtype=np.float64)[:, None] * inv[None]
    cos_t = np.repeat(np.cos(ang), 2, axis=1).astype(np.float32)
    sin_t = np.stack([-np.sin(ang), np.sin(ang)], axis=-1).reshape(S, RET_DK).astype(np.float32)

    def col(c):
        return pl.BlockSpec((C, COL_BLOCK), lambda b, n, c=c: (b * nc + n, c))

    const3 = lambda shape: pl.BlockSpec(shape, lambda b, n: (0, 0, 0))
    return pl.pallas_call(
        functools.partial(_ret_kernel, g_chunk=g_chunk),
        out_shape=jax.ShapeDtypeStruct((T, RET_HEADS * RET_DV), BF16),
        grid=(B, nc),
        in_specs=[col(col0), col(col0 + 1), col(col0 + 2), col(col0 + 3), col(col0 + 4), col(col0 + 5),
                  pl.BlockSpec((C, RET_DK), lambda b, n: (n, 0)),
                  pl.BlockSpec((C, RET_DK), lambda b, n: (n, 0)),
                  const3((RET_HEADS, C, C)), const3((RET_HEADS, C, RET_DV)), const3((RET_HEADS, C, RET_DK))],
        out_specs=pl.BlockSpec((C, RET_HEADS * RET_DV), lambda b, n: (b * nc + n, 0)),
        scratch_shapes=[pltpu.VMEM((RET_HEADS, RET_DK, RET_DV), F32)],
        compiler_params=_cparams(("arbitrary", "arbitrary")),
        name="retention",
    )(proj, proj, proj, proj, proj, proj, jnp.asarray(cos_t), jnp.asarray(sin_t),
      jnp.asarray(decay.astype(np.float32)), jnp.asarray(xi_b), jnp.asarray(zeta_b))


ROUTE_IDX, ROUTE_RANK, ROUTE_GATE = 0, 4, 8


def _merge_kernel(x_ref, o1_ref, o2_ref, o3_ref, l1_ref, l2_ref, l3_ref, ret_ref,
                  ga0_ref, ga1_ref, gr0_ref, gr1_ref, watt_ref, wret_ref, wout_ref, gffn_ref, wr_ref, br_ref,
                  x1_ref, hf_ref, route_ref, counts_ref):
    tm = x_ref.shape[0]
    o_refs = (o1_ref, o2_ref, o3_ref)
    lses = [r[...] for r in (l1_ref, l2_ref, l3_ref)]

    att_parts = []
    for h in range(HEADS_PER_GROUP):
        ls = [l[:, h:h + 1] for l in lses]
        m = jnp.maximum(jnp.maximum(ls[0], ls[1]), ls[2])
        ws = [jnp.exp(l - m) for l in ls]
        wsum = ws[0] + ws[1] + ws[2]
        hs = slice(h * HEAD_DIM, (h + 1) * HEAD_DIM)
        acc = (ws[0] / wsum) * o_refs[0][:, hs].astype(F32)
        acc = acc + (ws[1] / wsum) * o_refs[1][:, hs].astype(F32)
        acc = acc + (ws[2] / wsum) * o_refs[2][:, hs].astype(F32)
        att_parts.append(acc)
    att = jnp.concatenate(att_parts, axis=-1).astype(BF16)

    a_proj = jnp.dot(att, watt_ref[...], preferred_element_type=F32)
    r_proj = jnp.dot(ret_ref[...], wret_ref[...], preferred_element_type=F32)
    gate_a = jnp.concatenate([ga0_ref[...], ga1_ref[...]], axis=-1).astype(F32)
    gate_r = jnp.concatenate([gr0_ref[...], gr1_ref[...]], axis=-1).astype(F32)
    merged = jax.nn.sigmoid(gate_a) * a_proj + jax.nn.sigmoid(gate_r) * r_proj
    x1 = x_ref[...] + jnp.dot(merged.astype(BF16), wout_ref[...], preferred_element_type=F32)
    x1_ref[...] = x1

    ms = jnp.mean(x1 * x1, axis=-1, keepdims=True)
    hf = x1 * lax.rsqrt(ms + NORM_EPS) * gffn_ref[...]
    hf_ref[...] = hf

    logits = jnp.dot(hf.astype(BF16), wr_ref[...], preferred_element_type=F32) + br_ref[...]
    lane = lax.broadcasted_iota(jnp.int32, (tm, LANES), 1)
    lane_f = lane.astype(F32)
    work = logits
    vals, idxs, hits = [], [], []
    for _ in range(TOP_K):
        mk = jnp.max(work, axis=-1, keepdims=True)
        ik = jnp.min(jnp.where(work == mk, lane_f, float(LANES)), axis=-1, keepdims=True)
        hit = lane_f == ik
        work = jnp.where(hit, -jnp.inf, work)
        vals.append(mk)
        idxs.append(ik)
        hits.append(hit)
    es = [jnp.exp(v - vals[0]) for v in vals]
    esum = es[0] + es[1] + es[2] + es[3]
    gates = [e / esum for e in es]

    sel = jnp.zeros((tm, LANES), F32)
    for hit in hits:
        sel = jnp.where(hit, 1.0, sel)

    @pl.when(pl.program_id(0) == 0)
    def _():
        counts_ref[...] = jnp.zeros_like(counts_ref)

    row = lax.broadcasted_iota(jnp.int32, (tm, tm), 0)
    colm = lax.broadcasted_iota(jnp.int32, (tm, tm), 1)
    below = jnp.where(colm < row, 1.0, 0.0).astype(BF16)
    rank = jnp.dot(below, sel.astype(BF16), preferred_element_type=F32) + counts_ref[...]
    counts_ref[...] = counts_ref[...] + jnp.sum(sel, axis=0, keepdims=True)

    route = jnp.zeros((tm, LANES), F32)
    for k in range(TOP_K):
        rank_k = jnp.sum(jnp.where(hits[k], rank, 0.0), axis=-1, keepdims=True)
        route = jnp.where(lane == ROUTE_IDX + k, idxs[k], route)
        route = jnp.where(lane == ROUTE_RANK + k, rank_k, route)
        route = jnp.where(lane == ROUTE_GATE + k, gates[k], route)
    route_ref[...] = route


def _merge(x2d, o_list, lse_list, ret, proj, gate_col0, w_att, w_ret, w_out, g_ffn, w_router_p, b_router_p, tm=256):
    T, D = x2d.shape
    row = lambda w: pl.BlockSpec((tm, w), lambda i: (i, 0))
    full = lambda a: pl.BlockSpec(a.shape, lambda i: (0,) * a.ndim)
    gcol = lambda c: pl.BlockSpec((tm, COL_BLOCK), lambda i, c=c: (i, c))
    return pl.pallas_call(
        _merge_kernel,
        out_shape=(jax.ShapeDtypeStruct((T, D), F32), jax.ShapeDtypeStruct((T, D), F32),
                   jax.ShapeDtypeStruct((T, LANES), F32), jax.ShapeDtypeStruct((1, LANES), F32)),
        grid=(T // tm,),
        in_specs=[row(D), row(GROUP_WIDTH), row(GROUP_WIDTH), row(GROUP_WIDTH), row(LANES), row(LANES), row(LANES),
                  row(RET_HEADS * RET_DV), gcol(gate_col0), gcol(gate_col0 + 1), gcol(gate_col0 + 2),
                  gcol(gate_col0 + 3), full(w_att), full(w_ret), full(w_out), full(g_ffn), full(w_router_p),
                  full(b_router_p)],
        out_specs=(row(D), row(D), row(LANES), pl.BlockSpec((1, LANES), lambda i: (0, 0))),
        compiler_params=_cparams(("arbitrary",)),
        name="merge",
    )(x2d, *o_list, *lse_list, ret, proj, proj, proj, proj, w_att, w_ret, w_out, g_ffn, w_router_p, b_router_p)


def _row_copy(src, dst, src_row, dst_row, sem):
    return pltpu.make_async_copy(src.at[pl.ds(src_row, 1)], dst.at[pl.ds(dst_row, 1)], sem)


def _dispatch_kernel(pos_ref, hf_ref, xs_in_hbm, xs_hbm, sem, *, tile):
    del xs_in_hbm

    def issue(i, carry):
        for k in range(TOP_K):
            _row_copy(hf_ref, xs_hbm, i, pos_ref[0, k, i], sem).start(priority=k % 2)
        return carry

    lax.fori_loop(0, tile, issue, 0, unroll=8)

    def drain(i, carry):
        for k in range(TOP_K):
            _row_copy(hf_ref, xs_hbm, 0, 0, sem).wait()
        return carry

    lax.fori_loop(0, tile, drain, 0, unroll=8)


def _dispatch(hf, pos_tiles, n_rows, tile):
    T, D = hf.shape
    xs0 = jnp.zeros((n_rows, D), hf.dtype)
    return pl.pallas_call(
        functools.partial(_dispatch_kernel, tile=tile),
        out_shape=jax.ShapeDtypeStruct((n_rows, D), hf.dtype),
        grid=(T // tile,),
        in_specs=[pl.BlockSpec((1, TOP_K, tile), lambda i: (i, 0, 0), memory_space=pltpu.SMEM),
                  pl.BlockSpec((tile, D), lambda i: (i, 0)), pl.BlockSpec(memory_space=pl.ANY)],
        out_specs=pl.BlockSpec(memory_space=pl.ANY),
        scratch_shapes=[pltpu.SemaphoreType.DMA(())],
        input_output_aliases={2: 0},
        compiler_params=_cparams(("arbitrary",)),
        name="dispatch",
    )(pos_tiles, hf, xs0)


def _expert_kernel(be_ref, nu_ref, x_ref, w1_ref, b1_ref, w2_ref, b2_ref, y_ref, w1b_ref, w2b_ref):
    i = pl.program_id(0)
    F = w2_ref.shape[0]

    @pl.when(i < nu_ref[0])
    def _():
        @pl.when((i == 0) | (be_ref[i] != be_ref[jnp.maximum(i - 1, 0)]))
        def _():
            w1b_ref[...] = w1_ref[...].astype(BF16)
            w2b_ref[...] = w2_ref[...].astype(BF16)

        gu = jnp.dot(x_ref[...].astype(BF16), w1b_ref[...], preferred_element_type=F32) + b1_ref[...]
        gate = jnp.minimum(gu[:, :F], SWIGLU_LIMIT)
        up = jnp.clip(gu[:, F:], -SWIGLU_LIMIT, SWIGLU_LIMIT)
        act = (up + 1.0) * gate * jax.nn.sigmoid(SWIGLU_ALPHA * gate)
        y_ref[...] = jnp.dot(act.astype(BF16), w2b_ref[...], preferred_element_type=F32) + b2_ref[...]


def _experts(xs, blk_expert, n_used, w1, b1, w2, b2):
    P, D = xs.shape
    E, _, F2 = w1.shape
    F = w2.shape[1]
    M = EXPERT_BLOCK
    blk = lambda i, be, nu: (jnp.minimum(i, nu[0] - 1), 0)
    grid_spec = pltpu.PrefetchScalarGridSpec(
        num_scalar_prefetch=2,
        grid=(P // M,),
        in_specs=[pl.BlockSpec((M, D), blk),
                  pl.BlockSpec((None, D, F2), lambda i, be, nu: (be[i], 0, 0)),
                  pl.BlockSpec((None, 1, F2), lambda i, be, nu: (be[i], 0, 0)),
                  pl.BlockSpec((None, F, D), lambda i, be, nu: (be[i], 0, 0)),
                  pl.BlockSpec((None, 1, D), lambda i, be, nu: (be[i], 0, 0))],
        out_specs=pl.BlockSpec((M, D), blk),
        scratch_shapes=[pltpu.VMEM((D, F2), BF16), pltpu.VMEM((F, D), BF16)],
    )
    return pl.pallas_call(
        _expert_kernel,
        out_shape=jax.ShapeDtypeStruct((P, D), F32),
        grid_spec=grid_spec,
        compiler_params=_cparams(("arbitrary",), 56 * 1024 * 1024),
        name="experts",
    )(blk_expert, n_used, xs, w1, b1.reshape(E, 1, F2), w2, b2.reshape(E, 1, D))


def _combine_kernel(pos_ref, route_ref, x1_ref, g_ref, ys_hbm, o_ref, buf_ref, sem, *, tile):
    def issue(i, carry):
        for k in range(TOP_K):
            pltpu.make_async_copy(ys_hbm.at[pl.ds(pos_ref[0, k, i], 1)], buf_ref.at[k, pl.ds(i, 1)],
                                  sem).start(priority=k % 2)
        return carry

    lax.fori_loop(0, tile, issue, 0, unroll=8)

    def drain(i, carry):
        for k in range(TOP_K):
            pltpu.make_async_copy(ys_hbm.at[pl.ds(0, 1)], buf_ref.at[k, pl.ds(0, 1)], sem).wait()
        return carry

    lax.fori_loop(0, tile, drain, 0, unroll=8)

    route = route_ref[...]
    acc = x1_ref[...]
    for k in range(TOP_K):
        acc = acc + route[:, ROUTE_GATE + k:ROUTE_GATE + k + 1] * buf_ref[k]
    ms = jnp.mean(acc * acc, axis=-1, keepdims=True)
    o_ref[...] = acc * lax.rsqrt(ms + NORM_EPS) * g_ref[...]


def _combine(pos_tiles, route, x1, g_final, ys, tile):
    T, D = x1.shape
    return pl.pallas_call(
        functools.partial(_combine_kernel, tile=tile),
        out_shape=jax.ShapeDtypeStruct((T, D), F32),
        grid=(T // tile,),
        in_specs=[pl.BlockSpec((1, TOP_K, tile), lambda i: (i, 0, 0), memory_space=pltpu.SMEM),
                  pl.BlockSpec((tile, LANES), lambda i: (i, 0)),
                  pl.BlockSpec((tile, D), lambda i: (i, 0)),
                  pl.BlockSpec((1, D), lambda i: (0, 0)),
                  pl.BlockSpec(memory_space=pl.ANY)],
        out_specs=pl.BlockSpec((tile, D), lambda i: (i, 0)),
        scratch_shapes=[pltpu.VMEM((TOP_K, tile, D), F32), pltpu.SemaphoreType.DMA(())],
        compiler_params=_cparams(("arbitrary",)),
        name="combine",
    )(pos_tiles, route, x1, g_final.reshape(1, D), ys)


def _layer(x2d, B, S, norm_mix_g, w_in, rel_bias, w_att, w_ret, w_out, norm_ffn_g, w_router, b_router,
           w1, b1, w2, b2):
    T, D = x2d.shape
    proj = _inproj(x2d, norm_mix_g, w_in.astype(BF16))
    n_att = N_GROUPS * GROUP_WIDTH
    proj3 = proj.reshape(B, S, proj.shape[1])

    o_list, lse_list = [], []
    for g, d in enumerate(DILATIONS):
        L = S // d
        nb = L // N_STEPS
        bias = _band_bias(rel_bias[:, g * HEADS_PER_GROUP:(g + 1) * HEADS_PER_GROUP], d)
        if d == 1:
            p4 = proj3.reshape(B, 1, S, proj.shape[1])
            cols = tuple((off + g * GROUP_WIDTH) // COL_BLOCK for off in (0, n_att, 2 * n_att))
            o, lse = _attention_group(p4, p4, p4, cols, bias, blocks=min(nb, 8))
            o_list.append(o.reshape(T, GROUP_WIDTH))
            lse_list.append(lse.reshape(T, LANES))
            continue

        def gather(off, g=g, d=d, L=L):
            t = proj3[:, :, off + g * GROUP_WIDTH: off + (g + 1) * GROUP_WIDTH]
            return t.reshape(B, L, d, GROUP_WIDTH).transpose(0, 2, 1, 3)

        o, lse = _attention_group(gather(0), gather(n_att), gather(2 * n_att), (0, 0, 0), bias,
                                  blocks=min(nb, 8))
        o_list.append(o.transpose(0, 2, 1, 3).reshape(T, GROUP_WIDTH))
        lse_list.append(lse.transpose(0, 2, 1, 3).reshape(T, LANES))

    ret_col0 = 3 * n_att // COL_BLOCK
    ret = _retention(proj, B, S, ret_col0)
    gate_col0 = ret_col0 + (2 * RET_HEADS * RET_DK + 2 * RET_HEADS * RET_DV) // COL_BLOCK

    w_router_p = jnp.zeros((D, LANES), BF16).at[:, :N_EXPERTS].set(w_router.astype(BF16))
    b_router_p = jnp.full((1, LANES), NEG_BIG, F32).at[0, :N_EXPERTS].set(b_router.astype(F32))
    x1, hf, route, counts = _merge(x2d, o_list, lse_list, ret, proj, gate_col0, w_att.astype(BF16),
                                   w_ret.astype(BF16), w_out.astype(BF16), norm_ffn_g.reshape(1, D),
                                   w_router_p, b_router_p)

    M = EXPERT_BLOCK
    n_blocks = (T * TOP_K) // M + N_EXPERTS
    cnt = counts[0, :N_EXPERTS].astype(jnp.int32)
    padded = ((cnt + M - 1) // M) * M
    pend = jnp.cumsum(padded)
    pstart = pend - padded
    idx = route[:, ROUTE_IDX:ROUTE_IDX + TOP_K].astype(jnp.int32)
    rank = route[:, ROUTE_RANK:ROUTE_RANK + TOP_K].astype(jnp.int32)
    experts = jnp.arange(N_EXPERTS, dtype=jnp.int32)
    pos = rank + jnp.sum(jnp.where(idx[..., None] == experts, pstart, 0), axis=-1)
    n_used = (pend[-1] // M).astype(jnp.int32)
    blk_row = jnp.minimum(jnp.arange(n_blocks, dtype=jnp.int32), jnp.maximum(n_used - 1, 0)) * M
    blk_expert = jnp.minimum(jnp.sum((pend[None, :] <= blk_row[:, None]).astype(jnp.int32), axis=-1),
                             N_EXPERTS - 1)

    tile = 256
    pos_tiles = pos.reshape(T // tile, tile, TOP_K).transpose(0, 2, 1)
    xs = _dispatch(hf, pos_tiles, n_blocks * M, tile)
    ys = _experts(xs, blk_expert, n_used.reshape(1), w1, b1, w2, b2)
    return pos_tiles, route, x1, ys


def kernel(x, norm_mix_g, w_in, rel_bias, w_att_branch, w_ret_branch, w_out, norm_ffn_g, w_router, b_router,
           w1, b1, w2, b2, norm_final_g):
    B, S, D = x.shape
    depth = w_in.shape[0]
    assert depth == 1, "the combine stage applies the final norm; a deeper stack needs a separate norm pass"
    x2d = x.reshape(B * S, D)
    pos_tiles, route, x1, ys = _layer(x2d, B, S, norm_mix_g[0], w_in[0], rel_bias, w_att_branch[0],
                                      w_ret_branch[0], w_out[0], norm_ffn_g[0], w_router[0], b_router[0],
                                      w1[0], b1[0], w2[0], b2[0])
    out = _combine(pos_tiles, route, x1, norm_final_g, ys, tile=256)
    return out.reshape(B, S, D)
```

```python
import functools
import math

import numpy as np
import jax
import jax.numpy as jnp
from jax import lax
from jax.experimental import pallas as pl
from jax.experimental.pallas import tpu as pltpu

F32 = jnp.float32
BF16 = jnp.bfloat16

NORM_EPS = 1e-5

HEAD_DIM = 128
HEADS_PER_GROUP = 4
GROUP_WIDTH = HEADS_PER_GROUP * HEAD_DIM
DILATIONS = (1, 4, 16)
N_STEPS = 128
N_GROUPS = len(DILATIONS)
REL_BUCKETS = 32
REL_MAX_DIST = 2048

RET_HEADS = 4
RET_DK = 128
RET_DV = 256
RET_CHUNK = 128
ROPE_BASE = 10000.0

N_EXPERTS = 32
TOP_K = 4
SWIGLU_LIMIT = 7.0
SWIGLU_ALPHA = 1.702
EXPERT_BLOCK = 256

LANES = 128
COL_BLOCK = 512
NEG_BIG = -1e30

VMEM_LIMIT = 48 * 1024 * 1024


def _cparams(sem, vmem=VMEM_LIMIT):
    return pltpu.CompilerParams(dimension_semantics=sem, vmem_limit_bytes=vmem)


def _inproj_kernel(x_ref, g_ref, w_ref, o_ref, h_ref):
    x = x_ref[...]
    ms = jnp.mean(x * x, axis=-1, keepdims=True)
    h_ref[...] = (x * lax.rsqrt(ms + NORM_EPS) * g_ref[...]).astype(BF16)
    n_chunks = o_ref.shape[1] // COL_BLOCK
    for c in range(n_chunks):
        cols = slice(c * COL_BLOCK, (c + 1) * COL_BLOCK)
        o_ref[:, cols] = jnp.dot(h_ref[...], w_ref[:, cols], preferred_element_type=F32).astype(BF16)


def _inproj(x2d, g, w_bf16, tm=512):
    T, D = x2d.shape
    N = w_bf16.shape[1]
    return pl.pallas_call(
        _inproj_kernel,
        out_shape=jax.ShapeDtypeStruct((T, N), BF16),
        grid=(T // tm,),
        in_specs=[
            pl.BlockSpec((tm, D), lambda i: (i, 0)),
            pl.BlockSpec((1, D), lambda i: (0, 0)),
            pl.BlockSpec((D, N), lambda i: (0, 0), pipeline_mode=pl.Buffered(1)),
        ],
        out_specs=pl.BlockSpec((tm, N), lambda i: (i, 0)),
        scratch_shapes=[pltpu.VMEM((tm, D), BF16)],
        compiler_params=_cparams(("arbitrary",), 56 * 1024 * 1024),
        name="inproj",
    )(x2d, g.reshape(1, D), w_bf16)


def _attn_block(q, kp, kc, vp, vc, bias_ref, has_prev):
    scale = HEAD_DIM ** -0.5
    heads = range(HEADS_PER_GROUP)
    hs = [slice(h * HEAD_DIM, (h + 1) * HEAD_DIM) for h in heads]
    rs = [slice(h * N_STEPS, (h + 1) * N_STEPS) for h in heads]
    k = jnp.concatenate([kp, kc], axis=0)
    v = jnp.concatenate([vp, vc], axis=0)
    dn = (((1,), (1,)), ((), ()))
    s = jnp.concatenate([lax.dot_general(q[:, hs[h]], k[:, hs[h]], dn, preferred_element_type=F32)
                         for h in heads], axis=0)
    s = s * scale + bias_ref[...]
    if has_prev is not True:
        col = lax.broadcasted_iota(jnp.int32, s.shape, 1)
        s = jnp.where((col >= N_STEPS) | has_prev, s, NEG_BIG)
    m = jnp.max(s, axis=-1, keepdims=True)
    p = jnp.exp(s - m)
    l = jnp.sum(p, axis=-1, keepdims=True)
    pb = p.astype(BF16)
    lse_rows = m + jnp.log(l)
    lane = lax.broadcasted_iota(jnp.int32, (N_STEPS, LANES), 1)
    outs = []
    lse = jnp.zeros((N_STEPS, LANES), F32)
    for h in heads:
        acc = jnp.dot(pb[rs[h], :], v[:, hs[h]], preferred_element_type=F32)
        outs.append(acc / l[rs[h], :])
        lse = jnp.where(lane == h, lse_rows[rs[h], :], lse)
    return jnp.concatenate(outs, axis=-1), lse


def _attn_kernel(q_ref, k_ref, v_ref, kprev_ref, vprev_ref, bias_ref, o_ref, lse_ref, *, blocks):
    n = pl.program_id(2)

    def run(j, kp, vp, has_prev):
        rows = pl.ds(pl.multiple_of(j * N_STEPS, N_STEPS), N_STEPS)
        o, lse = _attn_block(q_ref[rows, :], kp, k_ref[rows, :], vp, v_ref[rows, :], bias_ref, has_prev)
        o_ref[rows, :] = o.astype(o_ref.dtype)
        lse_ref[rows, :] = lse

    run(0, kprev_ref[...], vprev_ref[...], n > 0)

    def body(j, carry):
        prev = pl.ds(pl.multiple_of((j - 1) * N_STEPS, N_STEPS), N_STEPS)
        run(j, k_ref[prev, :], v_ref[prev, :], True)
        return carry

    if blocks > 1:
        lax.fori_loop(1, blocks, body, 0, unroll=True)


def _attention_group(q, k, v, cols, bias, blocks):
    B, d, L, _ = q.shape
    W = GROUP_WIDTH
    rows = blocks * N_STEPS
    steps = L // rows

    def cur(c):
        return pl.BlockSpec((None, None, rows, W), lambda b, r, n: (b, r, n, c))

    def prev(c):
        return pl.BlockSpec((None, None, N_STEPS, W), lambda b, r, n: (b, r, jnp.maximum(n * blocks - 1, 0), c))

    return pl.pallas_call(
        functools.partial(_attn_kernel, blocks=blocks),
        out_shape=(jax.ShapeDtypeStruct((B, d, L, W), BF16), jax.ShapeDtypeStruct((B, d, L, LANES), F32)),
        grid=(B, d, steps),
        in_specs=[cur(cols[0]), cur(cols[1]), cur(cols[2]), prev(cols[1]), prev(cols[2]),
                  pl.BlockSpec((HEADS_PER_GROUP * N_STEPS, 2 * N_STEPS), lambda b, r, n: (0, 0))],
        out_specs=(pl.BlockSpec((None, None, rows, W), lambda b, r, n: (b, r, n, 0)),
                   pl.BlockSpec((None, None, rows, LANES), lambda b, r, n: (b, r, n, 0))),
        compiler_params=_cparams(("arbitrary", "arbitrary", "arbitrary")),
        name=f"attn_d{d}",
    )(q, k, v, k, v, bias)


def _t5_bucket(dist):
    max_exact = REL_BUCKETS // 2
    d_f = jnp.maximum(dist, 1).astype(F32)
    large = max_exact + (jnp.log(d_f / max_exact) / math.log(REL_MAX_DIST / max_exact)
                         * (REL_BUCKETS - max_exact)).astype(jnp.int32)
    large = jnp.minimum(large, REL_BUCKETS - 1)
    return jnp.where(dist < max_exact, dist, large)


def _band_bias(rel_bias_g, dilation):
    qi = jnp.arange(N_STEPS)[:, None]
    kj = jnp.arange(2 * N_STEPS)[None, :]
    step_dist = qi + N_STEPS - kj
    band = (step_dist >= 0) & (step_dist <= N_STEPS)
    bucket = _t5_bucket(jnp.maximum(step_dist, 0) * dilation)
    table = rel_bias_g.astype(F32).T
    hit = bucket[None, None] == jnp.arange(REL_BUCKETS)[None, :, None, None]
    bias = jnp.sum(jnp.where(hit, table[:, :, None, None], 0.0), axis=1)
    return jnp.where(band[None], bias, NEG_BIG).reshape(HEADS_PER_GROUP * N_STEPS, 2 * N_STEPS)


def _ret_kernel(q_ref, k_ref, v0_ref, v1_ref, g0_ref, g1_ref, cos_ref, sin_ref, decay_ref, xi_ref, zeta_ref,
                o_ref, state_ref, *, g_chunk):
    n = pl.program_id(1)

    @pl.when(n == 0)
    def _():
        state_ref[...] = jnp.zeros_like(state_ref)

    cos = cos_ref[...]
    sin = sin_ref[...]
    lane = lax.broadcasted_iota(jnp.int32, (RET_CHUNK, RET_DK), 1)
    even = (lane % 2) == 0

    def rotary(t):
        partner = jnp.where(even, pltpu.roll(t, RET_DK - 1, 1), pltpu.roll(t, 1, 1))
        return t * cos + partner * sin

    for h in range(RET_HEADS):
        ks = slice(h * RET_DK, (h + 1) * RET_DK)
        qr = rotary(q_ref[:, ks].astype(F32))
        kr = rotary(k_ref[:, ks].astype(F32)) * (RET_DK ** -0.5)
        v_ref = v0_ref if h < 2 else v1_ref
        g_ref = g0_ref if h < 2 else g1_ref
        vs = slice((h % 2) * RET_DV, (h % 2 + 1) * RET_DV)
        v = v_ref[:, vs]
        qb = qr.astype(BF16)
        kb = kr.astype(BF16)
        scores = lax.dot_general(qb, kb, (((1,), (1,)), ((), ())), preferred_element_type=F32) * decay_ref[h]
        inner = jnp.dot(scores.astype(BF16), v, preferred_element_type=F32)
        state = state_ref[h]
        cross = jnp.dot(qb, state.astype(BF16), preferred_element_type=F32) * xi_ref[h]
        kz = (kr * zeta_ref[h]).astype(BF16)
        state_ref[h] = state * g_chunk[h] + lax.dot_general(kz, v, (((0,), (0,)), ((), ())),
                                                            preferred_element_type=F32)
        ret = inner + cross
        mu = jnp.mean(ret, axis=-1, keepdims=True)
        cen = ret - mu
        var = jnp.mean(cen * cen, axis=-1, keepdims=True)
        normed = cen * lax.rsqrt(var + NORM_EPS)
        g = g_ref[:, vs].astype(F32)
        o_ref[:, h * RET_DV:(h + 1) * RET_DV] = (g * jax.nn.sigmoid(g) * normed).astype(o_ref.dtype)


def _retention(proj, B, S, col0):
    T = B * S
    nc = S // RET_CHUNK
    C = RET_CHUNK
    log_g = np.log(1.0 - 2.0 ** (-5.0 - np.arange(RET_HEADS, dtype=np.float64)))
    idx = np.arange(C, dtype=np.float64)
    diff = idx[:, None] - idx[None, :]
    decay = np.where(diff >= 0, np.exp(np.maximum(diff, 0.0)[None] * log_g[:, None, None]), 0.0)
    xi = np.exp((idx + 1.0)[None, :] * log_g[:, None])
    zeta = np.exp((C - 1.0 - idx)[None, :] * log_g[:, None])
    g_chunk = tuple(float(v) for v in np.exp(C * log_g))
    xi_b = np.broadcast_to(xi[:, :, None], (RET_HEADS, C, RET_DV)).astype(np.float32)
    zeta_b = np.broadcast_to(zeta[:, :, None], (RET_HEADS, C, RET_DK)).astype(np.float32)

    inv = ROPE_BASE ** (-np.arange(0, RET_DK, 2, dtype=np.float64) / RET_DK)
    ang = np.arange(S, dtype=np.float64)[:, None] * inv[None]
    cos_t = np.repeat(np.cos(ang), 2, axis=1).astype(np.float32)
    sin_t = np.stack([-np.sin(ang), np.sin(ang)], axis=-1).reshape(S, RET_DK).astype(np.float32)

    def col(c):
        return pl.BlockSpec((C, COL_BLOCK), lambda b, n, c=c: (b * nc + n, c))

    const3 = lambda shape: pl.BlockSpec(shape, lambda b, n: (0, 0, 0))
    return pl.pallas_call(
        functools.partial(_ret_kernel, g_chunk=g_chunk),
        out_shape=jax.ShapeDtypeStruct((T, RET_HEADS * RET_DV), BF16),
        grid=(B, nc),
        in_specs=[col(col0), col(col0 + 1), col(col0 + 2), col(col0 + 3), col(col0 + 4), col(col0 + 5),
                  pl.BlockSpec((C, RET_DK), lambda b, n: (n, 0)),
                  pl.BlockSpec((C, RET_DK), lambda b, n: (n, 0)),
                  const3((RET_HEADS, C, C)), const3((RET_HEADS, C, RET_DV)), const3((RET_HEADS, C, RET_DK))],
        out_specs=pl.BlockSpec((C, RET_HEADS * RET_DV), lambda b, n: (b * nc + n, 0)),
        scratch_shapes=[pltpu.VMEM((RET_HEADS, RET_DK, RET_DV), F32)],
        compiler_params=_cparams(("arbitrary", "arbitrary")),
        name="retention",
    )(proj, proj, proj, proj, proj, proj, jnp.asarray(cos_t), jnp.asarray(sin_t),
      jnp.asarray(decay.astype(np.float32)), jnp.asarray(xi_b), jnp.asarray(zeta_b))


ROUTE_IDX, ROUTE_RANK, ROUTE_GATE = 0, 4, 8


def _merge_kernel(x_ref, o1_ref, o2_ref, o3_ref, l1_ref, l2_ref, l3_ref, ret_ref,
                  ga0_ref, ga1_ref, gr0_ref, gr1_ref, watt_ref, wret_ref, wout_ref, gffn_ref, wr_ref, br_ref,
                  x1_ref, hf_ref, route_ref, counts_ref):
    tm = x_ref.shape[0]
    o_refs = (o1_ref, o2_ref, o3_ref)
    lses = [r[...] for r in (l1_ref, l2_ref, l3_ref)]

    att_parts = []
    for h in range(HEADS_PER_GROUP):
        ls = [l[:, h:h + 1] for l in lses]
        m = jnp.maximum(jnp.maximum(ls[0], ls[1]), ls[2])
        ws = [jnp.exp(l - m) for l in ls]
        wsum = ws[0] + ws[1] + ws[2]
        hs = slice(h * HEAD_DIM, (h + 1) * HEAD_DIM)
        acc = (ws[0] / wsum) * o_refs[0][:, hs].astype(F32)
        acc = acc + (ws[1] / wsum) * o_refs[1][:, hs].astype(F32)
        acc = acc + (ws[2] / wsum) * o_refs[2][:, hs].astype(F32)
        att_parts.append(acc)
    att = jnp.concatenate(att_parts, axis=-1).astype(BF16)

    a_proj = jnp.dot(att, watt_ref[...], preferred_element_type=F32)
    r_proj = jnp.dot(ret_ref[...], wret_ref[...], preferred_element_type=F32)
    gate_a = jnp.concatenate([ga0_ref[...], ga1_ref[...]], axis=-1).astype(F32)
    gate_r = jnp.concatenate([gr0_ref[...], gr1_ref[...]], axis=-1).astype(F32)
    merged = jax.nn.sigmoid(gate_a) * a_proj + jax.nn.sigmoid(gate_r) * r_proj
    x1 = x_ref[...] + jnp.dot(merged.astype(BF16), wout_ref[...], preferred_element_type=F32)
    x1_ref[...] = x1

    ms = jnp.mean(x1 * x1, axis=-1, keepdims=True)
    hf = x1 * lax.rsqrt(ms + NORM_EPS) * gffn_ref[...]
    hf_ref[...] = hf

    logits = jnp.dot(hf.astype(BF16), wr_ref[...], preferred_element_type=F32) + br_ref[...]
    lane = lax.broadcasted_iota(jnp.int32, (tm, LANES), 1)
    lane_f = lane.astype(F32)
    work = logits
    vals, idxs, hits = [], [], []
    for _ in range(TOP_K):
        mk = jnp.max(work, axis=-1, keepdims=True)
        ik = jnp.min(jnp.where(work == mk, lane_f, float(LANES)), axis=-1, keepdims=True)
        hit = lane_f == ik
        work = jnp.where(hit, -jnp.inf, work)
        vals.append(mk)
        idxs.append(ik)
        hits.append(hit)
    es = [jnp.exp(v - vals[0]) for v in vals]
    esum = es[0] + es[1] + es[2] + es[3]
    gates = [e / esum for e in es]

    sel = jnp.zeros((tm, LANES), F32)
    for hit in hits:
        sel = jnp.where(hit, 1.0, sel)

    @pl.when(pl.program_id(0) == 0)
    def _():
        counts_ref[...] = jnp.zeros_like(counts_ref)

    row = lax.broadcasted_iota(jnp.int32, (tm, tm), 0)
    colm = lax.broadcasted_iota(jnp.int32, (tm, tm), 1)
    below = jnp.where(colm < row, 1.0, 0.0).astype(BF16)
    rank = jnp.dot(below, sel.astype(BF16), preferred_element_type=F32) + counts_ref[...]
    counts_ref[...] = counts_ref[...] + jnp.sum(sel, axis=0, keepdims=True)

    route = jnp.zeros((tm, LANES), F32)
    for k in range(TOP_K):
        rank_k = jnp.sum(jnp.where(hits[k], rank, 0.0), axis=-1, keepdims=True)
        route = jnp.where(lane == ROUTE_IDX + k, idxs[k], route)
        route = jnp.where(lane == ROUTE_RANK + k, rank_k, route)
        route = jnp.where(lane == ROUTE_GATE + k, gates[k], route)
    route_ref[...] = route


def _merge(x2d, o_list, lse_list, ret, proj, gate_col0, w_att, w_ret, w_out, g_ffn, w_router_p, b_router_p, tm=256):
    T, D = x2d.shape
    row = lambda w: pl.BlockSpec((tm, w), lambda i: (i, 0))
    full = lambda a: pl.BlockSpec(a.shape, lambda i: (0,) * a.ndim)
    gcol = lambda c: pl.BlockSpec((tm, COL_BLOCK), lambda i, c=c: (i, c))
    return pl.pallas_call(
        _merge_kernel,
        out_shape=(jax.ShapeDtypeStruct((T, D), F32), jax.ShapeDtypeStruct((T, D), F32),
                   jax.ShapeDtypeStruct((T, LANES), F32), jax.ShapeDtypeStruct((1, LANES), F32)),
        grid=(T // tm,),
        in_specs=[row(D), row(GROUP_WIDTH), row(GROUP_WIDTH), row(GROUP_WIDTH), row(LANES), row(LANES), row(LANES),
                  row(RET_HEADS * RET_DV), gcol(gate_col0), gcol(gate_col0 + 1), gcol(gate_col0 + 2),
                  gcol(gate_col0 + 3), full(w_att), full(w_ret), full(w_out), full(g_ffn), full(w_router_p),
                  full(b_router_p)],
        out_specs=(row(D), row(D), row(LANES), pl.BlockSpec((1, LANES), lambda i: (0, 0))),
        compiler_params=_cparams(("arbitrary",)),
        name="merge",
    )(x2d, *o_list, *lse_list, ret, proj, proj, proj, proj, w_att, w_ret, w_out, g_ffn, w_router_p, b_router_p)


N_ACT_CHUNKS = 4


def _moe_kernel(be_ref, nu_ref, tok_cur_ref, tok_next_ref, slot_prev_ref, slot_cur_ref,
                hf_hbm, w1_ref, b1_ref, w2_ref, b2_ref, out_hbm,
                xbuf, ybuf, act_ref, w1b_ref, w2b_ref, gsem, ssem):
    i = pl.program_id(0)
    M = EXPERT_BLOCK
    F = w2_ref.shape[0]
    n_used = nu_ref[0]

    def gather_row(tok_ref, j, slot):
        return pltpu.make_async_copy(hf_hbm.at[pl.ds(tok_ref[0, 0, j], 1)], xbuf.at[slot, pl.ds(j, 1)],
                                     gsem.at[slot])

    def scatter_row(slot_ref, j, slot):
        return pltpu.make_async_copy(ybuf.at[slot, pl.ds(j, 1)], out_hbm.at[pl.ds(slot_ref[0, 0, j], 1)],
                                     ssem.at[slot])

    def wait_gather(slot):
        def body(j, c):
            pltpu.make_async_copy(hf_hbm.at[pl.ds(0, 1)], xbuf.at[slot, pl.ds(0, 1)], gsem.at[slot]).wait()
            return c
        lax.fori_loop(0, M, body, 0, unroll=8)

    def wait_scatter(slot):
        def body(j, c):
            pltpu.make_async_copy(ybuf.at[slot, pl.ds(0, 1)], out_hbm.at[pl.ds(0, 1)], ssem.at[slot]).wait()
            return c
        lax.fori_loop(0, M, body, 0, unroll=8)

    @pl.when(i < n_used)
    def _():
        cur = lax.rem(i, 2)
        nxt = 1 - cur

        @pl.when(i == 0)
        def _():
            ybuf[...] = jnp.zeros_like(ybuf)

            def issue(j, c):
                gather_row(tok_cur_ref, j, 0).start()
                return c
            lax.fori_loop(0, M, issue, 0, unroll=8)

        wait_gather(cur)

        @pl.when(i >= 1)
        def _():
            wait_scatter(cur)

        @pl.when((i == 0) | (be_ref[i] != be_ref[jnp.maximum(i - 1, 0)]))
        def _():
            w1b_ref[...] = w1_ref[...].astype(BF16)
            w2b_ref[...] = w2_ref[...].astype(BF16)

        x = xbuf[cur].astype(BF16)
        cw = F // N_ACT_CHUNKS
        per = M // (2 * N_ACT_CHUNKS)

        def issue_rows(piece):
            for j in range(piece * per, (piece + 1) * per):
                gather_row(tok_next_ref, j, nxt).start(priority=0)
                scatter_row(slot_prev_ref, j, nxt).start(priority=1)

        for c in range(N_ACT_CHUNKS):
            gc = slice(c * cw, (c + 1) * cw)
            uc = slice(F + c * cw, F + (c + 1) * cw)
            g = jnp.dot(x, w1b_ref[:, gc], preferred_element_type=F32) + b1_ref[:, gc]
            issue_rows(2 * c)
            u = jnp.dot(x, w1b_ref[:, uc], preferred_element_type=F32) + b1_ref[:, uc]
            gate = jnp.minimum(g, SWIGLU_LIMIT)
            up = jnp.clip(u, -SWIGLU_LIMIT, SWIGLU_LIMIT)
            act_ref[:, gc] = ((up + 1.0) * gate * jax.nn.sigmoid(SWIGLU_ALPHA * gate)).astype(BF16)
            issue_rows(2 * c + 1)
        ybuf[cur] = jnp.dot(act_ref[...], w2b_ref[...], preferred_element_type=F32) + b2_ref[...]

        @pl.when(i == n_used - 1)
        def _():
            def issue(j, c):
                scatter_row(slot_cur_ref, j, cur).start()
                return c
            lax.fori_loop(0, M, issue, 0, unroll=8)
            wait_gather(nxt)
            wait_scatter(nxt)
            wait_scatter(cur)


def _moe(hf, tok_blocks, slot_blocks, blk_expert, n_used, w1, b1, w2, b2, n_out_rows):
    T, D = hf.shape
    E, _, F2 = w1.shape
    F = w2.shape[1]
    M = EXPERT_BLOCK
    n_blocks = tok_blocks.shape[0] - 1
    smem = lambda off: pl.BlockSpec((1, 1, M), lambda i, be, nu: (i + off, 0, 0), memory_space=pltpu.SMEM)
    grid_spec = pltpu.PrefetchScalarGridSpec(
        num_scalar_prefetch=2,
        grid=(n_blocks,),
        in_specs=[smem(0), smem(1), smem(0), smem(1),
                  pl.BlockSpec(memory_space=pl.ANY),
                  pl.BlockSpec((None, D, F2), lambda i, be, nu: (be[i], 0, 0)),
                  pl.BlockSpec((None, 1, F2), lambda i, be, nu: (be[i], 0, 0)),
                  pl.BlockSpec((None, F, D), lambda i, be, nu: (be[i], 0, 0)),
                  pl.BlockSpec((None, 1, D), lambda i, be, nu: (be[i], 0, 0))],
        out_specs=pl.BlockSpec(memory_space=pl.ANY),
        scratch_shapes=[pltpu.VMEM((2, M, D), F32), pltpu.VMEM((2, M, D), F32), pltpu.VMEM((M, F), BF16),
                        pltpu.VMEM((D, F2), BF16), pltpu.VMEM((F, D), BF16),
                        pltpu.SemaphoreType.DMA((2,)), pltpu.SemaphoreType.DMA((2,))],
    )
    return pl.pallas_call(
        _moe_kernel,
        out_shape=jax.ShapeDtypeStruct((n_out_rows, D), F32),
        grid_spec=grid_spec,
        compiler_params=_cparams(("arbitrary",), 56 * 1024 * 1024),
        name="experts",
    )(blk_expert, n_used, tok_blocks, tok_blocks, slot_blocks, slot_blocks, hf, w1, b1.reshape(E, 1, F2), w2,
      b2.reshape(E, 1, D))


def _combine_kernel(route_ref, x1_ref, g_ref, y0_ref, y1_ref, y2_ref, y3_ref, o_ref):
    route = route_ref[...]
    acc = x1_ref[...]
    for k, y_ref in enumerate((y0_ref, y1_ref, y2_ref, y3_ref)):
        acc = acc + route[:, ROUTE_GATE + k:ROUTE_GATE + k + 1] * y_ref[...]
    ms = jnp.mean(acc * acc, axis=-1, keepdims=True)
    o_ref[...] = acc * lax.rsqrt(ms + NORM_EPS) * g_ref[...]


def _combine(route, x1, g_final, ys, tile=512):
    T, D = x1.shape
    nt = T // tile
    yk = lambda k: pl.BlockSpec((tile, D), lambda i, k=k: (k * nt + i, 0))
    return pl.pallas_call(
        _combine_kernel,
        out_shape=jax.ShapeDtypeStruct((T, D), F32),
        grid=(nt,),
        in_specs=[pl.BlockSpec((tile, LANES), lambda i: (i, 0)),
                  pl.BlockSpec((tile, D), lambda i: (i, 0)),
                  pl.BlockSpec((1, D), lambda i: (0, 0)),
                  yk(0), yk(1), yk(2), yk(3)],
        out_specs=pl.BlockSpec((tile, D), lambda i: (i, 0)),
        compiler_params=_cparams(("arbitrary",)),
        name="combine",
    )(route, x1, g_final.reshape(1, D), ys, ys, ys, ys)


def _layer(x2d, B, S, norm_mix_g, w_in, rel_bias, w_att, w_ret, w_out, norm_ffn_g, w_router, b_router,
           w1, b1, w2, b2):
    T, D = x2d.shape
    proj = _inproj(x2d, norm_mix_g, w_in.astype(BF16))
    n_att = N_GROUPS * GROUP_WIDTH
    proj3 = proj.reshape(B, S, proj.shape[1])

    o_list, lse_list = [], []
    for g, d in enumerate(DILATIONS):
        L = S // d
        nb = L // N_STEPS
        bias = _band_bias(rel_bias[:, g * HEADS_PER_GROUP:(g + 1) * HEADS_PER_GROUP], d)
        if d == 1:
            p4 = proj3.reshape(B, 1, S, proj.shape[1])
            cols = tuple((off + g * GROUP_WIDTH) // COL_BLOCK for off in (0, n_att, 2 * n_att))
            o, lse = _attention_group(p4, p4, p4, cols, bias, blocks=min(nb, 8))
            o_list.append(o.reshape(T, GROUP_WIDTH))
            lse_list.append(lse.reshape(T, LANES))
            continue

        def gather(off, g=g, d=d, L=L):
            t = proj3[:, :, off + g * GROUP_WIDTH: off + (g + 1) * GROUP_WIDTH]
            return t.reshape(B, L, d, GROUP_WIDTH).transpose(0, 2, 1, 3)

        o, lse = _attention_group(gather(0), gather(n_att), gather(2 * n_att), (0, 0, 0), bias,
                                  blocks=min(nb, 8))
        o_list.append(o.transpose(0, 2, 1, 3).reshape(T, GROUP_WIDTH))
        lse_list.append(lse.transpose(0, 2, 1, 3).reshape(T, LANES))

    ret_col0 = 3 * n_att // COL_BLOCK
    ret = _retention(proj, B, S, ret_col0)
    gate_col0 = ret_col0 + (2 * RET_HEADS * RET_DK + 2 * RET_HEADS * RET_DV) // COL_BLOCK

    w_router_p = jnp.zeros((D, LANES), BF16).at[:, :N_EXPERTS].set(w_router.astype(BF16))
    b_router_p = jnp.full((1, LANES), NEG_BIG, F32).at[0, :N_EXPERTS].set(b_router.astype(F32))
    x1, hf, route, counts = _merge(x2d, o_list, lse_list, ret, proj, gate_col0, w_att.astype(BF16),
                                   w_ret.astype(BF16), w_out.astype(BF16), norm_ffn_g.reshape(1, D),
                                   w_router_p, b_router_p)

    M = EXPERT_BLOCK
    n_blocks = (T * TOP_K) // M + N_EXPERTS
    cnt = counts[0, :N_EXPERTS].astype(jnp.int32)
    padded = ((cnt + M - 1) // M) * M
    pend = jnp.cumsum(padded)
    pstart = pend - padded
    idx = route[:, ROUTE_IDX:ROUTE_IDX + TOP_K].astype(jnp.int32)
    rank = route[:, ROUTE_RANK:ROUTE_RANK + TOP_K].astype(jnp.int32)
    experts = jnp.arange(N_EXPERTS, dtype=jnp.int32)
    pos = rank + jnp.sum(jnp.where(idx[..., None] == experts, pstart, 0), axis=-1)
    n_used = (pend[-1] // M).astype(jnp.int32)
    blk_row = jnp.minimum(jnp.arange(n_blocks, dtype=jnp.int32), jnp.maximum(n_used - 1, 0)) * M
    blk_expert = jnp.minimum(jnp.sum((pend[None, :] <= blk_row[:, None]).astype(jnp.int32), axis=-1),
                             N_EXPERTS - 1)

    A = T * TOP_K
    P = n_blocks * M
    assign = jnp.arange(TOP_K, dtype=jnp.int32)[None, :] * T + jnp.arange(T, dtype=jnp.int32)[:, None]
    inv = jnp.full((P + M,), -1, jnp.int32).at[pos.reshape(-1)].set(assign.reshape(-1), unique_indices=True)
    real = inv >= 0
    tok = jnp.where(real, inv % T, 0)
    pad_rank = jnp.cumsum(jnp.where(real, 0, 1).astype(jnp.int32)) - 1
    slot = jnp.where(real, inv, A + M + pad_rank)
    tok_blocks = tok.reshape(n_blocks + 1, 1, M)
    first_trash = A + jnp.arange(M, dtype=jnp.int32)
    slot_blocks = jnp.concatenate([first_trash, slot[:P]]).reshape(n_blocks + 1, 1, M)
    ys = _moe(hf, tok_blocks, slot_blocks, blk_expert, n_used.reshape(1), w1, b1, w2, b2, P + 2 * M)
    return route, x1, ys


def kernel(x, norm_mix_g, w_in, rel_bias, w_att_branch, w_ret_branch, w_out, norm_ffn_g, w_router, b_router,
           w1, b1, w2, b2, norm_final_g):
    B, S, D = x.shape
    depth = w_in.shape[0]
    assert depth == 1, "the combine stage applies the final norm; a deeper stack needs a separate norm pass"
    x2d = x.reshape(B * S, D)
    route, x1, ys = _layer(x2d, B, S, norm_mix_g[0], w_in[0], rel_bias, w_att_branch[0], w_ret_branch[0],
                           w_out[0], norm_ffn_g[0], w_router[0], b_router[0], w1[0], b1[0], w2[0], b2[0])
    out = _combine(route, x1, norm_final_g, ys)
    return out.reshape(B, S, D)
```

```python
import functools
import math

import numpy as np
import jax
import jax.numpy as jnp
from jax import lax
from jax.experimental import pallas as pl
from jax.experimental.pallas import tpu as pltpu

F32 = jnp.float32
BF16 = jnp.bfloat16

NORM_EPS = 1e-5

HEAD_DIM = 128
HEADS_PER_GROUP = 4
GROUP_WIDTH = HEADS_PER_GROUP * HEAD_DIM
DILATIONS = (1, 4, 16)
N_STEPS = 128
N_GROUPS = len(DILATIONS)
REL_BUCKETS = 32
REL_MAX_DIST = 2048

RET_HEADS = 4
RET_DK = 128
RET_DV = 256
RET_CHUNK = 128
ROPE_BASE = 10000.0

N_EXPERTS = 32
TOP_K = 4
SWIGLU_LIMIT = 7.0
SWIGLU_ALPHA = 1.702
EXPERT_BLOCK = 256

LANES = 128
COL_BLOCK = 512
NEG_BIG = -1e30

VMEM_LIMIT = 48 * 1024 * 1024
VMEM_LIMIT_BIG = 56 * 1024 * 1024

N_ATT_CHUNKS = 3 * N_GROUPS
MAIN_RET_COL = 3
MAIN_GATE_COL = MAIN_RET_COL + (2 * RET_HEADS * RET_DK + 2 * RET_HEADS * RET_DV) // COL_BLOCK


def _cparams(sem, vmem=VMEM_LIMIT):
    return pltpu.CompilerParams(dimension_semantics=sem, vmem_limit_bytes=vmem)


def _inproj_kernel(x_ref, g_ref, w_ref, main_ref, a4_ref, a16_ref, h_ref, slab_ref):
    tm = x_ref.shape[0]
    x = x_ref[...]
    ms = jnp.mean(x * x, axis=-1, keepdims=True)
    h_ref[...] = (x * lax.rsqrt(ms + NORM_EPS) * g_ref[...]).astype(BF16)
    n_chunks = w_ref.shape[1] // COL_BLOCK
    n_slabs = COL_BLOCK // LANES
    dilated = {1: (DILATIONS[1], a4_ref), 2: (DILATIONS[2], a16_ref)}
    for c in range(n_chunks):
        res = jnp.dot(h_ref[...], w_ref[:, c * COL_BLOCK:(c + 1) * COL_BLOCK], preferred_element_type=F32)
        if c >= N_ATT_CHUNKS:
            mc = c - N_ATT_CHUNKS + MAIN_RET_COL
            main_ref[:, mc * COL_BLOCK:(mc + 1) * COL_BLOCK] = res.astype(BF16)
            continue
        part, group = divmod(c, N_GROUPS)
        if group == 0:
            main_ref[:, part * COL_BLOCK:(part + 1) * COL_BLOCK] = res.astype(BF16)
            continue
        d, dest = dilated[group]
        for s in range(n_slabs):
            slab_ref[s] = res[:, s * LANES:(s + 1) * LANES]
        for r in range(d):
            for s in range(n_slabs):
                lo = part * COL_BLOCK + s * LANES
                dest[r, :, lo:lo + LANES] = slab_ref[s, pl.ds(r, tm // d, stride=d), :].astype(BF16)


def _inproj(x2d, g, w_bf16, B, S, tm=512):
    T, D = x2d.shape
    n_main = (w_bf16.shape[1] // COL_BLOCK - N_ATT_CHUNKS + 3) * COL_BLOCK
    d4, d16 = DILATIONS[1], DILATIONS[2]
    per_b = S // tm
    return pl.pallas_call(
        _inproj_kernel,
        out_shape=(jax.ShapeDtypeStruct((T, n_main), BF16),
                   jax.ShapeDtypeStruct((B, d4, S // d4, 3 * GROUP_WIDTH), BF16),
                   jax.ShapeDtypeStruct((B, d16, S // d16, 3 * GROUP_WIDTH), BF16)),
        grid=(T // tm,),
        in_specs=[
            pl.BlockSpec((tm, D), lambda i: (i, 0)),
            pl.BlockSpec((1, D), lambda i: (0, 0)),
            pl.BlockSpec(w_bf16.shape, lambda i: (0, 0), pipeline_mode=pl.Buffered(1)),
        ],
        out_specs=(pl.BlockSpec((tm, n_main), lambda i: (i, 0)),
                   pl.BlockSpec((None, d4, tm // d4, 3 * GROUP_WIDTH), lambda i: (i // per_b, 0, i % per_b, 0)),
                   pl.BlockSpec((None, d16, tm // d16, 3 * GROUP_WIDTH), lambda i: (i // per_b, 0, i % per_b, 0))),
        scratch_shapes=[pltpu.VMEM((tm, D), BF16), pltpu.VMEM((COL_BLOCK // LANES, tm, LANES), F32)],
        compiler_params=_cparams(("arbitrary",), VMEM_LIMIT_BIG),
        name="inproj",
    )(x2d, g.reshape(1, D), w_bf16)


def _attn_block(q, kp, kc, vp, vc, bias_ref, has_prev):
    scale = HEAD_DIM ** -0.5
    heads = range(HEADS_PER_GROUP)
    hs = [slice(h * HEAD_DIM, (h + 1) * HEAD_DIM) for h in heads]
    rs = [slice(h * N_STEPS, (h + 1) * N_STEPS) for h in heads]
    k = jnp.concatenate([kp, kc], axis=0)
    v = jnp.concatenate([vp, vc], axis=0)
    dn = (((1,), (1,)), ((), ()))
    s = jnp.concatenate([lax.dot_general(q[:, hs[h]], k[:, hs[h]], dn, preferred_element_type=F32)
                         for h in heads], axis=0)
    s = s * scale + bias_ref[...]
    if has_prev is not True:
        col = lax.broadcasted_iota(jnp.int32, s.shape, 1)
        s = jnp.where((col >= N_STEPS) | has_prev, s, NEG_BIG)
    m = jnp.max(s, axis=-1, keepdims=True)
    p = jnp.exp(s - m)
    l = jnp.sum(p, axis=-1, keepdims=True)
    pb = p.astype(BF16)
    lse_rows = m + jnp.log(l)
    lane = lax.broadcasted_iota(jnp.int32, (N_STEPS, LANES), 1)
    outs = []
    lse = jnp.zeros((N_STEPS, LANES), F32)
    for h in heads:
        acc = jnp.dot(pb[rs[h], :], v[:, hs[h]], preferred_element_type=F32)
        outs.append(acc / l[rs[h], :])
        lse = jnp.where(lane == h, lse_rows[rs[h], :], lse)
    return jnp.concatenate(outs, axis=-1), lse


def _attn_kernel(q_ref, k_ref, v_ref, kprev_ref, vprev_ref, bias_ref, o_ref, lse_ref, *, blocks):
    n = pl.program_id(2)

    def run(j, kp, vp, has_prev):
        rows = pl.ds(pl.multiple_of(j * N_STEPS, N_STEPS), N_STEPS)
        o, lse = _attn_block(q_ref[rows, :], kp, k_ref[rows, :], vp, v_ref[rows, :], bias_ref, has_prev)
        o_ref[rows, :] = o.astype(o_ref.dtype)
        lse_ref[rows, :] = lse

    run(0, kprev_ref[...], vprev_ref[...], n > 0)

    def body(j, carry):
        prev = pl.ds(pl.multiple_of((j - 1) * N_STEPS, N_STEPS), N_STEPS)
        run(j, k_ref[prev, :], v_ref[prev, :], True)
        return carry

    if blocks > 1:
        lax.fori_loop(1, blocks, body, 0, unroll=True)


def _attention_group(qkv, bias, blocks):
    B, d, L, _ = qkv.shape
    W = GROUP_WIDTH
    rows = blocks * N_STEPS
    steps = L // rows

    def cur(c):
        return pl.BlockSpec((None, None, rows, W), lambda b, r, n: (b, r, n, c))

    def prev(c):
        return pl.BlockSpec((None, None, N_STEPS, W), lambda b, r, n: (b, r, jnp.maximum(n * blocks - 1, 0), c))

    return pl.pallas_call(
        functools.partial(_attn_kernel, blocks=blocks),
        out_shape=(jax.ShapeDtypeStruct((B, d, L, W), BF16), jax.ShapeDtypeStruct((B, d, L, LANES), F32)),
        grid=(B, d, steps),
        in_specs=[cur(0), cur(1), cur(2), prev(1), prev(2),
                  pl.BlockSpec((HEADS_PER_GROUP * N_STEPS, 2 * N_STEPS), lambda b, r, n: (0, 0))],
        out_specs=(pl.BlockSpec((None, None, rows, W), lambda b, r, n: (b, r, n, 0)),
                   pl.BlockSpec((None, None, rows, LANES), lambda b, r, n: (b, r, n, 0))),
        compiler_params=_cparams(("arbitrary", "arbitrary", "arbitrary")),
        name=f"attn_d{d}",
    )(qkv, qkv, qkv, qkv, qkv, bias)


def _t5_bucket(dist):
    max_exact = REL_BUCKETS // 2
    d_f = jnp.maximum(dist, 1).astype(F32)
    large = max_exact + (jnp.log(d_f / max_exact) / math.log(REL_MAX_DIST / max_exact)
                         * (REL_BUCKETS - max_exact)).astype(jnp.int32)
    large = jnp.minimum(large, REL_BUCKETS - 1)
    return jnp.where(dist < max_exact, dist, large)


def _band_bias(rel_bias_g, dilation):
    qi = jnp.arange(N_STEPS)[:, None]
    kj = jnp.arange(2 * N_STEPS)[None, :]
    step_dist = qi + N_STEPS - kj
    band = (step_dist >= 0) & (step_dist <= N_STEPS)
    bucket = _t5_bucket(jnp.maximum(step_dist, 0) * dilation)
    table = rel_bias_g.astype(F32).T
    hit = bucket[None, None] == jnp.arange(REL_BUCKETS)[None, :, None, None]
    bias = jnp.sum(jnp.where(hit, table[:, :, None, None], 0.0), axis=1)
    return jnp.where(band[None], bias, NEG_BIG).reshape(HEADS_PER_GROUP * N_STEPS, 2 * N_STEPS)


def _ret_kernel(q_ref, k_ref, v0_ref, v1_ref, g0_ref, g1_ref, cos_ref, sin_ref, decay_ref, xi_ref, zeta_ref,
                o_ref, state_ref, *, g_chunk):
    n = pl.program_id(1)

    @pl.when(n == 0)
    def _():
        state_ref[...] = jnp.zeros_like(state_ref)

    cos = cos_ref[...]
    sin = sin_ref[...]
    lane = lax.broadcasted_iota(jnp.int32, (RET_CHUNK, RET_DK), 1)
    even = (lane % 2) == 0

    def rotary(t):
        partner = jnp.where(even, pltpu.roll(t, RET_DK - 1, 1), pltpu.roll(t, 1, 1))
        return t * cos + partner * sin

    for h in range(RET_HEADS):
        ks = slice(h * RET_DK, (h + 1) * RET_DK)
        qr = rotary(q_ref[:, ks].astype(F32))
        kr = rotary(k_ref[:, ks].astype(F32)) * (RET_DK ** -0.5)
        v_ref = v0_ref if h < 2 else v1_ref
        g_ref = g0_ref if h < 2 else g1_ref
        vs = slice((h % 2) * RET_DV, (h % 2 + 1) * RET_DV)
        v = v_ref[:, vs]
        qb = qr.astype(BF16)
        kb = kr.astype(BF16)
        scores = lax.dot_general(qb, kb, (((1,), (1,)), ((), ())), preferred_element_type=F32) * decay_ref[h]
        inner = jnp.dot(scores.astype(BF16), v, preferred_element_type=F32)
        state = state_ref[h]
        cross = jnp.dot(qb, state.astype(BF16), preferred_element_type=F32) * xi_ref[h]
        kz = (kr * zeta_ref[h]).astype(BF16)
        state_ref[h] = state * g_chunk[h] + lax.dot_general(kz, v, (((0,), (0,)), ((), ())),
                                                            preferred_element_type=F32)
        ret = inner + cross
        mu = jnp.mean(ret, axis=-1, keepdims=True)
        cen = ret - mu
        var = jnp.mean(cen * cen, axis=-1, keepdims=True)
        normed = cen * lax.rsqrt(var + NORM_EPS)
        g = g_ref[:, vs].astype(F32)
        o_ref[:, h * RET_DV:(h + 1) * RET_DV] = (g * jax.nn.sigmoid(g) * normed).astype(o_ref.dtype)


def _retention(proj, B, S, col0):
    T = B * S
    nc = S // RET_CHUNK
    C = RET_CHUNK
    log_g = np.log(1.0 - 2.0 ** (-5.0 - np.arange(RET_HEADS, dtype=np.float64)))
    idx = np.arange(C, dtype=np.float64)
    diff = idx[:, None] - idx[None, :]
    decay = np.where(diff >= 0, np.exp(np.maximum(diff, 0.0)[None] * log_g[:, None, None]), 0.0)
    xi = np.exp((idx + 1.0)[None, :] * log_g[:, None])
    zeta = np.exp((C - 1.0 - idx)[None, :] * log_g[:, None])
    g_chunk = tuple(float(v) for v in np.exp(C * log_g))
    xi_b = np.broadcast_to(xi[:, :, None], (RET_HEADS, C, RET_DV)).astype(np.float32)
    zeta_b = np.broadcast_to(zeta[:, :, None], (RET_HEADS, C, RET_DK)).astype(np.float32)

    inv = ROPE_BASE ** (-np.arange(0, RET_DK, 2, dtype=np.float64) / RET_DK)
    ang = np.arange(S, dtype=np.float64)[:, None] * inv[None]
    cos_t = np.repeat(np.cos(ang), 2, axis=1).astype(np.float32)
    sin_t = np.stack([-np.sin(ang), np.sin(ang)], axis=-1).reshape(S, RET_DK).astype(np.float32)

    def col(c):
        return pl.BlockSpec((C, COL_BLOCK), lambda b, n, c=c: (b * nc + n, c))

    const3 = lambda shape: pl.BlockSpec(shape, lambda b, n: (0, 0, 0))
    return pl.pallas_call(
        functools.partial(_ret_kernel, g_chunk=g_chunk),
        out_shape=jax.ShapeDtypeStruct((T, RET_HEADS * RET_DV), BF16),
        grid=(B, nc),
        in_specs=[col(col0), col(col0 + 1), col(col0 + 2), col(col0 + 3), col(col0 + 4), col(col0 + 5),
                  pl.BlockSpec((C, RET_DK), lambda b, n: (n, 0)),
                  pl.BlockSpec((C, RET_DK), lambda b, n: (n, 0)),
                  const3((RET_HEADS, C, C)), const3((RET_HEADS, C, RET_DV)), const3((RET_HEADS, C, RET_DK))],
        out_specs=pl.BlockSpec((C, RET_HEADS * RET_DV), lambda b, n: (b * nc + n, 0)),
        scratch_shapes=[pltpu.VMEM((RET_HEADS, RET_DK, RET_DV), F32)],
        compiler_params=_cparams(("arbitrary", "arbitrary")),
        name="retention",
    )(proj, proj, proj, proj, proj, proj, jnp.asarray(cos_t), jnp.asarray(sin_t),
      jnp.asarray(decay.astype(np.float32)), jnp.asarray(xi_b), jnp.asarray(zeta_b))


ROUTE_IDX, ROUTE_RANK, ROUTE_GATE = 0, 4, 8


def _to_token_order(src_ref, dst_ref, d):
    n = src_ref.shape[1]
    for r in range(d):
        blk = src_ref[r].astype(F32)
        for s in range(dst_ref.shape[0]):
            dst_ref[s, pl.ds(r, n, stride=d), :] = blk[:, s * LANES:(s + 1) * LANES]


def _merge_kernel(x_ref, o1_ref, o2_ref, o3_ref, l1_ref, l2_ref, l3_ref, ret_ref,
                  ga0_ref, ga1_ref, gr0_ref, gr1_ref, watt_ref, wret_ref, wout_ref, gffn_ref, wr_ref, br_ref,
                  below_ref, x1_ref, hf_ref, route_ref, counts_ref, on2_ref, on3_ref, ln2_ref, ln3_ref):
    tm = x_ref.shape[0]
    _to_token_order(o2_ref, on2_ref, DILATIONS[1])
    _to_token_order(o3_ref, on3_ref, DILATIONS[2])
    _to_token_order(l2_ref, ln2_ref, DILATIONS[1])
    _to_token_order(l3_ref, ln3_ref, DILATIONS[2])
    lses = [l1_ref[...], ln2_ref[0], ln3_ref[0]]

    att_parts = []
    for h in range(HEADS_PER_GROUP):
        ls = [l[:, h:h + 1] for l in lses]
        m = jnp.maximum(jnp.maximum(ls[0], ls[1]), ls[2])
        ws = [jnp.exp(l - m) for l in ls]
        wsum = ws[0] + ws[1] + ws[2]
        hs = slice(h * HEAD_DIM, (h + 1) * HEAD_DIM)
        acc = (ws[0] / wsum) * o1_ref[:, hs].astype(F32)
        acc = acc + (ws[1] / wsum) * on2_ref[h]
        acc = acc + (ws[2] / wsum) * on3_ref[h]
        att_parts.append(acc)
    att = jnp.concatenate(att_parts, axis=-1).astype(BF16)

    a_proj = jnp.dot(att, watt_ref[...], preferred_element_type=F32)
    r_proj = jnp.dot(ret_ref[...], wret_ref[...], preferred_element_type=F32)
    gate_a = jnp.concatenate([ga0_ref[...], ga1_ref[...]], axis=-1).astype(F32)
    gate_r = jnp.concatenate([gr0_ref[...], gr1_ref[...]], axis=-1).astype(F32)
    merged = jax.nn.sigmoid(gate_a) * a_proj + jax.nn.sigmoid(gate_r) * r_proj
    x1 = x_ref[...] + jnp.dot(merged.astype(BF16), wout_ref[...], preferred_element_type=F32)
    x1_ref[...] = x1

    ms = jnp.mean(x1 * x1, axis=-1, keepdims=True)
    hf = x1 * lax.rsqrt(ms + NORM_EPS) * gffn_ref[...]
    hf_ref[...] = hf

    logits = jnp.dot(hf.astype(BF16), wr_ref[...], preferred_element_type=F32) + br_ref[...]
    lane = lax.broadcasted_iota(jnp.int32, (tm, LANES), 1)
    lane_f = lane.astype(F32)
    work = logits
    vals, idxs, hits = [], [], []
    for _ in range(TOP_K):
        mk = jnp.max(work, axis=-1, keepdims=True)
        ik = jnp.min(jnp.where(work == mk, lane_f, float(LANES)), axis=-1, keepdims=True)
        hit = lane_f == ik
        work = jnp.where(hit, -jnp.inf, work)
        vals.append(mk)
        idxs.append(ik)
        hits.append(hit)
    es = [jnp.exp(v - vals[0]) for v in vals]
    esum = es[0] + es[1] + es[2] + es[3]
    gates = [e / esum for e in es]

    sel = jnp.zeros((tm, LANES), F32)
    for hit in hits:
        sel = jnp.where(hit, 1.0, sel)

    @pl.when(pl.program_id(0) == 0)
    def _():
        counts_ref[...] = jnp.zeros_like(counts_ref)

    rank = jnp.dot(below_ref[...], sel.astype(BF16), preferred_element_type=F32) + counts_ref[...]
    counts_ref[...] = counts_ref[...] + jnp.sum(sel, axis=0, keepdims=True)

    route = jnp.zeros((tm, LANES), F32)
    for k in range(TOP_K):
        rank_k = jnp.sum(jnp.where(hits[k], rank, 0.0), axis=-1, keepdims=True)
        route = jnp.where(lane == ROUTE_IDX + k, idxs[k], route)
        route = jnp.where(lane == ROUTE_RANK + k, rank_k, route)
        route = jnp.where(lane == ROUTE_GATE + k, gates[k], route)
    route_ref[...] = route


def _merge(x2d, B, S, o1, l1, o2, l2, o3, l3, ret, proj, gate_col0, w_att, w_ret, w_out, g_ffn, w_router_p,
           b_router_p, tm=256):
    T, D = x2d.shape
    per_b = S // tm
    d4, d16 = DILATIONS[1], DILATIONS[2]
    row = lambda w: pl.BlockSpec((tm, w), lambda i: (i, 0))
    full = lambda a: pl.BlockSpec(a.shape, lambda i: (0,) * a.ndim)
    gcol = lambda c: pl.BlockSpec((tm, COL_BLOCK), lambda i, c=c: (i, c))
    res = lambda d, w: pl.BlockSpec((None, d, tm // d, w), lambda i: (i // per_b, 0, i % per_b, 0))
    below = jnp.tril(jnp.ones((tm, tm), BF16), k=-1)
    slabs = lambda n: pltpu.VMEM((n, tm, LANES), F32)
    return pl.pallas_call(
        _merge_kernel,
        out_shape=(jax.ShapeDtypeStruct((T, D), F32), jax.ShapeDtypeStruct((T, D), F32),
                   jax.ShapeDtypeStruct((T, LANES), F32), jax.ShapeDtypeStruct((1, LANES), F32)),
        grid=(T // tm,),
        in_specs=[row(D), row(GROUP_WIDTH), res(d4, GROUP_WIDTH), res(d16, GROUP_WIDTH),
                  row(LANES), res(d4, LANES), res(d16, LANES),
                  row(RET_HEADS * RET_DV), gcol(gate_col0), gcol(gate_col0 + 1), gcol(gate_col0 + 2),
                  gcol(gate_col0 + 3), full(w_att), full(w_ret), full(w_out), full(g_ffn), full(w_router_p),
                  full(b_router_p), full(below)],
        out_specs=(row(D), row(D), row(LANES), pl.BlockSpec((1, LANES), lambda i: (0, 0))),
        scratch_shapes=[slabs(GROUP_WIDTH // LANES), slabs(GROUP_WIDTH // LANES), slabs(1), slabs(1)],
        compiler_params=_cparams(("arbitrary",)),
        name="merge",
    )(x2d, o1, o2, o3, l1, l2, l3, ret, proj, proj, proj, proj, w_att, w_ret, w_out, g_ffn, w_router_p,
      b_router_p, below)


def _row_copy(src, dst, src_row, dst_row, sem):
    return pltpu.make_async_copy(src.at[pl.ds(src_row, 1)], dst.at[pl.ds(dst_row, 1)], sem)


def _dispatch_kernel(pad_ref, pos_ref, hf_ref, xs_hbm, zero_ref, sem, *, tile):
    @pl.when(pl.program_id(0) == 0)
    def _():
        zero_ref[...] = jnp.zeros_like(zero_ref)
        total = 0
        for e in range(N_EXPERTS):
            first = pad_ref[e]

            def zero_row(j, carry, first=first):
                _row_copy(zero_ref, xs_hbm, 0, first + j, sem).start()
                return carry

            lax.fori_loop(0, pad_ref[N_EXPERTS + e], zero_row, 0)
            total = total + pad_ref[N_EXPERTS + e]

        def zero_wait(n, carry):
            _row_copy(zero_ref, xs_hbm, 0, 0, sem).wait()
            return carry

        lax.fori_loop(0, total, zero_wait, 0)

    def issue(i, carry):
        for k in range(TOP_K):
            _row_copy(hf_ref, xs_hbm, i, pos_ref[0, k, i], sem).start(priority=k % 2)
        return carry

    lax.fori_loop(0, tile, issue, 0, unroll=8)

    def drain(i, carry):
        for k in range(TOP_K):
            _row_copy(hf_ref, xs_hbm, 0, 0, sem).wait()
        return carry

    lax.fori_loop(0, tile, drain, 0, unroll=8)


def _dispatch(hf, pos_tiles, pad_start, n_rows, tile):
    T, D = hf.shape
    grid_spec = pltpu.PrefetchScalarGridSpec(
        num_scalar_prefetch=1,
        grid=(T // tile,),
        in_specs=[pl.BlockSpec((1, TOP_K, tile), lambda i, pad: (i, 0, 0), memory_space=pltpu.SMEM),
                  pl.BlockSpec((tile, D), lambda i, pad: (i, 0))],
        out_specs=pl.BlockSpec(memory_space=pl.ANY),
        scratch_shapes=[pltpu.VMEM((8, D), hf.dtype), pltpu.SemaphoreType.DMA(())],
    )
    return pl.pallas_call(
        functools.partial(_dispatch_kernel, tile=tile),
        out_shape=jax.ShapeDtypeStruct((n_rows, D), hf.dtype),
        grid_spec=grid_spec,
        compiler_params=_cparams(("arbitrary",)),
        name="dispatch",
    )(pad_start, pos_tiles, hf)


def _expert_kernel(be_ref, nu_ref, x_ref, w1_ref, b1_ref, w2_ref, b2_ref, y_ref, w1b_ref, w2b_ref):
    i = pl.program_id(0)
    F = w2_ref.shape[0]

    @pl.when(i < nu_ref[0])
    def _():
        @pl.when((i == 0) | (be_ref[i] != be_ref[jnp.maximum(i - 1, 0)]))
        def _():
            w1b_ref[...] = w1_ref[...].astype(BF16)
            w2b_ref[...] = w2_ref[...].astype(BF16)

        gu = jnp.dot(x_ref[...].astype(BF16), w1b_ref[...], preferred_element_type=F32) + b1_ref[...]
        gate = jnp.minimum(gu[:, :F], SWIGLU_LIMIT)
        up = jnp.clip(gu[:, F:], -SWIGLU_LIMIT, SWIGLU_LIMIT)
        act = (up + 1.0) * gate * jax.nn.sigmoid(SWIGLU_ALPHA * gate)
        y_ref[...] = jnp.dot(act.astype(BF16), w2b_ref[...], preferred_element_type=F32) + b2_ref[...]


def _experts(xs, n_blocks, blk_expert, n_used, w1, b1, w2, b2):
    D = xs.shape[1]
    E, _, F2 = w1.shape
    F = w2.shape[1]
    M = EXPERT_BLOCK
    blk = lambda i, be, nu: (jnp.minimum(i, nu[0] - 1), 0)
    grid_spec = pltpu.PrefetchScalarGridSpec(
        num_scalar_prefetch=2,
        grid=(n_blocks,),
        in_specs=[pl.BlockSpec((M, D), blk),
                  pl.BlockSpec((None, D, F2), lambda i, be, nu: (be[i], 0, 0)),
                  pl.BlockSpec((None, 1, F2), lambda i, be, nu: (be[i], 0, 0)),
                  pl.BlockSpec((None, F, D), lambda i, be, nu: (be[i], 0, 0)),
                  pl.BlockSpec((None, 1, D), lambda i, be, nu: (be[i], 0, 0))],
        out_specs=pl.BlockSpec((M, D), blk),
        scratch_shapes=[pltpu.VMEM((D, F2), BF16), pltpu.VMEM((F, D), BF16)],
    )
    return pl.pallas_call(
        _expert_kernel,
        out_shape=jax.ShapeDtypeStruct((n_blocks * M, D), F32),
        grid_spec=grid_spec,
        compiler_params=_cparams(("arbitrary",), VMEM_LIMIT_BIG),
        name="experts",
    )(blk_expert, n_used, xs, w1, b1.reshape(E, 1, F2), w2, b2.reshape(E, 1, D))


def _combine_kernel(pos_ref, route_ref, x1_ref, g_ref, ys_hbm, o_ref, buf_ref, sem, *, tile):
    def issue(i, carry):
        for k in range(TOP_K):
            pltpu.make_async_copy(ys_hbm.at[pl.ds(pos_ref[0, k, i], 1)], buf_ref.at[k, pl.ds(i, 1)],
                                  sem).start(priority=k % 2)
        return carry

    lax.fori_loop(0, tile, issue, 0, unroll=8)

    def drain(i, carry):
        for k in range(TOP_K):
            pltpu.make_async_copy(ys_hbm.at[pl.ds(0, 1)], buf_ref.at[k, pl.ds(0, 1)], sem).wait()
        return carry

    lax.fori_loop(0, tile, drain, 0, unroll=8)

    route = route_ref[...]
    acc = x1_ref[...]
    for k in range(TOP_K):
        acc = acc + route[:, ROUTE_GATE + k:ROUTE_GATE + k + 1] * buf_ref[k]
    ms = jnp.mean(acc * acc, axis=-1, keepdims=True)
    o_ref[...] = acc * lax.rsqrt(ms + NORM_EPS) * g_ref[...]


def _combine(pos_tiles, route, x1, g_final, ys, tile):
    T, D = x1.shape
    return pl.pallas_call(
        functools.partial(_combine_kernel, tile=tile),
        out_shape=jax.ShapeDtypeStruct((T, D), F32),
        grid=(T // tile,),
        in_specs=[pl.BlockSpec((1, TOP_K, tile), lambda i: (i, 0, 0), memory_space=pltpu.SMEM),
                  pl.BlockSpec((tile, LANES), lambda i: (i, 0)),
                  pl.BlockSpec((tile, D), lambda i: (i, 0)),
                  pl.BlockSpec((1, D), lambda i: (0, 0)),
                  pl.BlockSpec(memory_space=pl.ANY)],
        out_specs=pl.BlockSpec((tile, D), lambda i: (i, 0)),
        scratch_shapes=[pltpu.VMEM((TOP_K, tile, D), F32), pltpu.SemaphoreType.DMA(())],
        compiler_params=_cparams(("arbitrary",)),
        name="combine",
    )(pos_tiles, route, x1, g_final.reshape(1, D), ys)


def _layer(x2d, B, S, norm_mix_g, w_in, rel_bias, w_att, w_ret, w_out, norm_ffn_g, w_router, b_router,
           w1, b1, w2, b2):
    T, D = x2d.shape
    proj, a4, a16 = _inproj(x2d, norm_mix_g, w_in.astype(BF16), B, S)

    group_bias = [_band_bias(rel_bias[:, g * HEADS_PER_GROUP:(g + 1) * HEADS_PER_GROUP], d)
                  for g, d in enumerate(DILATIONS)]
    o1, l1 = _attention_group(proj.reshape(B, 1, S, proj.shape[1]), group_bias[0],
                              blocks=min(S // N_STEPS, 8))
    o2, l2 = _attention_group(a4, group_bias[1], blocks=min(a4.shape[2] // N_STEPS, 8))
    o3, l3 = _attention_group(a16, group_bias[2], blocks=min(a16.shape[2] // N_STEPS, 8))

    ret = _retention(proj, B, S, MAIN_RET_COL)

    w_router_p = jnp.zeros((D, LANES), BF16).at[:, :N_EXPERTS].set(w_router.astype(BF16))
    b_router_p = jnp.full((1, LANES), NEG_BIG, F32).at[0, :N_EXPERTS].set(b_router.astype(F32))
    x1, hf, route, counts = _merge(x2d, B, S, o1.reshape(T, GROUP_WIDTH), l1.reshape(T, LANES), o2, l2, o3, l3,
                                   ret, proj, MAIN_GATE_COL, w_att.astype(BF16), w_ret.astype(BF16),
                                   w_out.astype(BF16), norm_ffn_g.reshape(1, D), w_router_p, b_router_p)

    M = EXPERT_BLOCK
    n_blocks = (T * TOP_K) // M + N_EXPERTS
    cnt = counts[0, :N_EXPERTS].astype(jnp.int32)
    padded = ((cnt + M - 1) // M) * M
    pend = jnp.cumsum(padded)
    pstart = pend - padded
    idx = route[:, ROUTE_IDX:ROUTE_IDX + TOP_K].astype(jnp.int32)
    rank = route[:, ROUTE_RANK:ROUTE_RANK + TOP_K].astype(jnp.int32)
    experts = jnp.arange(N_EXPERTS, dtype=jnp.int32)
    pos = rank + jnp.sum(jnp.where(idx[..., None] == experts, pstart, 0), axis=-1)
    n_used = (pend[-1] // M).astype(jnp.int32)
    blk_row = jnp.minimum(jnp.arange(n_blocks, dtype=jnp.int32), jnp.maximum(n_used - 1, 0)) * M
    blk_expert = jnp.minimum(jnp.sum((pend[None, :] <= blk_row[:, None]).astype(jnp.int32), axis=-1),
                             N_EXPERTS - 1)

    tile = 256
    pos_tiles = pos.reshape(T // tile, tile, TOP_K).transpose(0, 2, 1)
    pad_info = jnp.concatenate([pstart + cnt, padded - cnt]).astype(jnp.int32)
    xs = _dispatch(hf, pos_tiles, pad_info, n_blocks * M, tile)
    ys = _experts(xs, n_blocks, blk_expert, n_used.reshape(1), w1, b1, w2, b2)
    return pos_tiles, route, x1, ys


def kernel(x, norm_mix_g, w_in, rel_bias, w_att_branch, w_ret_branch, w_out, norm_ffn_g, w_router, b_router,
           w1, b1, w2, b2, norm_final_g):
    B, S, D = x.shape
    depth = w_in.shape[0]
    assert depth == 1, "the combine stage applies the final norm; a deeper stack needs a separate norm pass"
    x2d = x.reshape(B * S, D)
    pos_tiles, route, x1, ys = _layer(x2d, B, S, norm_mix_g[0], w_in[0], rel_bias, w_att_branch[0],
                                      w_ret_branch[0], w_out[0], norm_ffn_g[0], w_router[0], b_router[0],
                                      w1[0], b1[0], w2[0], b2[0])
    out = _combine(pos_tiles, route, x1, norm_final_g, ys, tile=256)
    return out.reshape(B, S, D)
```

```python
import functools
import math

import numpy as np
import jax
import jax.numpy as jnp
from jax import lax
from jax.experimental import pallas as pl
from jax.experimental.pallas import tpu as pltpu

F32 = jnp.float32
BF16 = jnp.bfloat16

NORM_EPS = 1e-5

HEAD_DIM = 128
HEADS_PER_GROUP = 4
GROUP_WIDTH = HEADS_PER_GROUP * HEAD_DIM
DILATIONS = (1, 4, 16)
N_STEPS = 128
N_GROUPS = len(DILATIONS)
REL_BUCKETS = 32
REL_MAX_DIST = 2048

RET_HEADS = 4
RET_DK = 128
RET_DV = 256
RET_CHUNK = 128
ROPE_BASE = 10000.0

N_EXPERTS = 32
TOP_K = 4
SWIGLU_LIMIT = 7.0
SWIGLU_ALPHA = 1.702
EXPERT_BLOCK = 256

LANES = 128
COL_BLOCK = 512
NEG_BIG = -1e30

VMEM_LIMIT = 48 * 1024 * 1024
VMEM_LIMIT_BIG = 56 * 1024 * 1024

N_ATT_CHUNKS = 3 * N_GROUPS
MAIN_RET_COL = 3
MAIN_GATE_COL = MAIN_RET_COL + (2 * RET_HEADS * RET_DK + 2 * RET_HEADS * RET_DV) // COL_BLOCK


def _cparams(sem, vmem=VMEM_LIMIT):
    return pltpu.CompilerParams(dimension_semantics=sem, vmem_limit_bytes=vmem)


def _inproj_kernel(x_ref, g_ref, w_ref, main_ref, a4_ref, a16_ref, h_ref, slab_ref):
    tm = x_ref.shape[0]
    x = x_ref[...]
    ms = jnp.mean(x * x, axis=-1, keepdims=True)
    h_ref[...] = (x * lax.rsqrt(ms + NORM_EPS) * g_ref[...]).astype(BF16)
    n_chunks = w_ref.shape[1] // COL_BLOCK
    n_slabs = COL_BLOCK // LANES
    dilated = {1: (DILATIONS[1], a4_ref), 2: (DILATIONS[2], a16_ref)}
    for c in range(n_chunks):
        res = jnp.dot(h_ref[...], w_ref[:, c * COL_BLOCK:(c + 1) * COL_BLOCK], preferred_element_type=F32)
        if c >= N_ATT_CHUNKS:
            mc = c - N_ATT_CHUNKS + MAIN_RET_COL
            main_ref[:, mc * COL_BLOCK:(mc + 1) * COL_BLOCK] = res.astype(BF16)
            continue
        part, group = divmod(c, N_GROUPS)
        if group == 0:
            main_ref[:, part * COL_BLOCK:(part + 1) * COL_BLOCK] = res.astype(BF16)
            continue
        d, dest = dilated[group]
        for s in range(n_slabs):
            slab_ref[s] = res[:, s * LANES:(s + 1) * LANES]
        for r in range(d):
            for s in range(n_slabs):
                lo = part * COL_BLOCK + s * LANES
                dest[r, :, lo:lo + LANES] = slab_ref[s, pl.ds(r, tm // d, stride=d), :].astype(BF16)


def _inproj(x2d, g, w_bf16, B, S, tm=512):
    T, D = x2d.shape
    n_main = (w_bf16.shape[1] // COL_BLOCK - N_ATT_CHUNKS + 3) * COL_BLOCK
    d4, d16 = DILATIONS[1], DILATIONS[2]
    per_b = S // tm
    return pl.pallas_call(
        _inproj_kernel,
        out_shape=(jax.ShapeDtypeStruct((T, n_main), BF16),
                   jax.ShapeDtypeStruct((B, d4, S // d4, 3 * GROUP_WIDTH), BF16),
                   jax.ShapeDtypeStruct((B, d16, S // d16, 3 * GROUP_WIDTH), BF16)),
        grid=(T // tm,),
        in_specs=[
            pl.BlockSpec((tm, D), lambda i: (i, 0)),
            pl.BlockSpec((1, D), lambda i: (0, 0)),
            pl.BlockSpec(w_bf16.shape, lambda i: (0, 0), pipeline_mode=pl.Buffered(1)),
        ],
        out_specs=(pl.BlockSpec((tm, n_main), lambda i: (i, 0)),
                   pl.BlockSpec((None, d4, tm // d4, 3 * GROUP_WIDTH), lambda i: (i // per_b, 0, i % per_b, 0)),
                   pl.BlockSpec((None, d16, tm // d16, 3 * GROUP_WIDTH), lambda i: (i // per_b, 0, i % per_b, 0))),
        scratch_shapes=[pltpu.VMEM((tm, D), BF16), pltpu.VMEM((COL_BLOCK // LANES, tm, LANES), F32)],
        compiler_params=_cparams(("arbitrary",), VMEM_LIMIT_BIG),
        name="inproj",
    )(x2d, g.reshape(1, D), w_bf16)


def _attn_block(q, kp, kc, vp, vc, bias_ref, has_prev):
    scale = HEAD_DIM ** -0.5
    heads = range(HEADS_PER_GROUP)
    hs = [slice(h * HEAD_DIM, (h + 1) * HEAD_DIM) for h in heads]
    rs = [slice(h * N_STEPS, (h + 1) * N_STEPS) for h in heads]
    k = jnp.concatenate([kp, kc], axis=0)
    v = jnp.concatenate([vp, vc], axis=0)
    dn = (((1,), (1,)), ((), ()))
    s = jnp.concatenate([lax.dot_general(q[:, hs[h]], k[:, hs[h]], dn, preferred_element_type=F32)
                         for h in heads], axis=0)
    s = s * scale + bias_ref[...]
    if has_prev is not True:
        col = lax.broadcasted_iota(jnp.int32, s.shape, 1)
        s = jnp.where((col >= N_STEPS) | has_prev, s, NEG_BIG)
    m = jnp.max(s, axis=-1, keepdims=True)
    p = jnp.exp(s - m)
    l = jnp.sum(p, axis=-1, keepdims=True)
    pb = p.astype(BF16)
    lse_rows = m + jnp.log(l)
    lane = lax.broadcasted_iota(jnp.int32, (N_STEPS, LANES), 1)
    outs = []
    lse = jnp.zeros((N_STEPS, LANES), F32)
    for h in heads:
        acc = jnp.dot(pb[rs[h], :], v[:, hs[h]], preferred_element_type=F32)
        outs.append(acc / l[rs[h], :])
        lse = jnp.where(lane == h, lse_rows[rs[h], :], lse)
    return jnp.concatenate(outs, axis=-1), lse


def _attn_kernel(q_ref, k_ref, v_ref, kprev_ref, vprev_ref, bias_ref, o_ref, lse_ref, *, blocks):
    n = pl.program_id(2)

    def run(j, kp, vp, has_prev):
        rows = pl.ds(pl.multiple_of(j * N_STEPS, N_STEPS), N_STEPS)
        o, lse = _attn_block(q_ref[rows, :], kp, k_ref[rows, :], vp, v_ref[rows, :], bias_ref, has_prev)
        o_ref[rows, :] = o.astype(o_ref.dtype)
        lse_ref[rows, :] = lse

    run(0, kprev_ref[...], vprev_ref[...], n > 0)

    def body(j, carry):
        prev = pl.ds(pl.multiple_of((j - 1) * N_STEPS, N_STEPS), N_STEPS)
        run(j, k_ref[prev, :], v_ref[prev, :], True)
        return carry

    if blocks > 1:
        lax.fori_loop(1, blocks, body, 0, unroll=True)


def _attention_group(qkv, bias, blocks):
    B, d, L, _ = qkv.shape
    W = GROUP_WIDTH
    rows = blocks * N_STEPS
    steps = L // rows

    def cur(c):
        return pl.BlockSpec((None, None, rows, W), lambda b, r, n: (b, r, n, c))

    def prev(c):
        return pl.BlockSpec((None, None, N_STEPS, W), lambda b, r, n: (b, r, jnp.maximum(n * blocks - 1, 0), c))

    return pl.pallas_call(
        functools.partial(_attn_kernel, blocks=blocks),
        out_shape=(jax.ShapeDtypeStruct((B, d, L, W), BF16), jax.ShapeDtypeStruct((B, d, L, LANES), F32)),
        grid=(B, d, steps),
        in_specs=[cur(0), cur(1), cur(2), prev(1), prev(2),
                  pl.BlockSpec((HEADS_PER_GROUP * N_STEPS, 2 * N_STEPS), lambda b, r, n: (0, 0))],
        out_specs=(pl.BlockSpec((None, None, rows, W), lambda b, r, n: (b, r, n, 0)),
                   pl.BlockSpec((None, None, rows, LANES), lambda b, r, n: (b, r, n, 0))),
        compiler_params=_cparams(("arbitrary", "arbitrary", "arbitrary")),
        name=f"attn_d{d}",
    )(qkv, qkv, qkv, qkv, qkv, bias)


def _t5_bucket(dist):
    max_exact = REL_BUCKETS // 2
    d_f = jnp.maximum(dist, 1).astype(F32)
    large = max_exact + (jnp.log(d_f / max_exact) / math.log(REL_MAX_DIST / max_exact)
                         * (REL_BUCKETS - max_exact)).astype(jnp.int32)
    large = jnp.minimum(large, REL_BUCKETS - 1)
    return jnp.where(dist < max_exact, dist, large)


def _band_bias(rel_bias_g, dilation):
    qi = jnp.arange(N_STEPS)[:, None]
    kj = jnp.arange(2 * N_STEPS)[None, :]
    step_dist = qi + N_STEPS - kj
    band = (step_dist >= 0) & (step_dist <= N_STEPS)
    bucket = _t5_bucket(jnp.maximum(step_dist, 0) * dilation)
    table = rel_bias_g.astype(F32).T
    hit = bucket[None, None] == jnp.arange(REL_BUCKETS)[None, :, None, None]
    bias = jnp.sum(jnp.where(hit, table[:, :, None, None], 0.0), axis=1)
    return jnp.where(band[None], bias, NEG_BIG).reshape(HEADS_PER_GROUP * N_STEPS, 2 * N_STEPS)


def _ret_kernel(q_ref, k_ref, v0_ref, v1_ref, g0_ref, g1_ref, cos_ref, sin_ref, decay_ref, xi_ref, zeta_ref,
                o_ref, state_ref, *, g_chunk):
    n = pl.program_id(1)

    @pl.when(n == 0)
    def _():
        state_ref[...] = jnp.zeros_like(state_ref)

    cos = cos_ref[...]
    sin = sin_ref[...]
    lane = lax.broadcasted_iota(jnp.int32, (RET_CHUNK, RET_DK), 1)
    even = (lane % 2) == 0

    def rotary(t):
        partner = jnp.where(even, pltpu.roll(t, RET_DK - 1, 1), pltpu.roll(t, 1, 1))
        return t * cos + partner * sin

    for h in range(RET_HEADS):
        ks = slice(h * RET_DK, (h + 1) * RET_DK)
        qr = rotary(q_ref[:, ks].astype(F32))
        kr = rotary(k_ref[:, ks].astype(F32)) * (RET_DK ** -0.5)
        v_ref = v0_ref if h < 2 else v1_ref
        g_ref = g0_ref if h < 2 else g1_ref
        vs = slice((h % 2) * RET_DV, (h % 2 + 1) * RET_DV)
        v = v_ref[:, vs]
        qb = qr.astype(BF16)
        kb = kr.astype(BF16)
        scores = lax.dot_general(qb, kb, (((1,), (1,)), ((), ())), preferred_element_type=F32) * decay_ref[h]
        inner = jnp.dot(scores.astype(BF16), v, preferred_element_type=F32)
        state = state_ref[h]
        cross = jnp.dot(qb, state.astype(BF16), preferred_element_type=F32) * xi_ref[h]
        kz = (kr * zeta_ref[h]).astype(BF16)
        state_ref[h] = state * g_chunk[h] + lax.dot_general(kz, v, (((0,), (0,)), ((), ())),
                                                            preferred_element_type=F32)
        ret = inner + cross
        mu = jnp.mean(ret, axis=-1, keepdims=True)
        cen = ret - mu
        var = jnp.mean(cen * cen, axis=-1, keepdims=True)
        normed = cen * lax.rsqrt(var + NORM_EPS)
        g = g_ref[:, vs].astype(F32)
        o_ref[:, h * RET_DV:(h + 1) * RET_DV] = (g * jax.nn.sigmoid(g) * normed).astype(o_ref.dtype)


def _retention(proj, B, S, col0):
    T = B * S
    nc = S // RET_CHUNK
    C = RET_CHUNK
    log_g = np.log(1.0 - 2.0 ** (-5.0 - np.arange(RET_HEADS, dtype=np.float64)))
    idx = np.arange(C, dtype=np.float64)
    diff = idx[:, None] - idx[None, :]
    decay = np.where(diff >= 0, np.exp(np.maximum(diff, 0.0)[None] * log_g[:, None, None]), 0.0)
    xi = np.exp((idx + 1.0)[None, :] * log_g[:, None])
    zeta = np.exp((C - 1.0 - idx)[None, :] * log_g[:, None])
    g_chunk = tuple(float(v) for v in np.exp(C * log_g))
    xi_b = np.broadcast_to(xi[:, :, None], (RET_HEADS, C, RET_DV)).astype(np.float32)
    zeta_b = np.broadcast_to(zeta[:, :, None], (RET_HEADS, C, RET_DK)).astype(np.float32)

    inv = ROPE_BASE ** (-np.arange(0, RET_DK, 2, dtype=np.float64) / RET_DK)
    ang = np.arange(S, dtype=np.float64)[:, None] * inv[None]
    cos_t = np.repeat(np.cos(ang), 2, axis=1).astype(np.float32)
    sin_t = np.stack([-np.sin(ang), np.sin(ang)], axis=-1).reshape(S, RET_DK).astype(np.float32)

    def col(c):
        return pl.BlockSpec((C, COL_BLOCK), lambda b, n, c=c: (b * nc + n, c))

    const3 = lambda shape: pl.BlockSpec(shape, lambda b, n: (0, 0, 0))
    return pl.pallas_call(
        functools.partial(_ret_kernel, g_chunk=g_chunk),
        out_shape=jax.ShapeDtypeStruct((T, RET_HEADS * RET_DV), BF16),
        grid=(B, nc),
        in_specs=[col(col0), col(col0 + 1), col(col0 + 2), col(col0 + 3), col(col0 + 4), col(col0 + 5),
                  pl.BlockSpec((C, RET_DK), lambda b, n: (n, 0)),
                  pl.BlockSpec((C, RET_DK), lambda b, n: (n, 0)),
                  const3((RET_HEADS, C, C)), const3((RET_HEADS, C, RET_DV)), const3((RET_HEADS, C, RET_DK))],
        out_specs=pl.BlockSpec((C, RET_HEADS * RET_DV), lambda b, n: (b * nc + n, 0)),
        scratch_shapes=[pltpu.VMEM((RET_HEADS, RET_DK, RET_DV), F32)],
        compiler_params=_cparams(("arbitrary", "arbitrary")),
        name="retention",
    )(proj, proj, proj, proj, proj, proj, jnp.asarray(cos_t), jnp.asarray(sin_t),
      jnp.asarray(decay.astype(np.float32)), jnp.asarray(xi_b), jnp.asarray(zeta_b))


ROUTE_IDX, ROUTE_RANK, ROUTE_GATE = 0, 4, 8


ROW_TILE = 8


def _store_row_tiles(dst_ref, val):
    n = val.shape[0]
    for c in range(ROW_TILE):
        dst_ref[pl.ds(c, n, stride=ROW_TILE), :] = val[:, c * LANES:(c + 1) * LANES]


def _load_row_tiles(src_ref, n):
    return jnp.concatenate([src_ref[pl.ds(c, n, stride=ROW_TILE), :] for c in range(ROW_TILE)], axis=-1)


def _to_token_order(src_ref, dst_ref, d):
    n = src_ref.shape[1]
    for r in range(d):
        blk = src_ref[r].astype(F32)
        for s in range(dst_ref.shape[0]):
            dst_ref[s, pl.ds(r, n, stride=d), :] = blk[:, s * LANES:(s + 1) * LANES]


def _merge_kernel(x_ref, o1_ref, o2_ref, o3_ref, l1_ref, l2_ref, l3_ref, ret_ref,
                  ga0_ref, ga1_ref, gr0_ref, gr1_ref, watt_ref, wret_ref, wout_ref, gffn_ref, wr_ref, br_ref,
                  below_ref, x1_ref, hf_ref, route_ref, counts_ref, on2_ref, on3_ref, ln2_ref, ln3_ref):
    tm = x_ref.shape[0]
    _to_token_order(o2_ref, on2_ref, DILATIONS[1])
    _to_token_order(o3_ref, on3_ref, DILATIONS[2])
    _to_token_order(l2_ref, ln2_ref, DILATIONS[1])
    _to_token_order(l3_ref, ln3_ref, DILATIONS[2])
    lses = [l1_ref[...], ln2_ref[0], ln3_ref[0]]

    att_parts = []
    for h in range(HEADS_PER_GROUP):
        ls = [l[:, h:h + 1] for l in lses]
        m = jnp.maximum(jnp.maximum(ls[0], ls[1]), ls[2])
        ws = [jnp.exp(l - m) for l in ls]
        wsum = ws[0] + ws[1] + ws[2]
        hs = slice(h * HEAD_DIM, (h + 1) * HEAD_DIM)
        acc = (ws[0] / wsum) * o1_ref[:, hs].astype(F32)
        acc = acc + (ws[1] / wsum) * on2_ref[h]
        acc = acc + (ws[2] / wsum) * on3_ref[h]
        att_parts.append(acc)
    att = jnp.concatenate(att_parts, axis=-1).astype(BF16)

    a_proj = jnp.dot(att, watt_ref[...], preferred_element_type=F32)
    r_proj = jnp.dot(ret_ref[...], wret_ref[...], preferred_element_type=F32)
    gate_a = jnp.concatenate([ga0_ref[...], ga1_ref[...]], axis=-1).astype(F32)
    gate_r = jnp.concatenate([gr0_ref[...], gr1_ref[...]], axis=-1).astype(F32)
    merged = jax.nn.sigmoid(gate_a) * a_proj + jax.nn.sigmoid(gate_r) * r_proj
    x1 = x_ref[...] + jnp.dot(merged.astype(BF16), wout_ref[...], preferred_element_type=F32)
    x1_ref[...] = x1

    ms = jnp.mean(x1 * x1, axis=-1, keepdims=True)
    hf = x1 * lax.rsqrt(ms + NORM_EPS) * gffn_ref[...]
    _store_row_tiles(hf_ref, hf)

    logits = jnp.dot(hf.astype(BF16), wr_ref[...], preferred_element_type=F32) + br_ref[...]
    lane = lax.broadcasted_iota(jnp.int32, (tm, LANES), 1)
    lane_f = lane.astype(F32)
    work = logits
    vals, idxs, hits = [], [], []
    for _ in range(TOP_K):
        mk = jnp.max(work, axis=-1, keepdims=True)
        ik = jnp.min(jnp.where(work == mk, lane_f, float(LANES)), axis=-1, keepdims=True)
        hit = lane_f == ik
        work = jnp.where(hit, -jnp.inf, work)
        vals.append(mk)
        idxs.append(ik)
        hits.append(hit)
    es = [jnp.exp(v - vals[0]) for v in vals]
    esum = es[0] + es[1] + es[2] + es[3]
    gates = [e / esum for e in es]

    sel = jnp.zeros((tm, LANES), F32)
    for hit in hits:
        sel = jnp.where(hit, 1.0, sel)

    @pl.when(pl.program_id(0) == 0)
    def _():
        counts_ref[...] = jnp.zeros_like(counts_ref)

    rank = jnp.dot(below_ref[...], sel.astype(BF16), preferred_element_type=F32) + counts_ref[...]
    counts_ref[...] = counts_ref[...] + jnp.sum(sel, axis=0, keepdims=True)

    route = jnp.zeros((tm, LANES), F32)
    for k in range(TOP_K):
        rank_k = jnp.sum(jnp.where(hits[k], rank, 0.0), axis=-1, keepdims=True)
        route = jnp.where(lane == ROUTE_IDX + k, idxs[k], route)
        route = jnp.where(lane == ROUTE_RANK + k, rank_k, route)
        route = jnp.where(lane == ROUTE_GATE + k, gates[k], route)
    route_ref[...] = route


def _merge(x2d, B, S, o1, l1, o2, l2, o3, l3, ret, proj, gate_col0, w_att, w_ret, w_out, g_ffn, w_router_p,
           b_router_p, tm=256):
    T, D = x2d.shape
    per_b = S // tm
    d4, d16 = DILATIONS[1], DILATIONS[2]
    row = lambda w: pl.BlockSpec((tm, w), lambda i: (i, 0))
    full = lambda a: pl.BlockSpec(a.shape, lambda i: (0,) * a.ndim)
    gcol = lambda c: pl.BlockSpec((tm, COL_BLOCK), lambda i, c=c: (i, c))
    res = lambda d, w: pl.BlockSpec((None, d, tm // d, w), lambda i: (i // per_b, 0, i % per_b, 0))
    below = jnp.tril(jnp.ones((tm, tm), BF16), k=-1)
    slabs = lambda n: pltpu.VMEM((n, tm, LANES), F32)
    return pl.pallas_call(
        _merge_kernel,
        out_shape=(jax.ShapeDtypeStruct((T, D), F32), jax.ShapeDtypeStruct((T * ROW_TILE, LANES), F32),
                   jax.ShapeDtypeStruct((T, LANES), F32), jax.ShapeDtypeStruct((1, LANES), F32)),
        grid=(T // tm,),
        in_specs=[row(D), row(GROUP_WIDTH), res(d4, GROUP_WIDTH), res(d16, GROUP_WIDTH),
                  row(LANES), res(d4, LANES), res(d16, LANES),
                  row(RET_HEADS * RET_DV), gcol(gate_col0), gcol(gate_col0 + 1), gcol(gate_col0 + 2),
                  gcol(gate_col0 + 3), full(w_att), full(w_ret), full(w_out), full(g_ffn), full(w_router_p),
                  full(b_router_p), full(below)],
        out_specs=(row(D), pl.BlockSpec((tm * ROW_TILE, LANES), lambda i: (i, 0)), row(LANES),
                   pl.BlockSpec((1, LANES), lambda i: (0, 0))),
        scratch_shapes=[slabs(GROUP_WIDTH // LANES), slabs(GROUP_WIDTH // LANES), slabs(1), slabs(1)],
        compiler_params=_cparams(("arbitrary",)),
        name="merge",
    )(x2d, o1, o2, o3, l1, l2, l3, ret, proj, proj, proj, proj, w_att, w_ret, w_out, g_ffn, w_router_p,
      b_router_p, below)


def _row_copy(src, dst, src_row8, dst_row8, sem):
    return pltpu.make_async_copy(src.at[pl.ds(pl.multiple_of(src_row8, ROW_TILE), ROW_TILE)],
                                 dst.at[pl.ds(pl.multiple_of(dst_row8, ROW_TILE), ROW_TILE)], sem)


def _dispatch_kernel(pad_ref, pos_ref, hf_ref, xs_hbm, zero_ref, sem, *, tile):
    @pl.when(pl.program_id(0) == 0)
    def _():
        zero_ref[...] = jnp.zeros_like(zero_ref)
        total = 0
        for e in range(N_EXPERTS):
            first = pad_ref[e]

            def zero_row(j, carry, first=first):
                _row_copy(zero_ref, xs_hbm, 0, first + j * ROW_TILE, sem).start()
                return carry

            lax.fori_loop(0, pad_ref[N_EXPERTS + e], zero_row, 0)
            total = total + pad_ref[N_EXPERTS + e]

        def zero_wait(n, carry):
            _row_copy(zero_ref, xs_hbm, 0, 0, sem).wait()
            return carry

        lax.fori_loop(0, total, zero_wait, 0)

    def issue(i, carry):
        for k in range(TOP_K):
            _row_copy(hf_ref, xs_hbm, i * ROW_TILE, pos_ref[0, k, i], sem).start(priority=k % 2)
        return carry

    lax.fori_loop(0, tile, issue, 0, unroll=8)

    def drain(i, carry):
        for k in range(TOP_K):
            _row_copy(hf_ref, xs_hbm, 0, 0, sem).wait()
        return carry

    lax.fori_loop(0, tile, drain, 0, unroll=8)


def _dispatch(hf, pos8_tiles, pad_info, n_rows, tile):
    T = hf.shape[0] // ROW_TILE
    grid_spec = pltpu.PrefetchScalarGridSpec(
        num_scalar_prefetch=1,
        grid=(T // tile,),
        in_specs=[pl.BlockSpec((1, TOP_K, tile), lambda i, pad: (i, 0, 0), memory_space=pltpu.SMEM),
                  pl.BlockSpec((tile * ROW_TILE, LANES), lambda i, pad: (i, 0))],
        out_specs=pl.BlockSpec(memory_space=pl.ANY),
        scratch_shapes=[pltpu.VMEM((ROW_TILE, LANES), hf.dtype), pltpu.SemaphoreType.DMA(())],
    )
    return pl.pallas_call(
        functools.partial(_dispatch_kernel, tile=tile),
        out_shape=jax.ShapeDtypeStruct((n_rows * ROW_TILE, LANES), hf.dtype),
        grid_spec=grid_spec,
        compiler_params=_cparams(("arbitrary",)),
        name="dispatch",
    )(pad_info, pos8_tiles, hf)


def _expert_kernel(be_ref, nu_ref, x_ref, w1_ref, b1_ref, w2_ref, b2_ref, y_ref, w1b_ref, w2b_ref):
    i = pl.program_id(0)
    F = w2_ref.shape[0]

    @pl.when(i < nu_ref[0])
    def _():
        @pl.when((i == 0) | (be_ref[i] != be_ref[jnp.maximum(i - 1, 0)]))
        def _():
            w1b_ref[...] = w1_ref[...].astype(BF16)
            w2b_ref[...] = w2_ref[...].astype(BF16)

        x = _load_row_tiles(x_ref, EXPERT_BLOCK).astype(BF16)
        gu = jnp.dot(x, w1b_ref[...], preferred_element_type=F32) + b1_ref[...]
        gate = jnp.minimum(gu[:, :F], SWIGLU_LIMIT)
        up = jnp.clip(gu[:, F:], -SWIGLU_LIMIT, SWIGLU_LIMIT)
        act = (up + 1.0) * gate * jax.nn.sigmoid(SWIGLU_ALPHA * gate)
        _store_row_tiles(y_ref, jnp.dot(act.astype(BF16), w2b_ref[...], preferred_element_type=F32) + b2_ref[...])


def _experts(xs, n_blocks, blk_expert, n_used, w1, b1, w2, b2):
    E, D, F2 = w1.shape
    F = w2.shape[1]
    M = EXPERT_BLOCK
    blk = lambda i, be, nu: (jnp.minimum(i, nu[0] - 1), 0)
    grid_spec = pltpu.PrefetchScalarGridSpec(
        num_scalar_prefetch=2,
        grid=(n_blocks,),
        in_specs=[pl.BlockSpec((M * ROW_TILE, LANES), blk),
                  pl.BlockSpec((None, D, F2), lambda i, be, nu: (be[i], 0, 0)),
                  pl.BlockSpec((None, 1, F2), lambda i, be, nu: (be[i], 0, 0)),
                  pl.BlockSpec((None, F, D), lambda i, be, nu: (be[i], 0, 0)),
                  pl.BlockSpec((None, 1, D), lambda i, be, nu: (be[i], 0, 0))],
        out_specs=pl.BlockSpec((M * ROW_TILE, LANES), blk),
        scratch_shapes=[pltpu.VMEM((D, F2), BF16), pltpu.VMEM((F, D), BF16)],
    )
    return pl.pallas_call(
        _expert_kernel,
        out_shape=jax.ShapeDtypeStruct((n_blocks * M * ROW_TILE, LANES), F32),
        grid_spec=grid_spec,
        compiler_params=_cparams(("arbitrary",), VMEM_LIMIT_BIG),
        name="experts",
    )(blk_expert, n_used, xs, w1, b1.reshape(E, 1, F2), w2, b2.reshape(E, 1, D))


def _combine_kernel(pos_ref, route_ref, x1_ref, g_ref, ys_hbm, o_ref, buf_ref, sem, *, tile):
    def issue(i, carry):
        for k in range(TOP_K):
            _row_copy(ys_hbm, buf_ref.at[k], pos_ref[0, k, i], i * ROW_TILE, sem).start(priority=k % 2)
        return carry

    lax.fori_loop(0, tile, issue, 0, unroll=8)

    def drain(i, carry):
        for k in range(TOP_K):
            _row_copy(ys_hbm, buf_ref.at[k], 0, 0, sem).wait()
        return carry

    lax.fori_loop(0, tile, drain, 0, unroll=8)

    route = route_ref[...]
    acc = x1_ref[...]
    for k in range(TOP_K):
        acc = acc + route[:, ROUTE_GATE + k:ROUTE_GATE + k + 1] * _load_row_tiles(buf_ref.at[k], tile)
    ms = jnp.mean(acc * acc, axis=-1, keepdims=True)
    o_ref[...] = acc * lax.rsqrt(ms + NORM_EPS) * g_ref[...]


def _combine(pos_tiles, route, x1, g_final, ys, tile):
    T, D = x1.shape
    return pl.pallas_call(
        functools.partial(_combine_kernel, tile=tile),
        out_shape=jax.ShapeDtypeStruct((T, D), F32),
        grid=(T // tile,),
        in_specs=[pl.BlockSpec((1, TOP_K, tile), lambda i: (i, 0, 0), memory_space=pltpu.SMEM),
                  pl.BlockSpec((tile, LANES), lambda i: (i, 0)),
                  pl.BlockSpec((tile, D), lambda i: (i, 0)),
                  pl.BlockSpec((1, D), lambda i: (0, 0)),
                  pl.BlockSpec(memory_space=pl.ANY)],
        out_specs=pl.BlockSpec((tile, D), lambda i: (i, 0)),
        scratch_shapes=[pltpu.VMEM((TOP_K, tile * ROW_TILE, LANES), F32), pltpu.SemaphoreType.DMA(())],
        compiler_params=_cparams(("arbitrary",)),
        name="combine",
    )(pos_tiles, route, x1, g_final.reshape(1, D), ys)


def _layer(x2d, B, S, norm_mix_g, w_in, rel_bias, w_att, w_ret, w_out, norm_ffn_g, w_router, b_router,
           w1, b1, w2, b2):
    T, D = x2d.shape
    proj, a4, a16 = _inproj(x2d, norm_mix_g, w_in.astype(BF16), B, S)

    group_bias = [_band_bias(rel_bias[:, g * HEADS_PER_GROUP:(g + 1) * HEADS_PER_GROUP], d)
                  for g, d in enumerate(DILATIONS)]
    o1, l1 = _attention_group(proj.reshape(B, 1, S, proj.shape[1]), group_bias[0],
                              blocks=min(S // N_STEPS, 8))
    o2, l2 = _attention_group(a4, group_bias[1], blocks=min(a4.shape[2] // N_STEPS, 8))
    o3, l3 = _attention_group(a16, group_bias[2], blocks=min(a16.shape[2] // N_STEPS, 8))

    ret = _retention(proj, B, S, MAIN_RET_COL)

    w_router_p = jnp.zeros((D, LANES), BF16).at[:, :N_EXPERTS].set(w_router.astype(BF16))
    b_router_p = jnp.full((1, LANES), NEG_BIG, F32).at[0, :N_EXPERTS].set(b_router.astype(F32))
    x1, hf, route, counts = _merge(x2d, B, S, o1.reshape(T, GROUP_WIDTH), l1.reshape(T, LANES), o2, l2, o3, l3,
                                   ret, proj, MAIN_GATE_COL, w_att.astype(BF16), w_ret.astype(BF16),
                                   w_out.astype(BF16), norm_ffn_g.reshape(1, D), w_router_p, b_router_p)

    M = EXPERT_BLOCK
    n_blocks = (T * TOP_K) // M + N_EXPERTS
    cnt = counts[0, :N_EXPERTS].astype(jnp.int32)
    padded = ((cnt + M - 1) // M) * M
    pend = jnp.cumsum(padded)
    pstart = pend - padded
    idx = route[:, ROUTE_IDX:ROUTE_IDX + TOP_K].astype(jnp.int32)
    rank = route[:, ROUTE_RANK:ROUTE_RANK + TOP_K].astype(jnp.int32)
    experts = jnp.arange(N_EXPERTS, dtype=jnp.int32)
    pos = rank + jnp.sum(jnp.where(idx[..., None] == experts, pstart, 0), axis=-1)
    n_used = (pend[-1] // M).astype(jnp.int32)
    blk_row = jnp.minimum(jnp.arange(n_blocks, dtype=jnp.int32), jnp.maximum(n_used - 1, 0)) * M
    blk_expert = jnp.minimum(jnp.sum((pend[None, :] <= blk_row[:, None]).astype(jnp.int32), axis=-1),
                             N_EXPERTS - 1)

    assert D == ROW_TILE * LANES
    tile = 256
    pos_tiles = (pos * ROW_TILE).reshape(T // tile, tile, TOP_K).transpose(0, 2, 1)
    pad_info = jnp.concatenate([(pstart + cnt) * ROW_TILE, padded - cnt]).astype(jnp.int32)
    xs = _dispatch(hf, pos_tiles, pad_info, n_blocks * M, tile)
    ys = _experts(xs, n_blocks, blk_expert, n_used.reshape(1), w1, b1, w2, b2)
    return pos_tiles, route, x1, ys


def kernel(x, norm_mix_g, w_in, rel_bias, w_att_branch, w_ret_branch, w_out, norm_ffn_g, w_router, b_router,
           w1, b1, w2, b2, norm_final_g):
    B, S, D = x.shape
    depth = w_in.shape[0]
    assert depth == 1, "the combine stage applies the final norm; a deeper stack needs a separate norm pass"
    x2d = x.reshape(B * S, D)
    pos_tiles, route, x1, ys = _layer(x2d, B, S, norm_mix_g[0], w_in[0], rel_bias, w_att_branch[0],
                                      w_ret_branch[0], w_out[0], norm_ffn_g[0], w_router[0], b_router[0],
                                      w1[0], b1[0], w2[0], b2[0])
    out = _combine(pos_tiles, route, x1, norm_final_g, ys, tile=256)
    return out.reshape(B, S, D)
```

```python
import functools
import math

import numpy as np
import jax
import jax.numpy as jnp
from jax import lax
from jax.experimental import pallas as pl
from jax.experimental.pallas import tpu as pltpu

F32 = jnp.float32
BF16 = jnp.bfloat16

NORM_EPS = 1e-5

HEAD_DIM = 128
HEADS_PER_GROUP = 4
GROUP_WIDTH = HEADS_PER_GROUP * HEAD_DIM
DILATIONS = (1, 4, 16)
N_STEPS = 128
N_GROUPS = len(DILATIONS)
REL_BUCKETS = 32
REL_MAX_DIST = 2048

RET_HEADS = 4
RET_DK = 128
RET_DV = 256
RET_CHUNK = 128
ROPE_BASE = 10000.0

N_EXPERTS = 32
TOP_K = 4
SWIGLU_LIMIT = 7.0
SWIGLU_ALPHA = 1.702
EXPERT_BLOCK = 512

LANES = 128
COL_BLOCK = 512
NEG_BIG = -1e30

VMEM_LIMIT = 48 * 1024 * 1024
VMEM_LIMIT_BIG = 56 * 1024 * 1024

N_ATT_CHUNKS = 3 * N_GROUPS
MAIN_RET_COL = 3
MAIN_GATE_COL = MAIN_RET_COL + (2 * RET_HEADS * RET_DK + 2 * RET_HEADS * RET_DV) // COL_BLOCK


def _cparams(sem, vmem=VMEM_LIMIT):
    return pltpu.CompilerParams(dimension_semantics=sem, vmem_limit_bytes=vmem)


def _inproj_kernel(x_ref, g_ref, w_ref, main_ref, a4_ref, a16_ref, h_ref, slab_ref):
    tm = x_ref.shape[0]
    x = x_ref[...]
    ms = jnp.mean(x * x, axis=-1, keepdims=True)
    h_ref[...] = (x * lax.rsqrt(ms + NORM_EPS) * g_ref[...]).astype(BF16)
    n_chunks = w_ref.shape[1] // COL_BLOCK
    n_slabs = COL_BLOCK // LANES
    dilated = {1: (DILATIONS[1], a4_ref), 2: (DILATIONS[2], a16_ref)}
    for c in range(n_chunks):
        res = jnp.dot(h_ref[...], w_ref[:, c * COL_BLOCK:(c + 1) * COL_BLOCK], preferred_element_type=F32)
        if c >= N_ATT_CHUNKS:
            mc = c - N_ATT_CHUNKS + MAIN_RET_COL
            main_ref[:, mc * COL_BLOCK:(mc + 1) * COL_BLOCK] = res.astype(BF16)
            continue
        part, group = divmod(c, N_GROUPS)
        if group == 0:
            main_ref[:, part * COL_BLOCK:(part + 1) * COL_BLOCK] = res.astype(BF16)
            continue
        d, dest = dilated[group]
        for s in range(n_slabs):
            slab_ref[s] = res[:, s * LANES:(s + 1) * LANES]
        for r in range(d):
            for s in range(n_slabs):
                lo = part * COL_BLOCK + s * LANES
                dest[r, :, lo:lo + LANES] = slab_ref[s, pl.ds(r, tm // d, stride=d), :].astype(BF16)


def _inproj(x2d, g, w_bf16, B, S, tm=512):
    T, D = x2d.shape
    n_main = (w_bf16.shape[1] // COL_BLOCK - N_ATT_CHUNKS + 3) * COL_BLOCK
    d4, d16 = DILATIONS[1], DILATIONS[2]
    per_b = S // tm
    return pl.pallas_call(
        _inproj_kernel,
        out_shape=(jax.ShapeDtypeStruct((T, n_main), BF16),
                   jax.ShapeDtypeStruct((B, d4, S // d4, 3 * GROUP_WIDTH), BF16),
                   jax.ShapeDtypeStruct((B, d16, S // d16, 3 * GROUP_WIDTH), BF16)),
        grid=(T // tm,),
        in_specs=[
            pl.BlockSpec((tm, D), lambda i: (i, 0)),
            pl.BlockSpec((1, D), lambda i: (0, 0)),
            pl.BlockSpec(w_bf16.shape, lambda i: (0, 0), pipeline_mode=pl.Buffered(1)),
        ],
        out_specs=(pl.BlockSpec((tm, n_main), lambda i: (i, 0)),
                   pl.BlockSpec((None, d4, tm // d4, 3 * GROUP_WIDTH), lambda i: (i // per_b, 0, i % per_b, 0)),
                   pl.BlockSpec((None, d16, tm // d16, 3 * GROUP_WIDTH), lambda i: (i // per_b, 0, i % per_b, 0))),
        scratch_shapes=[pltpu.VMEM((tm, D), BF16), pltpu.VMEM((COL_BLOCK // LANES, tm, LANES), F32)],
        compiler_params=_cparams(("arbitrary",), VMEM_LIMIT_BIG),
        name="inproj",
    )(x2d, g.reshape(1, D), w_bf16)


def _attn_block(q, kp, kc, vp, vc, bias_ref, has_prev):
    scale = HEAD_DIM ** -0.5
    heads = range(HEADS_PER_GROUP)
    hs = [slice(h * HEAD_DIM, (h + 1) * HEAD_DIM) for h in heads]
    rs = [slice(h * N_STEPS, (h + 1) * N_STEPS) for h in heads]
    k = jnp.concatenate([kp, kc], axis=0)
    v = jnp.concatenate([vp, vc], axis=0)
    dn = (((1,), (1,)), ((), ()))
    s = jnp.concatenate([lax.dot_general(q[:, hs[h]], k[:, hs[h]], dn, preferred_element_type=F32)
                         for h in heads], axis=0)
    s = s * scale + bias_ref[...]
    if has_prev is not True:
        col = lax.broadcasted_iota(jnp.int32, s.shape, 1)
        s = jnp.where((col >= N_STEPS) | has_prev, s, NEG_BIG)
    m = jnp.max(s, axis=-1, keepdims=True)
    p = jnp.exp(s - m)
    l = jnp.sum(p, axis=-1, keepdims=True)
    pb = p.astype(BF16)
    lse_rows = m + jnp.log(l)
    lane = lax.broadcasted_iota(jnp.int32, (N_STEPS, LANES), 1)
    outs = []
    lse = jnp.zeros((N_STEPS, LANES), F32)
    for h in heads:
        acc = jnp.dot(pb[rs[h], :], v[:, hs[h]], preferred_element_type=F32)
        outs.append(acc / l[rs[h], :])
        lse = jnp.where(lane == h, lse_rows[rs[h], :], lse)
    return jnp.concatenate(outs, axis=-1), lse


def _attn_kernel(q_ref, k_ref, v_ref, kprev_ref, vprev_ref, bias_ref, o_ref, lse_ref, *, blocks):
    n = pl.program_id(2)

    def run(j, kp, vp, has_prev):
        rows = pl.ds(pl.multiple_of(j * N_STEPS, N_STEPS), N_STEPS)
        o, lse = _attn_block(q_ref[rows, :], kp, k_ref[rows, :], vp, v_ref[rows, :], bias_ref, has_prev)
        o_ref[rows, :] = o.astype(o_ref.dtype)
        lse_ref[rows, :] = lse

    run(0, kprev_ref[...], vprev_ref[...], n > 0)

    def body(j, carry):
        prev = pl.ds(pl.multiple_of((j - 1) * N_STEPS, N_STEPS), N_STEPS)
        run(j, k_ref[prev, :], v_ref[prev, :], True)
        return carry

    if blocks > 1:
        lax.fori_loop(1, blocks, body, 0, unroll=True)


def _attention_group(qkv, bias, blocks):
    B, d, L, _ = qkv.shape
    W = GROUP_WIDTH
    rows = blocks * N_STEPS
    steps = L // rows

    def cur(c):
        return pl.BlockSpec((None, None, rows, W), lambda b, r, n: (b, r, n, c))

    def prev(c):
        return pl.BlockSpec((None, None, N_STEPS, W), lambda b, r, n: (b, r, jnp.maximum(n * blocks - 1, 0), c))

    return pl.pallas_call(
        functools.partial(_attn_kernel, blocks=blocks),
        out_shape=(jax.ShapeDtypeStruct((B, d, L, W), BF16), jax.ShapeDtypeStruct((B, d, L, LANES), F32)),
        grid=(B, d, steps),
        in_specs=[cur(0), cur(1), cur(2), prev(1), prev(2),
                  pl.BlockSpec((HEADS_PER_GROUP * N_STEPS, 2 * N_STEPS), lambda b, r, n: (0, 0))],
        out_specs=(pl.BlockSpec((None, None, rows, W), lambda b, r, n: (b, r, n, 0)),
                   pl.BlockSpec((None, None, rows, LANES), lambda b, r, n: (b, r, n, 0))),
        compiler_params=_cparams(("arbitrary", "arbitrary", "arbitrary")),
        name=f"attn_d{d}",
    )(qkv, qkv, qkv, qkv, qkv, bias)


def _t5_bucket(dist):
    max_exact = REL_BUCKETS // 2
    d_f = jnp.maximum(dist, 1).astype(F32)
    large = max_exact + (jnp.log(d_f / max_exact) / math.log(REL_MAX_DIST / max_exact)
                         * (REL_BUCKETS - max_exact)).astype(jnp.int32)
    large = jnp.minimum(large, REL_BUCKETS - 1)
    return jnp.where(dist < max_exact, dist, large)


def _band_bias(rel_bias_g, dilation):
    qi = jnp.arange(N_STEPS)[:, None]
    kj = jnp.arange(2 * N_STEPS)[None, :]
    step_dist = qi + N_STEPS - kj
    band = (step_dist >= 0) & (step_dist <= N_STEPS)
    bucket = _t5_bucket(jnp.maximum(step_dist, 0) * dilation)
    table = rel_bias_g.astype(F32).T
    hit = bucket[None, None] == jnp.arange(REL_BUCKETS)[None, :, None, None]
    bias = jnp.sum(jnp.where(hit, table[:, :, None, None], 0.0), axis=1)
    return jnp.where(band[None], bias, NEG_BIG).reshape(HEADS_PER_GROUP * N_STEPS, 2 * N_STEPS)


def _ret_kernel(q_ref, k_ref, v0_ref, v1_ref, g0_ref, g1_ref, cos_ref, sin_ref, decay_ref, xi_ref, zeta_ref,
                o_ref, state_ref, *, g_chunk):
    n = pl.program_id(1)

    @pl.when(n == 0)
    def _():
        state_ref[...] = jnp.zeros_like(state_ref)

    C = RET_CHUNK
    n_sub = q_ref.shape[0] // C
    lane = lax.broadcasted_iota(jnp.int32, (C, RET_DK), 1)
    even = (lane % 2) == 0

    def rotary(t, rows):
        partner = jnp.where(even, pltpu.roll(t, RET_DK - 1, 1), pltpu.roll(t, 1, 1))
        return t * cos_ref[rows, :] + partner * sin_ref[rows, :]

    for h in range(RET_HEADS):
        ks = slice(h * RET_DK, (h + 1) * RET_DK)
        v_ref = v0_ref if h < 2 else v1_ref
        g_ref = g0_ref if h < 2 else g1_ref
        vs = slice((h % 2) * RET_DV, (h % 2 + 1) * RET_DV)
        state = state_ref[h]
        for sub in range(n_sub):
            rows = slice(sub * C, (sub + 1) * C)
            qr = rotary(q_ref[rows, ks].astype(F32), rows)
            kr = rotary(k_ref[rows, ks].astype(F32), rows) * (RET_DK ** -0.5)
            v = v_ref[rows, vs]
            qb = qr.astype(BF16)
            kb = kr.astype(BF16)
            scores = lax.dot_general(qb, kb, (((1,), (1,)), ((), ())), preferred_element_type=F32) * decay_ref[h]
            inner = jnp.dot(scores.astype(BF16), v, preferred_element_type=F32)
            cross = jnp.dot(qb, state.astype(BF16), preferred_element_type=F32) * xi_ref[h]
            kz = (kr * zeta_ref[h]).astype(BF16)
            state = state * g_chunk[h] + lax.dot_general(kz, v, (((0,), (0,)), ((), ())),
                                                         preferred_element_type=F32)
            ret = inner + cross
            mu = jnp.mean(ret, axis=-1, keepdims=True)
            cen = ret - mu
            var = jnp.mean(cen * cen, axis=-1, keepdims=True)
            normed = cen * lax.rsqrt(var + NORM_EPS)
            g = g_ref[rows, vs].astype(F32)
            o_ref[rows, h * RET_DV:(h + 1) * RET_DV] = (g * jax.nn.sigmoid(g) * normed).astype(o_ref.dtype)
        state_ref[h] = state


def _retention(proj, B, S, col0):
    T = B * S
    nc = S // RET_CHUNK
    C = RET_CHUNK
    log_g = np.log(1.0 - 2.0 ** (-5.0 - np.arange(RET_HEADS, dtype=np.float64)))
    idx = np.arange(C, dtype=np.float64)
    diff = idx[:, None] - idx[None, :]
    decay = np.where(diff >= 0, np.exp(np.maximum(diff, 0.0)[None] * log_g[:, None, None]), 0.0)
    xi = np.exp((idx + 1.0)[None, :] * log_g[:, None])
    zeta = np.exp((C - 1.0 - idx)[None, :] * log_g[:, None])
    g_chunk = tuple(float(v) for v in np.exp(C * log_g))
    xi_b = np.broadcast_to(xi[:, :, None], (RET_HEADS, C, RET_DV)).astype(np.float32)
    zeta_b = np.broadcast_to(zeta[:, :, None], (RET_HEADS, C, RET_DK)).astype(np.float32)

    inv = ROPE_BASE ** (-np.arange(0, RET_DK, 2, dtype=np.float64) / RET_DK)
    ang = np.arange(S, dtype=np.float64)[:, None] * inv[None]
    cos_t = np.repeat(np.cos(ang), 2, axis=1).astype(np.float32)
    sin_t = np.stack([-np.sin(ang), np.sin(ang)], axis=-1).reshape(S, RET_DK).astype(np.float32)

    rows = 2 * C if nc % 2 == 0 else C
    ns = S // rows

    def col(c):
        return pl.BlockSpec((rows, COL_BLOCK), lambda b, n, c=c: (b * ns + n, c))

    const3 = lambda shape: pl.BlockSpec(shape, lambda b, n: (0, 0, 0))
    return pl.pallas_call(
        functools.partial(_ret_kernel, g_chunk=g_chunk),
        out_shape=jax.ShapeDtypeStruct((T, RET_HEADS * RET_DV), BF16),
        grid=(B, ns),
        in_specs=[col(col0), col(col0 + 1), col(col0 + 2), col(col0 + 3), col(col0 + 4), col(col0 + 5),
                  pl.BlockSpec((rows, RET_DK), lambda b, n: (n, 0)),
                  pl.BlockSpec((rows, RET_DK), lambda b, n: (n, 0)),
                  const3((RET_HEADS, C, C)), const3((RET_HEADS, C, RET_DV)), const3((RET_HEADS, C, RET_DK))],
        out_specs=pl.BlockSpec((rows, RET_HEADS * RET_DV), lambda b, n: (b * ns + n, 0)),
        scratch_shapes=[pltpu.VMEM((RET_HEADS, RET_DK, RET_DV), F32)],
        compiler_params=_cparams(("arbitrary", "arbitrary")),
        name="retention",
    )(proj, proj, proj, proj, proj, proj, jnp.asarray(cos_t), jnp.asarray(sin_t),
      jnp.asarray(decay.astype(np.float32)), jnp.asarray(xi_b), jnp.asarray(zeta_b))


ROUTE_IDX, ROUTE_RANK, ROUTE_GATE = 0, 4, 8


ROW_TILE = 8


def _store_row_tiles(dst_ref, val):
    n = val.shape[0]
    for c in range(ROW_TILE):
        dst_ref[pl.ds(c, n, stride=ROW_TILE), :] = val[:, c * LANES:(c + 1) * LANES]


def _load_row_tiles(src_ref, n):
    return jnp.concatenate([src_ref[pl.ds(c, n, stride=ROW_TILE), :] for c in range(ROW_TILE)], axis=-1)


def _to_token_order(src_ref, dst_ref, d):
    n = src_ref.shape[1]
    for r in range(d):
        blk = src_ref[r].astype(F32)
        for s in range(dst_ref.shape[0]):
            dst_ref[s, pl.ds(r, n, stride=d), :] = blk[:, s * LANES:(s + 1) * LANES]


def _merge_kernel(x_ref, o1_ref, o2_ref, o3_ref, l1_ref, l2_ref, l3_ref, ret_ref,
                  ga0_ref, ga1_ref, gr0_ref, gr1_ref, watt_ref, wret_ref, wout_ref, gffn_ref, wr_ref, br_ref,
                  below_ref, x1_ref, hf_ref, route_ref, counts_ref, on2_ref, on3_ref, ln2_ref, ln3_ref):
    tm = x_ref.shape[0]
    _to_token_order(o2_ref, on2_ref, DILATIONS[1])
    _to_token_order(o3_ref, on3_ref, DILATIONS[2])
    _to_token_order(l2_ref, ln2_ref, DILATIONS[1])
    _to_token_order(l3_ref, ln3_ref, DILATIONS[2])
    lses = [l1_ref[...], ln2_ref[0], ln3_ref[0]]

    att_parts = []
    for h in range(HEADS_PER_GROUP):
        ls = [l[:, h:h + 1] for l in lses]
        m = jnp.maximum(jnp.maximum(ls[0], ls[1]), ls[2])
        ws = [jnp.exp(l - m) for l in ls]
        wsum = ws[0] + ws[1] + ws[2]
        hs = slice(h * HEAD_DIM, (h + 1) * HEAD_DIM)
        acc = (ws[0] / wsum) * o1_ref[:, hs].astype(F32)
        acc = acc + (ws[1] / wsum) * on2_ref[h]
        acc = acc + (ws[2] / wsum) * on3_ref[h]
        att_parts.append(acc)
    att = jnp.concatenate(att_parts, axis=-1).astype(BF16)

    a_proj = jnp.dot(att, watt_ref[...], preferred_element_type=F32)
    r_proj = jnp.dot(ret_ref[...], wret_ref[...], preferred_element_type=F32)
    gate_a = jnp.concatenate([ga0_ref[...], ga1_ref[...]], axis=-1).astype(F32)
    gate_r = jnp.concatenate([gr0_ref[...], gr1_ref[...]], axis=-1).astype(F32)
    merged = jax.nn.sigmoid(gate_a) * a_proj + jax.nn.sigmoid(gate_r) * r_proj
    x1 = x_ref[...] + jnp.dot(merged.astype(BF16), wout_ref[...], preferred_element_type=F32)
    x1_ref[...] = x1

    ms = jnp.mean(x1 * x1, axis=-1, keepdims=True)
    hf = x1 * lax.rsqrt(ms + NORM_EPS) * gffn_ref[...]
    _store_row_tiles(hf_ref, hf)

    logits = jnp.dot(hf.astype(BF16), wr_ref[...], preferred_element_type=F32) + br_ref[...]
    lane = lax.broadcasted_iota(jnp.int32, (tm, LANES), 1)
    lane_f = lane.astype(F32)
    work = logits
    vals, idxs, hits = [], [], []
    for _ in range(TOP_K):
        mk = jnp.max(work, axis=-1, keepdims=True)
        ik = jnp.min(jnp.where(work == mk, lane_f, float(LANES)), axis=-1, keepdims=True)
        hit = lane_f == ik
        work = jnp.where(hit, -jnp.inf, work)
        vals.append(mk)
        idxs.append(ik)
        hits.append(hit)
    es = [jnp.exp(v - vals[0]) for v in vals]
    esum = es[0] + es[1] + es[2] + es[3]
    gates = [e / esum for e in es]

    sel = jnp.zeros((tm, LANES), F32)
    for hit in hits:
        sel = jnp.where(hit, 1.0, sel)

    @pl.when(pl.program_id(0) == 0)
    def _():
        counts_ref[...] = jnp.zeros_like(counts_ref)

    rank = jnp.dot(below_ref[...], sel.astype(BF16), preferred_element_type=F32) + counts_ref[...]
    counts_ref[...] = counts_ref[...] + jnp.sum(sel, axis=0, keepdims=True)

    route = jnp.zeros((tm, LANES), F32)
    for k in range(TOP_K):
        rank_k = jnp.sum(jnp.where(hits[k], rank, 0.0), axis=-1, keepdims=True)
        route = jnp.where(lane == ROUTE_IDX + k, idxs[k], route)
        route = jnp.where(lane == ROUTE_RANK + k, rank_k, route)
        route = jnp.where(lane == ROUTE_GATE + k, gates[k], route)
    route_ref[...] = route


def _merge(x2d, B, S, o1, l1, o2, l2, o3, l3, ret, proj, gate_col0, w_att, w_ret, w_out, g_ffn, w_router_p,
           b_router_p, tm=256):
    T, D = x2d.shape
    per_b = S // tm
    d4, d16 = DILATIONS[1], DILATIONS[2]
    row = lambda w: pl.BlockSpec((tm, w), lambda i: (i, 0))
    full = lambda a: pl.BlockSpec(a.shape, lambda i: (0,) * a.ndim)
    gcol = lambda c: pl.BlockSpec((tm, COL_BLOCK), lambda i, c=c: (i, c))
    res = lambda d, w: pl.BlockSpec((None, d, tm // d, w), lambda i: (i // per_b, 0, i % per_b, 0))
    below = jnp.tril(jnp.ones((tm, tm), BF16), k=-1)
    slabs = lambda n: pltpu.VMEM((n, tm, LANES), F32)
    return pl.pallas_call(
        _merge_kernel,
        out_shape=(jax.ShapeDtypeStruct((T, D), F32), jax.ShapeDtypeStruct((T * ROW_TILE, LANES), F32),
                   jax.ShapeDtypeStruct((T, LANES), F32), jax.ShapeDtypeStruct((1, LANES), F32)),
        grid=(T // tm,),
        in_specs=[row(D), row(GROUP_WIDTH), res(d4, GROUP_WIDTH), res(d16, GROUP_WIDTH),
                  row(LANES), res(d4, LANES), res(d16, LANES),
                  row(RET_HEADS * RET_DV), gcol(gate_col0), gcol(gate_col0 + 1), gcol(gate_col0 + 2),
                  gcol(gate_col0 + 3), full(w_att), full(w_ret), full(w_out), full(g_ffn), full(w_router_p),
                  full(b_router_p), full(below)],
        out_specs=(row(D), pl.BlockSpec((tm * ROW_TILE, LANES), lambda i: (i, 0)), row(LANES),
                   pl.BlockSpec((1, LANES), lambda i: (0, 0))),
        scratch_shapes=[slabs(GROUP_WIDTH // LANES), slabs(GROUP_WIDTH // LANES), slabs(1), slabs(1)],
        compiler_params=_cparams(("arbitrary",)),
        name="merge",
    )(x2d, o1, o2, o3, l1, l2, l3, ret, proj, proj, proj, proj, w_att, w_ret, w_out, g_ffn, w_router_p,
      b_router_p, below)


def _row_copy(src, dst, src_row8, dst_row8, sem):
    return pltpu.make_async_copy(src.at[pl.ds(pl.multiple_of(src_row8, ROW_TILE), ROW_TILE)],
                                 dst.at[pl.ds(pl.multiple_of(dst_row8, ROW_TILE), ROW_TILE)], sem)


def _dispatch_kernel(pad_ref, pos_ref, hf_ref, xs_hbm, zero_ref, sem, *, tile):
    @pl.when(pl.program_id(0) == 0)
    def _():
        zero_ref[...] = jnp.zeros_like(zero_ref)
        total = 0
        for e in range(N_EXPERTS):
            first = pad_ref[e]

            def zero_row(j, carry, first=first):
                _row_copy(zero_ref, xs_hbm, 0, first + j * ROW_TILE, sem).start()
                return carry

            lax.fori_loop(0, pad_ref[N_EXPERTS + e], zero_row, 0)
            total = total + pad_ref[N_EXPERTS + e]

        def zero_wait(n, carry):
            _row_copy(zero_ref, xs_hbm, 0, 0, sem).wait()
            return carry

        lax.fori_loop(0, total, zero_wait, 0)

    def issue(i, carry):
        for k in range(TOP_K):
            _row_copy(hf_ref, xs_hbm, i * ROW_TILE, pos_ref[0, k, i], sem).start(priority=k % 2)
        return carry

    lax.fori_loop(0, tile, issue, 0, unroll=8)

    def drain(i, carry):
        for k in range(TOP_K):
            _row_copy(hf_ref, xs_hbm, 0, 0, sem).wait()
        return carry

    lax.fori_loop(0, tile, drain, 0, unroll=8)


def _dispatch(hf, pos8_tiles, pad_info, n_rows, tile):
    T = hf.shape[0] // ROW_TILE
    grid_spec = pltpu.PrefetchScalarGridSpec(
        num_scalar_prefetch=1,
        grid=(T // tile,),
        in_specs=[pl.BlockSpec((1, TOP_K, tile), lambda i, pad: (i, 0, 0), memory_space=pltpu.SMEM),
                  pl.BlockSpec((tile * ROW_TILE, LANES), lambda i, pad: (i, 0))],
        out_specs=pl.BlockSpec(memory_space=pl.ANY),
        scratch_shapes=[pltpu.VMEM((ROW_TILE, LANES), hf.dtype), pltpu.SemaphoreType.DMA(())],
    )
    return pl.pallas_call(
        functools.partial(_dispatch_kernel, tile=tile),
        out_shape=jax.ShapeDtypeStruct((n_rows * ROW_TILE, LANES), hf.dtype),
        grid_spec=grid_spec,
        compiler_params=_cparams(("arbitrary",)),
        name="dispatch",
    )(pad_info, pos8_tiles, hf)


def _expert_kernel(be_ref, nu_ref, x_ref, w1_ref, b1_ref, w2_ref, b2_ref, y_ref, w1b_ref, w2b_ref):
    i = pl.program_id(0)
    F = w2_ref.shape[0]

    @pl.when(i < nu_ref[0])
    def _():
        @pl.when((i == 0) | (be_ref[i] != be_ref[jnp.maximum(i - 1, 0)]))
        def _():
            w1b_ref[...] = w1_ref[...].astype(BF16)
            w2b_ref[...] = w2_ref[...].astype(BF16)

        x = _load_row_tiles(x_ref, EXPERT_BLOCK).astype(BF16)
        gu = jnp.dot(x, w1b_ref[...], preferred_element_type=F32) + b1_ref[...]
        gate = jnp.minimum(gu[:, :F], SWIGLU_LIMIT)
        up = jnp.clip(gu[:, F:], -SWIGLU_LIMIT, SWIGLU_LIMIT)
        act = (up + 1.0) * gate * jax.nn.sigmoid(SWIGLU_ALPHA * gate)
        _store_row_tiles(y_ref, jnp.dot(act.astype(BF16), w2b_ref[...], preferred_element_type=F32) + b2_ref[...])


def _experts(xs, n_blocks, blk_expert, n_used, w1, b1, w2, b2):
    E, D, F2 = w1.shape
    F = w2.shape[1]
    M = EXPERT_BLOCK
    blk = lambda i, be, nu: (jnp.minimum(i, nu[0] - 1), 0)
    grid_spec = pltpu.PrefetchScalarGridSpec(
        num_scalar_prefetch=2,
        grid=(n_blocks,),
        in_specs=[pl.BlockSpec((M * ROW_TILE, LANES), blk),
                  pl.BlockSpec((None, D, F2), lambda i, be, nu: (be[i], 0, 0)),
                  pl.BlockSpec((None, 1, F2), lambda i, be, nu: (be[i], 0, 0)),
                  pl.BlockSpec((None, F, D), lambda i, be, nu: (be[i], 0, 0)),
                  pl.BlockSpec((None, 1, D), lambda i, be, nu: (be[i], 0, 0))],
        out_specs=pl.BlockSpec((M * ROW_TILE, LANES), blk),
        scratch_shapes=[pltpu.VMEM((D, F2), BF16), pltpu.VMEM((F, D), BF16)],
    )
    return pl.pallas_call(
        _expert_kernel,
        out_shape=jax.ShapeDtypeStruct((n_blocks * M * ROW_TILE, LANES), F32),
        grid_spec=grid_spec,
        compiler_params=_cparams(("arbitrary",), VMEM_LIMIT_BIG),
        name="experts",
    )(blk_expert, n_used, xs, w1, b1.reshape(E, 1, F2), w2, b2.reshape(E, 1, D))


def _combine_kernel(pos_ref, pos_next_ref, route_ref, x1_ref, g_ref, ys_hbm, o_ref, buf_ref, sem, *, tile):
    step = pl.program_id(0)
    cur = lax.rem(step, 2)

    def gather(p_ref, slot):
        def issue(i, carry):
            for k in range(TOP_K):
                _row_copy(ys_hbm, buf_ref.at[slot, k], p_ref[0, k, i], i * ROW_TILE,
                          sem.at[slot]).start(priority=k % 2)
            return carry

        lax.fori_loop(0, tile, issue, 0, unroll=8)

    @pl.when(step == 0)
    def _():
        gather(pos_ref, 0)

    @pl.when(step + 1 < pl.num_programs(0))
    def _():
        gather(pos_next_ref, 1 - cur)

    def drain(i, carry):
        for k in range(TOP_K):
            _row_copy(ys_hbm, buf_ref.at[cur, k], 0, 0, sem.at[cur]).wait()
        return carry

    lax.fori_loop(0, tile, drain, 0, unroll=8)

    route = route_ref[...]
    acc = x1_ref[...]
    for k in range(TOP_K):
        acc = acc + route[:, ROUTE_GATE + k:ROUTE_GATE + k + 1] * _load_row_tiles(buf_ref.at[cur, k], tile)
    ms = jnp.mean(acc * acc, axis=-1, keepdims=True)
    o_ref[...] = acc * lax.rsqrt(ms + NORM_EPS) * g_ref[...]


def _combine(pos_tiles, route, x1, g_final, ys, tile):
    T, D = x1.shape
    nt = T // tile
    return pl.pallas_call(
        functools.partial(_combine_kernel, tile=tile),
        out_shape=jax.ShapeDtypeStruct((T, D), F32),
        grid=(nt,),
        in_specs=[pl.BlockSpec((1, TOP_K, tile), lambda i: (i, 0, 0), memory_space=pltpu.SMEM),
                  pl.BlockSpec((1, TOP_K, tile), lambda i: (jnp.minimum(i + 1, nt - 1), 0, 0),
                               memory_space=pltpu.SMEM),
                  pl.BlockSpec((tile, LANES), lambda i: (i, 0)),
                  pl.BlockSpec((tile, D), lambda i: (i, 0)),
                  pl.BlockSpec((1, D), lambda i: (0, 0)),
                  pl.BlockSpec(memory_space=pl.ANY)],
        out_specs=pl.BlockSpec((tile, D), lambda i: (i, 0)),
        scratch_shapes=[pltpu.VMEM((2, TOP_K, tile * ROW_TILE, LANES), F32), pltpu.SemaphoreType.DMA((2,))],
        compiler_params=_cparams(("arbitrary",)),
        name="combine",
    )(pos_tiles, pos_tiles, route, x1, g_final.reshape(1, D), ys)


def _layer(x2d, B, S, norm_mix_g, w_in, rel_bias, w_att, w_ret, w_out, norm_ffn_g, w_router, b_router,
           w1, b1, w2, b2):
    T, D = x2d.shape
    proj, a4, a16 = _inproj(x2d, norm_mix_g, w_in.astype(BF16), B, S)

    group_bias = [_band_bias(rel_bias[:, g * HEADS_PER_GROUP:(g + 1) * HEADS_PER_GROUP], d)
                  for g, d in enumerate(DILATIONS)]
    o1, l1 = _attention_group(proj.reshape(B, 1, S, proj.shape[1]), group_bias[0],
                              blocks=min(S // N_STEPS, 8))
    o2, l2 = _attention_group(a4, group_bias[1], blocks=min(a4.shape[2] // N_STEPS, 8))
    o3, l3 = _attention_group(a16, group_bias[2], blocks=min(a16.shape[2] // N_STEPS, 8))

    ret = _retention(proj, B, S, MAIN_RET_COL)

    w_router_p = jnp.zeros((D, LANES), BF16).at[:, :N_EXPERTS].set(w_router.astype(BF16))
    b_router_p = jnp.full((1, LANES), NEG_BIG, F32).at[0, :N_EXPERTS].set(b_router.astype(F32))
    x1, hf, route, counts = _merge(x2d, B, S, o1.reshape(T, GROUP_WIDTH), l1.reshape(T, LANES), o2, l2, o3, l3,
                                   ret, proj, MAIN_GATE_COL, w_att.astype(BF16), w_ret.astype(BF16),
                                   w_out.astype(BF16), norm_ffn_g.reshape(1, D), w_router_p, b_router_p)

    M = EXPERT_BLOCK
    n_blocks = (T * TOP_K) // M + N_EXPERTS
    cnt = counts[0, :N_EXPERTS].astype(jnp.int32)
    padded = ((cnt + M - 1) // M) * M
    pend = jnp.cumsum(padded)
    pstart = pend - padded
    idx = route[:, ROUTE_IDX:ROUTE_IDX + TOP_K].astype(jnp.int32)
    rank = route[:, ROUTE_RANK:ROUTE_RANK + TOP_K].astype(jnp.int32)
    experts = jnp.arange(N_EXPERTS, dtype=jnp.int32)
    pos = rank + jnp.sum(jnp.where(idx[..., None] == experts, pstart, 0), axis=-1)
    n_used = (pend[-1] // M).astype(jnp.int32)
    blk_row = jnp.minimum(jnp.arange(n_blocks, dtype=jnp.int32), jnp.maximum(n_used - 1, 0)) * M
    blk_expert = jnp.minimum(jnp.sum((pend[None, :] <= blk_row[:, None]).astype(jnp.int32), axis=-1),
                             N_EXPERTS - 1)

    assert D == ROW_TILE * LANES
    tile = 256
    pos_tiles = (pos * ROW_TILE).reshape(T // tile, tile, TOP_K).transpose(0, 2, 1)
    pad_info = jnp.concatenate([(pstart + cnt) * ROW_TILE, padded - cnt]).astype(jnp.int32)
    xs = _dispatch(hf, pos_tiles, pad_info, n_blocks * M, tile)
    ys = _experts(xs, n_blocks, blk_expert, n_used.reshape(1), w1, b1, w2, b2)
    return pos_tiles, route, x1, ys


def kernel(x, norm_mix_g, w_in, rel_bias, w_att_branch, w_ret_branch, w_out, norm_ffn_g, w_router, b_router,
           w1, b1, w2, b2, norm_final_g):
    B, S, D = x.shape
    depth = w_in.shape[0]
    assert depth == 1, "the combine stage applies the final norm; a deeper stack needs a separate norm pass"
    x2d = x.reshape(B * S, D)
    pos_tiles, route, x1, ys = _layer(x2d, B, S, norm_mix_g[0], w_in[0], rel_bias, w_att_branch[0],
                                      w_ret_branch[0], w_out[0], norm_ffn_g[0], w_router[0], b_router[0],
                                      w1[0], b1[0], w2[0], b2[0])
    out = _combine(pos_tiles, route, x1, norm_final_g, ys, tile=256)
    return out.reshape(B, S, D)
```

```python
import functools
import math

import numpy as np
import jax
import jax.numpy as jnp
from jax import lax
from jax.experimental import pallas as pl
from jax.experimental.pallas import tpu as pltpu

F32 = jnp.float32
BF16 = jnp.bfloat16

NORM_EPS = 1e-5

HEAD_DIM = 128
HEADS_PER_GROUP = 4
GROUP_WIDTH = HEADS_PER_GROUP * HEAD_DIM
DILATIONS = (1, 4, 16)
N_STEPS = 128
N_GROUPS = len(DILATIONS)
REL_BUCKETS = 32
REL_MAX_DIST = 2048

RET_HEADS = 4
RET_DK = 128
RET_DV = 256
RET_CHUNK = 128
ROPE_BASE = 10000.0

N_EXPERTS = 32
TOP_K = 4
SWIGLU_LIMIT = 7.0
SWIGLU_ALPHA = 1.702
EXPERT_BLOCK = 512

LANES = 128
COL_BLOCK = 512
NEG_BIG = -1e30

VMEM_LIMIT = 48 * 1024 * 1024
VMEM_LIMIT_BIG = 56 * 1024 * 1024

N_ATT_CHUNKS = 3 * N_GROUPS
MAIN_RET_COL = 3
MAIN_GATE_COL = MAIN_RET_COL + (2 * RET_HEADS * RET_DK + 2 * RET_HEADS * RET_DV) // COL_BLOCK


def _cparams(sem, vmem=VMEM_LIMIT):
    return pltpu.CompilerParams(dimension_semantics=sem, vmem_limit_bytes=vmem)


def _inproj_kernel(x_ref, g_ref, w_ref, main_ref, a4_ref, a16_ref, h_ref, slab_ref):
    tm = x_ref.shape[0]
    x = x_ref[...]
    ms = jnp.mean(x * x, axis=-1, keepdims=True)
    h_ref[...] = (x * lax.rsqrt(ms + NORM_EPS) * g_ref[...]).astype(BF16)
    n_chunks = w_ref.shape[1] // COL_BLOCK
    n_slabs = COL_BLOCK // LANES
    dilated = {1: (DILATIONS[1], a4_ref), 2: (DILATIONS[2], a16_ref)}
    for c in range(n_chunks):
        res = jnp.dot(h_ref[...], w_ref[:, c * COL_BLOCK:(c + 1) * COL_BLOCK], preferred_element_type=F32)
        if c >= N_ATT_CHUNKS:
            mc = c - N_ATT_CHUNKS + MAIN_RET_COL
            main_ref[:, mc * COL_BLOCK:(mc + 1) * COL_BLOCK] = res.astype(BF16)
            continue
        part, group = divmod(c, N_GROUPS)
        if group == 0:
            main_ref[:, part * COL_BLOCK:(part + 1) * COL_BLOCK] = res.astype(BF16)
            continue
        d, dest = dilated[group]
        for s in range(n_slabs):
            slab_ref[s] = res[:, s * LANES:(s + 1) * LANES]
        for r in range(d):
            for s in range(n_slabs):
                lo = part * COL_BLOCK + s * LANES
                dest[r, :, lo:lo + LANES] = slab_ref[s, pl.ds(r, tm // d, stride=d), :].astype(BF16)


def _inproj(x2d, g, w_bf16, B, S, tm=512):
    T, D = x2d.shape
    n_main = (w_bf16.shape[1] // COL_BLOCK - N_ATT_CHUNKS + 3) * COL_BLOCK
    d4, d16 = DILATIONS[1], DILATIONS[2]
    per_b = S // tm
    return pl.pallas_call(
        _inproj_kernel,
        out_shape=(jax.ShapeDtypeStruct((T, n_main), BF16),
                   jax.ShapeDtypeStruct((B, d4, S // d4, 3 * GROUP_WIDTH), BF16),
                   jax.ShapeDtypeStruct((B, d16, S // d16, 3 * GROUP_WIDTH), BF16)),
        grid=(T // tm,),
        in_specs=[
            pl.BlockSpec((tm, D), lambda i: (i, 0)),
            pl.BlockSpec((1, D), lambda i: (0, 0)),
            pl.BlockSpec(w_bf16.shape, lambda i: (0, 0), pipeline_mode=pl.Buffered(1)),
        ],
        out_specs=(pl.BlockSpec((tm, n_main), lambda i: (i, 0)),
                   pl.BlockSpec((None, d4, tm // d4, 3 * GROUP_WIDTH), lambda i: (i // per_b, 0, i % per_b, 0)),
                   pl.BlockSpec((None, d16, tm // d16, 3 * GROUP_WIDTH), lambda i: (i // per_b, 0, i % per_b, 0))),
        scratch_shapes=[pltpu.VMEM((tm, D), BF16), pltpu.VMEM((COL_BLOCK // LANES, tm, LANES), F32)],
        compiler_params=_cparams(("arbitrary",), VMEM_LIMIT_BIG),
        name="inproj",
    )(x2d, g.reshape(1, D), w_bf16)


def _attn_block(q, kp, kc, vp, vc, bias_ref, has_prev):
    scale = HEAD_DIM ** -0.5
    heads = range(HEADS_PER_GROUP)
    hs = [slice(h * HEAD_DIM, (h + 1) * HEAD_DIM) for h in heads]
    rs = [slice(h * N_STEPS, (h + 1) * N_STEPS) for h in heads]
    k = jnp.concatenate([kp, kc], axis=0)
    v = jnp.concatenate([vp, vc], axis=0)
    dn = (((1,), (1,)), ((), ()))
    s = jnp.concatenate([lax.dot_general(q[:, hs[h]], k[:, hs[h]], dn, preferred_element_type=F32)
                         for h in heads], axis=0)
    s = s * scale + bias_ref[...]
    if has_prev is not True:
        col = lax.broadcasted_iota(jnp.int32, s.shape, 1)
        s = jnp.where((col >= N_STEPS) | has_prev, s, NEG_BIG)
    m = jnp.max(s, axis=-1, keepdims=True)
    p = jnp.exp(s - m)
    l = jnp.sum(p, axis=-1, keepdims=True)
    pb = p.astype(BF16)
    lse_rows = m + jnp.log(l)
    lane = lax.broadcasted_iota(jnp.int32, (N_STEPS, LANES), 1)
    outs = []
    lse = jnp.zeros((N_STEPS, LANES), F32)
    for h in heads:
        acc = jnp.dot(pb[rs[h], :], v[:, hs[h]], preferred_element_type=F32)
        outs.append(acc / l[rs[h], :])
        lse = jnp.where(lane == h, lse_rows[rs[h], :], lse)
    return jnp.concatenate(outs, axis=-1), lse


def _attn_kernel(*refs, blocks, residues, with_prev):
    if with_prev:
        q_ref, k_ref, v_ref, kprev_ref, vprev_ref, bias_ref, o_ref, lse_ref = refs
    else:
        q_ref, k_ref, v_ref, bias_ref, o_ref, lse_ref = refs
    n = pl.program_id(2)

    def run(rr, j, kp, vp, has_prev):
        rows = slice(j * N_STEPS, (j + 1) * N_STEPS)
        o, lse = _attn_block(q_ref[rr, rows, :], kp, k_ref[rr, rows, :], vp, v_ref[rr, rows, :], bias_ref,
                             has_prev)
        o_ref[rr, rows, :] = o.astype(o_ref.dtype)
        lse_ref[rr, rows, :] = lse

    for rr in range(residues):
        if with_prev:
            run(rr, 0, kprev_ref[rr], vprev_ref[rr], n > 0)
        else:
            run(rr, 0, k_ref[rr, :N_STEPS, :], v_ref[rr, :N_STEPS, :], False)
        for j in range(1, blocks):
            prev = slice((j - 1) * N_STEPS, j * N_STEPS)
            run(rr, j, k_ref[rr, prev, :], v_ref[rr, prev, :], True)


def _attention_group(qkv, bias, blocks, residues):
    B, d, L, _ = qkv.shape
    W = GROUP_WIDTH
    rows = blocks * N_STEPS
    steps = L // rows
    with_prev = steps > 1

    def cur(c):
        return pl.BlockSpec((None, residues, rows, W), lambda b, r, n: (b, r, n, c))

    def prev(c):
        return pl.BlockSpec((None, residues, N_STEPS, W),
                            lambda b, r, n: (b, r, jnp.maximum(n * blocks - 1, 0), c))

    in_specs = [cur(0), cur(1), cur(2)] + ([prev(1), prev(2)] if with_prev else [])
    in_specs.append(pl.BlockSpec((HEADS_PER_GROUP * N_STEPS, 2 * N_STEPS), lambda b, r, n: (0, 0)))
    operands = [qkv] * (5 if with_prev else 3) + [bias]
    return pl.pallas_call(
        functools.partial(_attn_kernel, blocks=blocks, residues=residues, with_prev=with_prev),
        out_shape=(jax.ShapeDtypeStruct((B, d, L, W), BF16), jax.ShapeDtypeStruct((B, d, L, LANES), F32)),
        grid=(B, d // residues, steps),
        in_specs=in_specs,
        out_specs=(pl.BlockSpec((None, residues, rows, W), lambda b, r, n: (b, r, n, 0)),
                   pl.BlockSpec((None, residues, rows, LANES), lambda b, r, n: (b, r, n, 0))),
        compiler_params=_cparams(("arbitrary", "arbitrary", "arbitrary")),
        name=f"attn_d{d}",
    )(*operands)


def _t5_bucket(dist):
    max_exact = REL_BUCKETS // 2
    d_f = jnp.maximum(dist, 1).astype(F32)
    large = max_exact + (jnp.log(d_f / max_exact) / math.log(REL_MAX_DIST / max_exact)
                         * (REL_BUCKETS - max_exact)).astype(jnp.int32)
    large = jnp.minimum(large, REL_BUCKETS - 1)
    return jnp.where(dist < max_exact, dist, large)


def _band_bias(rel_bias_g, dilation):
    qi = jnp.arange(N_STEPS)[:, None]
    kj = jnp.arange(2 * N_STEPS)[None, :]
    step_dist = qi + N_STEPS - kj
    band = (step_dist >= 0) & (step_dist <= N_STEPS)
    bucket = _t5_bucket(jnp.maximum(step_dist, 0) * dilation)
    table = rel_bias_g.astype(F32).T
    hit = bucket[None, None] == jnp.arange(REL_BUCKETS)[None, :, None, None]
    bias = jnp.sum(jnp.where(hit, table[:, :, None, None], 0.0), axis=1)
    return jnp.where(band[None], bias, NEG_BIG).reshape(HEADS_PER_GROUP * N_STEPS, 2 * N_STEPS)


def _ret_kernel(q_ref, k_ref, v0_ref, v1_ref, g0_ref, g1_ref, cos_ref, sin_ref, decay_ref, xi_ref, zeta_ref,
                o_ref, state_ref, *, g_chunk):
    n = pl.program_id(1)

    @pl.when(n == 0)
    def _():
        state_ref[...] = jnp.zeros_like(state_ref)

    C = RET_CHUNK
    n_sub = q_ref.shape[0] // C
    lane = lax.broadcasted_iota(jnp.int32, (C, RET_DK), 1)
    even = (lane % 2) == 0

    def rotary(t, rows):
        partner = jnp.where(even, pltpu.roll(t, RET_DK - 1, 1), pltpu.roll(t, 1, 1))
        return t * cos_ref[rows, :] + partner * sin_ref[rows, :]

    heads = range(RET_HEADS)
    ks = [slice(h * RET_DK, (h + 1) * RET_DK) for h in heads]
    vs = [slice((h % 2) * RET_DV, (h % 2 + 1) * RET_DV) for h in heads]
    v_refs = [v0_ref if h < 2 else v1_ref for h in heads]
    g_refs = [g0_ref if h < 2 else g1_ref for h in heads]
    states = [state_ref[h] for h in heads]
    for sub in range(n_sub):
        rows = slice(sub * C, (sub + 1) * C)
        qbs, kzs, inners, vals = [], [], [], []
        for h in heads:
            qr = rotary(q_ref[rows, ks[h]].astype(F32), rows)
            kr = rotary(k_ref[rows, ks[h]].astype(F32), rows) * (RET_DK ** -0.5)
            v = v_refs[h][rows, vs[h]]
            qb = qr.astype(BF16)
            scores = lax.dot_general(qb, kr.astype(BF16), (((1,), (1,)), ((), ())),
                                     preferred_element_type=F32) * decay_ref[h]
            inners.append(jnp.dot(scores.astype(BF16), v, preferred_element_type=F32))
            qbs.append(qb)
            kzs.append((kr * zeta_ref[h]).astype(BF16))
            vals.append(v)
        for h in heads:
            cross = jnp.dot(qbs[h], states[h].astype(BF16), preferred_element_type=F32) * xi_ref[h]
            states[h] = states[h] * g_chunk[h] + lax.dot_general(kzs[h], vals[h], (((0,), (0,)), ((), ())),
                                                                 preferred_element_type=F32)
            ret = inners[h] + cross
            mu = jnp.mean(ret, axis=-1, keepdims=True)
            cen = ret - mu
            var = jnp.mean(cen * cen, axis=-1, keepdims=True)
            normed = cen * lax.rsqrt(var + NORM_EPS)
            g = g_refs[h][rows, vs[h]].astype(F32)
            o_ref[rows, h * RET_DV:(h + 1) * RET_DV] = (g * jax.nn.sigmoid(g) * normed).astype(o_ref.dtype)
    for h in heads:
        state_ref[h] = states[h]


def _retention(proj, B, S, col0):
    T = B * S
    nc = S // RET_CHUNK
    C = RET_CHUNK
    log_g = np.log(1.0 - 2.0 ** (-5.0 - np.arange(RET_HEADS, dtype=np.float64)))
    idx = np.arange(C, dtype=np.float64)
    diff = idx[:, None] - idx[None, :]
    decay = np.where(diff >= 0, np.exp(np.maximum(diff, 0.0)[None] * log_g[:, None, None]), 0.0)
    xi = np.exp((idx + 1.0)[None, :] * log_g[:, None])
    zeta = np.exp((C - 1.0 - idx)[None, :] * log_g[:, None])
    g_chunk = tuple(float(v) for v in np.exp(C * log_g))
    xi_b = np.broadcast_to(xi[:, :, None], (RET_HEADS, C, RET_DV)).astype(np.float32)
    zeta_b = np.broadcast_to(zeta[:, :, None], (RET_HEADS, C, RET_DK)).astype(np.float32)

    inv = ROPE_BASE ** (-np.arange(0, RET_DK, 2, dtype=np.float64) / RET_DK)
    ang = np.arange(S, dtype=np.float64)[:, None] * inv[None]
    cos_t = np.repeat(np.cos(ang), 2, axis=1).astype(np.float32)
    sin_t = np.stack([-np.sin(ang), np.sin(ang)], axis=-1).reshape(S, RET_DK).astype(np.float32)

    rows = 2 * C if nc % 2 == 0 else C
    ns = S // rows

    def col(c):
        return pl.BlockSpec((rows, COL_BLOCK), lambda b, n, c=c: (b * ns + n, c))

    const3 = lambda shape: pl.BlockSpec(shape, lambda b, n: (0, 0, 0))
    return pl.pallas_call(
        functools.partial(_ret_kernel, g_chunk=g_chunk),
        out_shape=jax.ShapeDtypeStruct((T, RET_HEADS * RET_DV), BF16),
        grid=(B, ns),
        in_specs=[col(col0), col(col0 + 1), col(col0 + 2), col(col0 + 3), col(col0 + 4), col(col0 + 5),
                  pl.BlockSpec((rows, RET_DK), lambda b, n: (n, 0)),
                  pl.BlockSpec((rows, RET_DK), lambda b, n: (n, 0)),
                  const3((RET_HEADS, C, C)), const3((RET_HEADS, C, RET_DV)), const3((RET_HEADS, C, RET_DK))],
        out_specs=pl.BlockSpec((rows, RET_HEADS * RET_DV), lambda b, n: (b * ns + n, 0)),
        scratch_shapes=[pltpu.VMEM((RET_HEADS, RET_DK, RET_DV), F32)],
        compiler_params=_cparams(("arbitrary", "arbitrary")),
        name="retention",
    )(proj, proj, proj, proj, proj, proj, jnp.asarray(cos_t), jnp.asarray(sin_t),
      jnp.asarray(decay.astype(np.float32)), jnp.asarray(xi_b), jnp.asarray(zeta_b))


ROUTE_IDX, ROUTE_RANK, ROUTE_GATE = 0, 4, 8


ROW_TILE = 8


def _store_row_tiles(dst_ref, val):
    n = val.shape[0]
    for c in range(ROW_TILE):
        dst_ref[pl.ds(c, n, stride=ROW_TILE), :] = val[:, c * LANES:(c + 1) * LANES]


def _load_row_tiles(src_ref, n):
    return jnp.concatenate([src_ref[pl.ds(c, n, stride=ROW_TILE), :] for c in range(ROW_TILE)], axis=-1)


def _to_token_order(src_ref, dst_ref, d):
    n = src_ref.shape[1]
    for r in range(d):
        blk = src_ref[r].astype(F32)
        for s in range(dst_ref.shape[0]):
            dst_ref[s, pl.ds(r, n, stride=d), :] = blk[:, s * LANES:(s + 1) * LANES]


def _merge_kernel(x_ref, o1_ref, o2_ref, o3_ref, l1_ref, l2_ref, l3_ref, ret_ref,
                  ga0_ref, ga1_ref, gr0_ref, gr1_ref, watt_ref, wret_ref, wout_ref, gffn_ref, wr_ref, br_ref,
                  below_ref, x1_ref, hf_ref, route_ref, counts_ref, on2_ref, on3_ref, ln2_ref, ln3_ref):
    tm = x_ref.shape[0]
    _to_token_order(o2_ref, on2_ref, DILATIONS[1])
    _to_token_order(o3_ref, on3_ref, DILATIONS[2])
    _to_token_order(l2_ref, ln2_ref, DILATIONS[1])
    _to_token_order(l3_ref, ln3_ref, DILATIONS[2])
    lses = [l1_ref[...], ln2_ref[0], ln3_ref[0]]

    att_parts = []
    for h in range(HEADS_PER_GROUP):
        ls = [l[:, h:h + 1] for l in lses]
        m = jnp.maximum(jnp.maximum(ls[0], ls[1]), ls[2])
        ws = [jnp.exp(l - m) for l in ls]
        wsum = ws[0] + ws[1] + ws[2]
        hs = slice(h * HEAD_DIM, (h + 1) * HEAD_DIM)
        acc = (ws[0] / wsum) * o1_ref[:, hs].astype(F32)
        acc = acc + (ws[1] / wsum) * on2_ref[h]
        acc = acc + (ws[2] / wsum) * on3_ref[h]
        att_parts.append(acc)
    att = jnp.concatenate(att_parts, axis=-1).astype(BF16)

    a_proj = jnp.dot(att, watt_ref[...], preferred_element_type=F32)
    r_proj = jnp.dot(ret_ref[...], wret_ref[...], preferred_element_type=F32)
    gate_a = jnp.concatenate([ga0_ref[...], ga1_ref[...]], axis=-1).astype(F32)
    gate_r = jnp.concatenate([gr0_ref[...], gr1_ref[...]], axis=-1).astype(F32)
    merged = jax.nn.sigmoid(gate_a) * a_proj + jax.nn.sigmoid(gate_r) * r_proj
    x1 = x_ref[...] + jnp.dot(merged.astype(BF16), wout_ref[...], preferred_element_type=F32)
    x1_ref[...] = x1

    ms = jnp.mean(x1 * x1, axis=-1, keepdims=True)
    hf = x1 * lax.rsqrt(ms + NORM_EPS) * gffn_ref[...]
    _store_row_tiles(hf_ref, hf)

    logits = jnp.dot(hf.astype(BF16), wr_ref[...], preferred_element_type=F32) + br_ref[...]
    lane = lax.broadcasted_iota(jnp.int32, (tm, LANES), 1)
    lane_f = lane.astype(F32)
    work = logits
    vals, idxs, hits = [], [], []
    for _ in range(TOP_K):
        mk = jnp.max(work, axis=-1, keepdims=True)
        ik = jnp.min(jnp.where(work == mk, lane_f, float(LANES)), axis=-1, keepdims=True)
        hit = lane_f == ik
        work = jnp.where(hit, -jnp.inf, work)
        vals.append(mk)
        idxs.append(ik)
        hits.append(hit)
    es = [jnp.exp(v - vals[0]) for v in vals]
    esum = es[0] + es[1] + es[2] + es[3]
    gates = [e / esum for e in es]

    sel = jnp.zeros((tm, LANES), F32)
    for hit in hits:
        sel = jnp.where(hit, 1.0, sel)

    @pl.when(pl.program_id(0) == 0)
    def _():
        counts_ref[...] = jnp.zeros_like(counts_ref)

    rank = jnp.dot(below_ref[...], sel.astype(BF16), preferred_element_type=F32) + counts_ref[...]
    counts_ref[...] = counts_ref[...] + jnp.sum(sel, axis=0, keepdims=True)

    route = jnp.zeros((tm, LANES), F32)
    for k in range(TOP_K):
        rank_k = jnp.sum(jnp.where(hits[k], rank, 0.0), axis=-1, keepdims=True)
        route = jnp.where(lane == ROUTE_IDX + k, idxs[k], route)
        route = jnp.where(lane == ROUTE_RANK + k, rank_k, route)
        route = jnp.where(lane == ROUTE_GATE + k, gates[k], route)
    route_ref[...] = route


def _merge(x2d, B, S, o1, l1, o2, l2, o3, l3, ret, proj, gate_col0, w_att, w_ret, w_out, g_ffn, w_router_p,
           b_router_p, tm=256):
    T, D = x2d.shape
    per_b = S // tm
    d4, d16 = DILATIONS[1], DILATIONS[2]
    row = lambda w: pl.BlockSpec((tm, w), lambda i: (i, 0))
    full = lambda a: pl.BlockSpec(a.shape, lambda i: (0,) * a.ndim)
    gcol = lambda c: pl.BlockSpec((tm, COL_BLOCK), lambda i, c=c: (i, c))
    res = lambda d, w: pl.BlockSpec((None, d, tm // d, w), lambda i: (i // per_b, 0, i % per_b, 0))
    below = jnp.tril(jnp.ones((tm, tm), BF16), k=-1)
    slabs = lambda n: pltpu.VMEM((n, tm, LANES), F32)
    return pl.pallas_call(
        _merge_kernel,
        out_shape=(jax.ShapeDtypeStruct((T, D), F32), jax.ShapeDtypeStruct((T * ROW_TILE, LANES), F32),
                   jax.ShapeDtypeStruct((T, LANES), F32), jax.ShapeDtypeStruct((1, LANES), F32)),
        grid=(T // tm,),
        in_specs=[row(D), row(GROUP_WIDTH), res(d4, GROUP_WIDTH), res(d16, GROUP_WIDTH),
                  row(LANES), res(d4, LANES), res(d16, LANES),
                  row(RET_HEADS * RET_DV), gcol(gate_col0), gcol(gate_col0 + 1), gcol(gate_col0 + 2),
                  gcol(gate_col0 + 3), full(w_att), full(w_ret), full(w_out), full(g_ffn), full(w_router_p),
                  full(b_router_p), full(below)],
        out_specs=(row(D), pl.BlockSpec((tm * ROW_TILE, LANES), lambda i: (i, 0)), row(LANES),
                   pl.BlockSpec((1, LANES), lambda i: (0, 0))),
        scratch_shapes=[slabs(GROUP_WIDTH // LANES), slabs(GROUP_WIDTH // LANES), slabs(1), slabs(1)],
        compiler_params=_cparams(("arbitrary",)),
        name="merge",
    )(x2d, o1, o2, o3, l1, l2, l3, ret, proj, proj, proj, proj, w_att, w_ret, w_out, g_ffn, w_router_p,
      b_router_p, below)


def _row_copy(src, dst, src_row8, dst_row8, sem):
    return pltpu.make_async_copy(src.at[pl.ds(pl.multiple_of(src_row8, ROW_TILE), ROW_TILE)],
                                 dst.at[pl.ds(pl.multiple_of(dst_row8, ROW_TILE), ROW_TILE)], sem)


def _dispatch_kernel(pad_ref, pos_ref, hf_ref, xs_hbm, zero_ref, sem, *, tile):
    @pl.when(pl.program_id(0) == 0)
    def _():
        zero_ref[...] = jnp.zeros_like(zero_ref)
        total = 0
        for e in range(N_EXPERTS):
            first = pad_ref[e]

            def zero_row(j, carry, first=first):
                _row_copy(zero_ref, xs_hbm, 0, first + j * ROW_TILE, sem).start()
                return carry

            lax.fori_loop(0, pad_ref[N_EXPERTS + e], zero_row, 0)
            total = total + pad_ref[N_EXPERTS + e]

        def zero_wait(n, carry):
            _row_copy(zero_ref, xs_hbm, 0, 0, sem).wait()
            return carry

        lax.fori_loop(0, total, zero_wait, 0)

    def issue(i, carry):
        for k in range(TOP_K):
            _row_copy(hf_ref, xs_hbm, i * ROW_TILE, pos_ref[0, k, i], sem).start(priority=k % 2)
        return carry

    lax.fori_loop(0, tile, issue, 0, unroll=8)

    def drain(i, carry):
        for k in range(TOP_K):
            _row_copy(hf_ref, xs_hbm, 0, 0, sem).wait()
        return carry

    lax.fori_loop(0, tile, drain, 0, unroll=8)


def _dispatch(hf, pos8_tiles, pad_info, n_rows, tile):
    T = hf.shape[0] // ROW_TILE
    grid_spec = pltpu.PrefetchScalarGridSpec(
        num_scalar_prefetch=1,
        grid=(T // tile,),
        in_specs=[pl.BlockSpec((1, TOP_K, tile), lambda i, pad: (i, 0, 0), memory_space=pltpu.SMEM),
                  pl.BlockSpec((tile * ROW_TILE, LANES), lambda i, pad: (i, 0))],
        out_specs=pl.BlockSpec(memory_space=pl.ANY),
        scratch_shapes=[pltpu.VMEM((ROW_TILE, LANES), hf.dtype), pltpu.SemaphoreType.DMA(())],
    )
    return pl.pallas_call(
        functools.partial(_dispatch_kernel, tile=tile),
        out_shape=jax.ShapeDtypeStruct((n_rows * ROW_TILE, LANES), hf.dtype),
        grid_spec=grid_spec,
        compiler_params=_cparams(("arbitrary",)),
        name="dispatch",
    )(pad_info, pos8_tiles, hf)


def _expert_kernel(be_ref, nu_ref, x_ref, w1_ref, b1_ref, w2_ref, b2_ref, y_ref, w1b_ref, w2b_ref):
    i = pl.program_id(0)
    F = w2_ref.shape[0]

    @pl.when(i < nu_ref[0])
    def _():
        @pl.when((i == 0) | (be_ref[i] != be_ref[jnp.maximum(i - 1, 0)]))
        def _():
            w1b_ref[...] = w1_ref[...].astype(BF16)
            w2b_ref[...] = w2_ref[...].astype(BF16)

        x = _load_row_tiles(x_ref, EXPERT_BLOCK).astype(BF16)
        gu = jnp.dot(x, w1b_ref[...], preferred_element_type=F32) + b1_ref[...]
        gate = jnp.minimum(gu[:, :F], SWIGLU_LIMIT)
        up = jnp.clip(gu[:, F:], -SWIGLU_LIMIT, SWIGLU_LIMIT)
        act = (up + 1.0) * gate * jax.nn.sigmoid(SWIGLU_ALPHA * gate)
        _store_row_tiles(y_ref, jnp.dot(act.astype(BF16), w2b_ref[...], preferred_element_type=F32) + b2_ref[...])


def _experts(xs, n_blocks, blk_expert, n_used, w1, b1, w2, b2):
    E, D, F2 = w1.shape
    F = w2.shape[1]
    M = EXPERT_BLOCK
    blk = lambda i, be, nu: (jnp.minimum(i, nu[0] - 1), 0)
    grid_spec = pltpu.PrefetchScalarGridSpec(
        num_scalar_prefetch=2,
        grid=(n_blocks,),
        in_specs=[pl.BlockSpec((M * ROW_TILE, LANES), blk),
                  pl.BlockSpec((None, D, F2), lambda i, be, nu: (be[i], 0, 0)),
                  pl.BlockSpec((None, 1, F2), lambda i, be, nu: (be[i], 0, 0)),
                  pl.BlockSpec((None, F, D), lambda i, be, nu: (be[i], 0, 0)),
                  pl.BlockSpec((None, 1, D), lambda i, be, nu: (be[i], 0, 0))],
        out_specs=pl.BlockSpec((M * ROW_TILE, LANES), blk),
        scratch_shapes=[pltpu.VMEM((D, F2), BF16), pltpu.VMEM((F, D), BF16)],
    )
    return pl.pallas_call(
        _expert_kernel,
        out_shape=jax.ShapeDtypeStruct((n_blocks * M * ROW_TILE, LANES), F32),
        grid_spec=grid_spec,
        compiler_params=_cparams(("arbitrary",), VMEM_LIMIT_BIG),
        name="experts",
    )(blk_expert, n_used, xs, w1, b1.reshape(E, 1, F2), w2, b2.reshape(E, 1, D))


def _combine_kernel(pos_ref, pos_next_ref, route_ref, x1_ref, g_ref, ys_hbm, o_ref, buf_ref, sem, *, tile):
    step = pl.program_id(0)
    cur = lax.rem(step, 2)

    def gather(p_ref, slot):
        def issue(i, carry):
            for k in range(TOP_K):
                _row_copy(ys_hbm, buf_ref.at[slot, k], p_ref[0, k, i], i * ROW_TILE,
                          sem.at[slot]).start(priority=k % 2)
            return carry

        lax.fori_loop(0, tile, issue, 0, unroll=8)

    @pl.when(step == 0)
    def _():
        gather(pos_ref, 0)

    @pl.when(step + 1 < pl.num_programs(0))
    def _():
        gather(pos_next_ref, 1 - cur)

    def drain(i, carry):
        for k in range(TOP_K):
            _row_copy(ys_hbm, buf_ref.at[cur, k], 0, 0, sem.at[cur]).wait()
        return carry

    lax.fori_loop(0, tile, drain, 0, unroll=8)

    route = route_ref[...]
    acc = x1_ref[...]
    for k in range(TOP_K):
        acc = acc + route[:, ROUTE_GATE + k:ROUTE_GATE + k + 1] * _load_row_tiles(buf_ref.at[cur, k], tile)
    ms = jnp.mean(acc * acc, axis=-1, keepdims=True)
    o_ref[...] = acc * lax.rsqrt(ms + NORM_EPS) * g_ref[...]


def _combine(pos_tiles, route, x1, g_final, ys, tile):
    T, D = x1.shape
    nt = T // tile
    return pl.pallas_call(
        functools.partial(_combine_kernel, tile=tile),
        out_shape=jax.ShapeDtypeStruct((T, D), F32),
        grid=(nt,),
        in_specs=[pl.BlockSpec((1, TOP_K, tile), lambda i: (i, 0, 0), memory_space=pltpu.SMEM),
                  pl.BlockSpec((1, TOP_K, tile), lambda i: (jnp.minimum(i + 1, nt - 1), 0, 0),
                               memory_space=pltpu.SMEM),
                  pl.BlockSpec((tile, LANES), lambda i: (i, 0)),
                  pl.BlockSpec((tile, D), lambda i: (i, 0)),
                  pl.BlockSpec((1, D), lambda i: (0, 0)),
                  pl.BlockSpec(memory_space=pl.ANY)],
        out_specs=pl.BlockSpec((tile, D), lambda i: (i, 0)),
        scratch_shapes=[pltpu.VMEM((2, TOP_K, tile * ROW_TILE, LANES), F32), pltpu.SemaphoreType.DMA((2,))],
        compiler_params=_cparams(("arbitrary",)),
        name="combine",
    )(pos_tiles, pos_tiles, route, x1, g_final.reshape(1, D), ys)


def _layer(x2d, B, S, norm_mix_g, w_in, rel_bias, w_att, w_ret, w_out, norm_ffn_g, w_router, b_router,
           w1, b1, w2, b2):
    T, D = x2d.shape
    proj, a4, a16 = _inproj(x2d, norm_mix_g, w_in.astype(BF16), B, S)

    group_bias = [_band_bias(rel_bias[:, g * HEADS_PER_GROUP:(g + 1) * HEADS_PER_GROUP], d)
                  for g, d in enumerate(DILATIONS)]
    per_step = 8

    def attend(qkv, bias):
        blocks = min(qkv.shape[2] // N_STEPS, per_step)
        residues = min(qkv.shape[1], max(per_step // blocks, 1))
        return _attention_group(qkv, bias, blocks, residues)

    o1, l1 = attend(proj.reshape(B, 1, S, proj.shape[1]), group_bias[0])
    o2, l2 = attend(a4, group_bias[1])
    o3, l3 = attend(a16, group_bias[2])

    ret = _retention(proj, B, S, MAIN_RET_COL)

    w_router_p = jnp.zeros((D, LANES), BF16).at[:, :N_EXPERTS].set(w_router.astype(BF16))
    b_router_p = jnp.full((1, LANES), NEG_BIG, F32).at[0, :N_EXPERTS].set(b_router.astype(F32))
    x1, hf, route, counts = _merge(x2d, B, S, o1.reshape(T, GROUP_WIDTH), l1.reshape(T, LANES), o2, l2, o3, l3,
                                   ret, proj, MAIN_GATE_COL, w_att.astype(BF16), w_ret.astype(BF16),
                                   w_out.astype(BF16), norm_ffn_g.reshape(1, D), w_router_p, b_router_p)

    M = EXPERT_BLOCK
    n_blocks = (T * TOP_K) // M + N_EXPERTS
    cnt = counts[0, :N_EXPERTS].astype(jnp.int32)
    padded = ((cnt + M - 1) // M) * M
    pend = jnp.cumsum(padded)
    pstart = pend - padded
    idx = route[:, ROUTE_IDX:ROUTE_IDX + TOP_K].astype(jnp.int32)
    rank = route[:, ROUTE_RANK:ROUTE_RANK + TOP_K].astype(jnp.int32)
    experts = jnp.arange(N_EXPERTS, dtype=jnp.int32)
    pos = rank + jnp.sum(jnp.where(idx[..., None] == experts, pstart, 0), axis=-1)
    n_used = (pend[-1] // M).astype(jnp.int32)
    blk_row = jnp.minimum(jnp.arange(n_blocks, dtype=jnp.int32), jnp.maximum(n_used - 1, 0)) * M
    blk_expert = jnp.minimum(jnp.sum((pend[None, :] <= blk_row[:, None]).astype(jnp.int32), axis=-1),
                             N_EXPERTS - 1)

    assert D == ROW_TILE * LANES
    tile = 256
    pos_tiles = (pos * ROW_TILE).reshape(T // tile, tile, TOP_K).transpose(0, 2, 1)
    pad_info = jnp.concatenate([(pstart + cnt) * ROW_TILE, padded - cnt]).astype(jnp.int32)
    xs = _dispatch(hf, pos_tiles, pad_info, n_blocks * M, tile)
    ys = _experts(xs, n_blocks, blk_expert, n_used.reshape(1), w1, b1, w2, b2)
    return pos_tiles, route, x1, ys


def kernel(x, norm_mix_g, w_in, rel_bias, w_att_branch, w_ret_branch, w_out, norm_ffn_g, w_router, b_router,
           w1, b1, w2, b2, norm_final_g):
    B, S, D = x.shape
    depth = w_in.shape[0]
    assert depth == 1, "the combine stage applies the final norm; a deeper stack needs a separate norm pass"
    x2d = x.reshape(B * S, D)
    pos_tiles, route, x1, ys = _layer(x2d, B, S, norm_mix_g[0], w_in[0], rel_bias, w_att_branch[0],
                                      w_ret_branch[0], w_out[0], norm_ffn_g[0], w_router[0], b_router[0],
                                      w1[0], b1[0], w2[0], b2[0])
    out = _combine(pos_tiles, route, x1, norm_final_g, ys, tile=256)
    return out.reshape(B, S, D)
```

```python
import functools
import math

import numpy as np
import jax
import jax.numpy as jnp
from jax import lax
from jax.experimental import pallas as pl
from jax.experimental.pallas import tpu as pltpu

F32 = jnp.float32
BF16 = jnp.bfloat16

NORM_EPS = 1e-5

HEAD_DIM = 128
HEADS_PER_GROUP = 4
GROUP_WIDTH = HEADS_PER_GROUP * HEAD_DIM
DILATIONS = (1, 4, 16)
N_STEPS = 128
N_GROUPS = len(DILATIONS)
REL_BUCKETS = 32
REL_MAX_DIST = 2048

RET_HEADS = 4
RET_DK = 128
RET_DV = 256
RET_CHUNK = 128
ROPE_BASE = 10000.0

N_EXPERTS = 32
TOP_K = 4
SWIGLU_LIMIT = 7.0
SWIGLU_ALPHA = 1.702
EXPERT_BLOCK = 512
ROW_MOVE_TILE = 512

LANES = 128
COL_BLOCK = 512
NEG_BIG = -1e30

VMEM_LIMIT = 48 * 1024 * 1024
VMEM_LIMIT_BIG = 56 * 1024 * 1024

N_ATT_CHUNKS = 3 * N_GROUPS
MAIN_RET_COL = 3
MAIN_GATE_COL = MAIN_RET_COL + (2 * RET_HEADS * RET_DK + 2 * RET_HEADS * RET_DV) // COL_BLOCK


def _cparams(sem, vmem=VMEM_LIMIT):
    return pltpu.CompilerParams(dimension_semantics=sem, vmem_limit_bytes=vmem)


def _inproj_kernel(x_ref, g_ref, w_ref, main_ref, a4_ref, a16_ref, h_ref, slab_ref):
    tm = x_ref.shape[0]
    x = x_ref[...]
    ms = jnp.mean(x * x, axis=-1, keepdims=True)
    h_ref[...] = (x * lax.rsqrt(ms + NORM_EPS) * g_ref[...]).astype(BF16)
    n_chunks = w_ref.shape[1] // COL_BLOCK
    n_slabs = COL_BLOCK // LANES
    dilated = {1: (DILATIONS[1], a4_ref), 2: (DILATIONS[2], a16_ref)}
    for c in range(n_chunks):
        res = jnp.dot(h_ref[...], w_ref[:, c * COL_BLOCK:(c + 1) * COL_BLOCK], preferred_element_type=F32)
        if c >= N_ATT_CHUNKS:
            mc = c - N_ATT_CHUNKS + MAIN_RET_COL
            main_ref[:, mc * COL_BLOCK:(mc + 1) * COL_BLOCK] = res.astype(BF16)
            continue
        part, group = divmod(c, N_GROUPS)
        if group == 0:
            main_ref[:, part * COL_BLOCK:(part + 1) * COL_BLOCK] = res.astype(BF16)
            continue
        d, dest = dilated[group]
        for s in range(n_slabs):
            slab_ref[s] = res[:, s * LANES:(s + 1) * LANES]
        for r in range(d):
            for s in range(n_slabs):
                lo = part * COL_BLOCK + s * LANES
                dest[r, :, lo:lo + LANES] = slab_ref[s, pl.ds(r, tm // d, stride=d), :].astype(BF16)


def _inproj(x2d, g, w_bf16, B, S, tm=512):
    T, D = x2d.shape
    n_main = (w_bf16.shape[1] // COL_BLOCK - N_ATT_CHUNKS + 3) * COL_BLOCK
    d4, d16 = DILATIONS[1], DILATIONS[2]
    per_b = S // tm
    return pl.pallas_call(
        _inproj_kernel,
        out_shape=(jax.ShapeDtypeStruct((T, n_main), BF16),
                   jax.ShapeDtypeStruct((B, d4, S // d4, 3 * GROUP_WIDTH), BF16),
                   jax.ShapeDtypeStruct((B, d16, S // d16, 3 * GROUP_WIDTH), BF16)),
        grid=(T // tm,),
        in_specs=[
            pl.BlockSpec((tm, D), lambda i: (i, 0)),
            pl.BlockSpec((1, D), lambda i: (0, 0)),
            pl.BlockSpec(w_bf16.shape, lambda i: (0, 0), pipeline_mode=pl.Buffered(1)),
        ],
        out_specs=(pl.BlockSpec((tm, n_main), lambda i: (i, 0)),
                   pl.BlockSpec((None, d4, tm // d4, 3 * GROUP_WIDTH), lambda i: (i // per_b, 0, i % per_b, 0)),
                   pl.BlockSpec((None, d16, tm // d16, 3 * GROUP_WIDTH), lambda i: (i // per_b, 0, i % per_b, 0))),
        scratch_shapes=[pltpu.VMEM((tm, D), BF16), pltpu.VMEM((COL_BLOCK // LANES, tm, LANES), F32)],
        compiler_params=_cparams(("arbitrary",), VMEM_LIMIT_BIG),
        name="inproj",
    )(x2d, g.reshape(1, D), w_bf16)


def _attn_block(q, kp, kc, vp, vc, bias_ref, has_prev):
    scale = HEAD_DIM ** -0.5
    heads = range(HEADS_PER_GROUP)
    hs = [slice(h * HEAD_DIM, (h + 1) * HEAD_DIM) for h in heads]
    rs = [slice(h * N_STEPS, (h + 1) * N_STEPS) for h in heads]
    k = jnp.concatenate([kp, kc], axis=0)
    v = jnp.concatenate([vp, vc], axis=0)
    dn = (((1,), (1,)), ((), ()))
    s = jnp.concatenate([lax.dot_general(q[:, hs[h]], k[:, hs[h]], dn, preferred_element_type=F32)
                         for h in heads], axis=0)
    s = s * scale + bias_ref[...]
    if has_prev is not True:
        col = lax.broadcasted_iota(jnp.int32, s.shape, 1)
        s = jnp.where((col >= N_STEPS) | has_prev, s, NEG_BIG)
    m = jnp.max(s, axis=-1, keepdims=True)
    p = jnp.exp(s - m)
    l = jnp.sum(p, axis=-1, keepdims=True)
    pb = p.astype(BF16)
    lse_rows = m + jnp.log(l)
    lane = lax.broadcasted_iota(jnp.int32, (N_STEPS, LANES), 1)
    outs = []
    lse = jnp.zeros((N_STEPS, LANES), F32)
    for h in heads:
        acc = jnp.dot(pb[rs[h], :], v[:, hs[h]], preferred_element_type=F32)
        outs.append(acc / l[rs[h], :])
        lse = jnp.where(lane == h, lse_rows[rs[h], :], lse)
    return jnp.concatenate(outs, axis=-1), lse


def _attn_kernel(*refs, blocks, residues, with_prev):
    if with_prev:
        q_ref, k_ref, v_ref, kprev_ref, vprev_ref, bias_ref, o_ref, lse_ref = refs
    else:
        q_ref, k_ref, v_ref, bias_ref, o_ref, lse_ref = refs
    n = pl.program_id(2)

    def run(rr, j, kp, vp, has_prev):
        rows = slice(j * N_STEPS, (j + 1) * N_STEPS)
        o, lse = _attn_block(q_ref[rr, rows, :], kp, k_ref[rr, rows, :], vp, v_ref[rr, rows, :], bias_ref,
                             has_prev)
        o_ref[rr, rows, :] = o.astype(o_ref.dtype)
        lse_ref[rr, rows, :] = lse

    for rr in range(residues):
        if with_prev:
            run(rr, 0, kprev_ref[rr], vprev_ref[rr], n > 0)
        else:
            run(rr, 0, k_ref[rr, :N_STEPS, :], v_ref[rr, :N_STEPS, :], False)
        for j in range(1, blocks):
            prev = slice((j - 1) * N_STEPS, j * N_STEPS)
            run(rr, j, k_ref[rr, prev, :], v_ref[rr, prev, :], True)


def _attention_group(qkv, bias, blocks, residues):
    B, d, L, _ = qkv.shape
    W = GROUP_WIDTH
    rows = blocks * N_STEPS
    steps = L // rows
    with_prev = steps > 1

    def cur(c):
        return pl.BlockSpec((None, residues, rows, W), lambda b, r, n: (b, r, n, c))

    def prev(c):
        return pl.BlockSpec((None, residues, N_STEPS, W),
                            lambda b, r, n: (b, r, jnp.maximum(n * blocks - 1, 0), c))

    in_specs = [cur(0), cur(1), cur(2)] + ([prev(1), prev(2)] if with_prev else [])
    in_specs.append(pl.BlockSpec((HEADS_PER_GROUP * N_STEPS, 2 * N_STEPS), lambda b, r, n: (0, 0)))
    operands = [qkv] * (5 if with_prev else 3) + [bias]
    return pl.pallas_call(
        functools.partial(_attn_kernel, blocks=blocks, residues=residues, with_prev=with_prev),
        out_shape=(jax.ShapeDtypeStruct((B, d, L, W), BF16), jax.ShapeDtypeStruct((B, d, L, LANES), F32)),
        grid=(B, d // residues, steps),
        in_specs=in_specs,
        out_specs=(pl.BlockSpec((None, residues, rows, W), lambda b, r, n: (b, r, n, 0)),
                   pl.BlockSpec((None, residues, rows, LANES), lambda b, r, n: (b, r, n, 0))),
        compiler_params=_cparams(("arbitrary", "arbitrary", "arbitrary")),
        name=f"attn_d{d}",
    )(*operands)


def _t5_bucket(dist):
    max_exact = REL_BUCKETS // 2
    d_f = jnp.maximum(dist, 1).astype(F32)
    large = max_exact + (jnp.log(d_f / max_exact) / math.log(REL_MAX_DIST / max_exact)
                         * (REL_BUCKETS - max_exact)).astype(jnp.int32)
    large = jnp.minimum(large, REL_BUCKETS - 1)
    return jnp.where(dist < max_exact, dist, large)


def _band_bias(rel_bias_g, dilation):
    qi = jnp.arange(N_STEPS)[:, None]
    kj = jnp.arange(2 * N_STEPS)[None, :]
    step_dist = qi + N_STEPS - kj
    band = (step_dist >= 0) & (step_dist <= N_STEPS)
    bucket = _t5_bucket(jnp.maximum(step_dist, 0) * dilation)
    table = rel_bias_g.astype(F32).T
    hit = bucket[None, None] == jnp.arange(REL_BUCKETS)[None, :, None, None]
    bias = jnp.sum(jnp.where(hit, table[:, :, None, None], 0.0), axis=1)
    return jnp.where(band[None], bias, NEG_BIG).reshape(HEADS_PER_GROUP * N_STEPS, 2 * N_STEPS)


def _ret_kernel(q_ref, k_ref, v0_ref, v1_ref, g0_ref, g1_ref, cos_ref, sin_ref, decay_ref, xi_ref, zeta_ref,
                o_ref, state_ref, *, g_chunk):
    n = pl.program_id(1)

    @pl.when(n == 0)
    def _():
        state_ref[...] = jnp.zeros_like(state_ref)

    C = RET_CHUNK
    n_sub = q_ref.shape[0] // C
    lane = lax.broadcasted_iota(jnp.int32, (C, RET_DK), 1)
    even = (lane % 2) == 0

    def rotary(t, rows):
        partner = jnp.where(even, pltpu.roll(t, RET_DK - 1, 1), pltpu.roll(t, 1, 1))
        return t * cos_ref[rows, :] + partner * sin_ref[rows, :]

    heads = range(RET_HEADS)
    ks = [slice(h * RET_DK, (h + 1) * RET_DK) for h in heads]
    vs = [slice((h % 2) * RET_DV, (h % 2 + 1) * RET_DV) for h in heads]
    v_refs = [v0_ref if h < 2 else v1_ref for h in heads]
    g_refs = [g0_ref if h < 2 else g1_ref for h in heads]
    states = [state_ref[h] for h in heads]
    for sub in range(n_sub):
        rows = slice(sub * C, (sub + 1) * C)
        qbs, kzs, inners, vals = [], [], [], []
        for h in heads:
            qr = rotary(q_ref[rows, ks[h]].astype(F32), rows)
            kr = rotary(k_ref[rows, ks[h]].astype(F32), rows) * (RET_DK ** -0.5)
            v = v_refs[h][rows, vs[h]]
            qb = qr.astype(BF16)
            scores = lax.dot_general(qb, kr.astype(BF16), (((1,), (1,)), ((), ())),
                                     preferred_element_type=F32) * decay_ref[h]
            inners.append(jnp.dot(scores.astype(BF16), v, preferred_element_type=F32))
            qbs.append(qb)
            kzs.append((kr * zeta_ref[h]).astype(BF16))
            vals.append(v)
        for h in heads:
            cross = jnp.dot(qbs[h], states[h].astype(BF16), preferred_element_type=F32) * xi_ref[h]
            states[h] = states[h] * g_chunk[h] + lax.dot_general(kzs[h], vals[h], (((0,), (0,)), ((), ())),
                                                                 preferred_element_type=F32)
            ret = inners[h] + cross
            mu = jnp.mean(ret, axis=-1, keepdims=True)
            cen = ret - mu
            var = jnp.mean(cen * cen, axis=-1, keepdims=True)
            normed = cen * lax.rsqrt(var + NORM_EPS)
            g = g_refs[h][rows, vs[h]].astype(F32)
            o_ref[rows, h * RET_DV:(h + 1) * RET_DV] = (g * jax.nn.sigmoid(g) * normed).astype(o_ref.dtype)
    for h in heads:
        state_ref[h] = states[h]


def _retention(proj, B, S, col0):
    T = B * S
    nc = S // RET_CHUNK
    C = RET_CHUNK
    log_g = np.log(1.0 - 2.0 ** (-5.0 - np.arange(RET_HEADS, dtype=np.float64)))
    idx = np.arange(C, dtype=np.float64)
    diff = idx[:, None] - idx[None, :]
    decay = np.where(diff >= 0, np.exp(np.maximum(diff, 0.0)[None] * log_g[:, None, None]), 0.0)
    xi = np.exp((idx + 1.0)[None, :] * log_g[:, None])
    zeta = np.exp((C - 1.0 - idx)[None, :] * log_g[:, None])
    g_chunk = tuple(float(v) for v in np.exp(C * log_g))
    xi_b = np.broadcast_to(xi[:, :, None], (RET_HEADS, C, RET_DV)).astype(np.float32)
    zeta_b = np.broadcast_to(zeta[:, :, None], (RET_HEADS, C, RET_DK)).astype(np.float32)

    inv = ROPE_BASE ** (-np.arange(0, RET_DK, 2, dtype=np.float64) / RET_DK)
    ang = np.arange(S, dtype=np.float64)[:, None] * inv[None]
    cos_t = np.repeat(np.cos(ang), 2, axis=1).astype(np.float32)
    sin_t = np.stack([-np.sin(ang), np.sin(ang)], axis=-1).reshape(S, RET_DK).astype(np.float32)

    rows = 2 * C if nc % 2 == 0 else C
    ns = S // rows

    def col(c):
        return pl.BlockSpec((rows, COL_BLOCK), lambda b, n, c=c: (b * ns + n, c))

    const3 = lambda shape: pl.BlockSpec(shape, lambda b, n: (0, 0, 0))
    return pl.pallas_call(
        functools.partial(_ret_kernel, g_chunk=g_chunk),
        out_shape=jax.ShapeDtypeStruct((T, RET_HEADS * RET_DV), BF16),
        grid=(B, ns),
        in_specs=[col(col0), col(col0 + 1), col(col0 + 2), col(col0 + 3), col(col0 + 4), col(col0 + 5),
                  pl.BlockSpec((rows, RET_DK), lambda b, n: (n, 0)),
                  pl.BlockSpec((rows, RET_DK), lambda b, n: (n, 0)),
                  const3((RET_HEADS, C, C)), const3((RET_HEADS, C, RET_DV)), const3((RET_HEADS, C, RET_DK))],
        out_specs=pl.BlockSpec((rows, RET_HEADS * RET_DV), lambda b, n: (b * ns + n, 0)),
        scratch_shapes=[pltpu.VMEM((RET_HEADS, RET_DK, RET_DV), F32)],
        compiler_params=_cparams(("arbitrary", "arbitrary")),
        name="retention",
    )(proj, proj, proj, proj, proj, proj, jnp.asarray(cos_t), jnp.asarray(sin_t),
      jnp.asarray(decay.astype(np.float32)), jnp.asarray(xi_b), jnp.asarray(zeta_b))


ROUTE_IDX, ROUTE_RANK, ROUTE_GATE = 0, 4, 8


ROW_TILE = 8


def _store_row_tiles(dst_ref, val):
    n = val.shape[0]
    for c in range(ROW_TILE):
        dst_ref[pl.ds(c, n, stride=ROW_TILE), :] = val[:, c * LANES:(c + 1) * LANES]


def _load_row_tiles(src_ref, n):
    return jnp.concatenate([src_ref[pl.ds(c, n, stride=ROW_TILE), :] for c in range(ROW_TILE)], axis=-1)


def _to_token_order(src_ref, dst_ref, d):
    n = src_ref.shape[1]
    for r in range(d):
        blk = src_ref[r].astype(F32)
        for s in range(dst_ref.shape[0]):
            dst_ref[s, pl.ds(r, n, stride=d), :] = blk[:, s * LANES:(s + 1) * LANES]


def _merge_kernel(x_ref, o1_ref, o2_ref, o3_ref, l1_ref, l2_ref, l3_ref, ret_ref,
                  ga0_ref, ga1_ref, gr0_ref, gr1_ref, watt_ref, wret_ref, wout_ref, gffn_ref, wr_ref, br_ref,
                  below_ref, x1_ref, hf_ref, route_ref, counts_ref, on2_ref, on3_ref, ln2_ref, ln3_ref):
    tm = x_ref.shape[0]
    _to_token_order(o2_ref, on2_ref, DILATIONS[1])
    _to_token_order(o3_ref, on3_ref, DILATIONS[2])
    _to_token_order(l2_ref, ln2_ref, DILATIONS[1])
    _to_token_order(l3_ref, ln3_ref, DILATIONS[2])
    lses = [l1_ref[...], ln2_ref[0], ln3_ref[0]]

    att_parts = []
    for h in range(HEADS_PER_GROUP):
        ls = [l[:, h:h + 1] for l in lses]
        m = jnp.maximum(jnp.maximum(ls[0], ls[1]), ls[2])
        ws = [jnp.exp(l - m) for l in ls]
        wsum = ws[0] + ws[1] + ws[2]
        hs = slice(h * HEAD_DIM, (h + 1) * HEAD_DIM)
        acc = (ws[0] / wsum) * o1_ref[:, hs].astype(F32)
        acc = acc + (ws[1] / wsum) * on2_ref[h]
        acc = acc + (ws[2] / wsum) * on3_ref[h]
        att_parts.append(acc)
    att = jnp.concatenate(att_parts, axis=-1).astype(BF16)

    a_proj = jnp.dot(att, watt_ref[...], preferred_element_type=F32)
    r_proj = jnp.dot(ret_ref[...], wret_ref[...], preferred_element_type=F32)
    gate_a = jnp.concatenate([ga0_ref[...], ga1_ref[...]], axis=-1).astype(F32)
    gate_r = jnp.concatenate([gr0_ref[...], gr1_ref[...]], axis=-1).astype(F32)
    merged = jax.nn.sigmoid(gate_a) * a_proj + jax.nn.sigmoid(gate_r) * r_proj
    x1 = x_ref[...] + jnp.dot(merged.astype(BF16), wout_ref[...], preferred_element_type=F32)
    x1_ref[...] = x1

    ms = jnp.mean(x1 * x1, axis=-1, keepdims=True)
    hf = x1 * lax.rsqrt(ms + NORM_EPS) * gffn_ref[...]
    _store_row_tiles(hf_ref, hf)

    logits = jnp.dot(hf.astype(BF16), wr_ref[...], preferred_element_type=F32) + br_ref[...]
    lane = lax.broadcasted_iota(jnp.int32, (tm, LANES), 1)
    lane_f = lane.astype(F32)
    work = logits
    vals, idxs, hits = [], [], []
    for _ in range(TOP_K):
        mk = jnp.max(work, axis=-1, keepdims=True)
        ik = jnp.min(jnp.where(work == mk, lane_f, float(LANES)), axis=-1, keepdims=True)
        hit = lane_f == ik
        work = jnp.where(hit, -jnp.inf, work)
        vals.append(mk)
        idxs.append(ik)
        hits.append(hit)
    es = [jnp.exp(v - vals[0]) for v in vals]
    esum = es[0] + es[1] + es[2] + es[3]
    gates = [e / esum for e in es]

    sel = jnp.zeros((tm, LANES), F32)
    for hit in hits:
        sel = jnp.where(hit, 1.0, sel)

    @pl.when(pl.program_id(0) == 0)
    def _():
        counts_ref[...] = jnp.zeros_like(counts_ref)

    rank = jnp.dot(below_ref[...], sel.astype(BF16), preferred_element_type=F32) + counts_ref[...]
    counts_ref[...] = counts_ref[...] + jnp.sum(sel, axis=0, keepdims=True)

    route = jnp.zeros((tm, LANES), F32)
    for k in range(TOP_K):
        rank_k = jnp.sum(jnp.where(hits[k], rank, 0.0), axis=-1, keepdims=True)
        route = jnp.where(lane == ROUTE_IDX + k, idxs[k], route)
        route = jnp.where(lane == ROUTE_RANK + k, rank_k, route)
        route = jnp.where(lane == ROUTE_GATE + k, gates[k], route)
    route_ref[...] = route


def _merge(x2d, B, S, o1, l1, o2, l2, o3, l3, ret, proj, gate_col0, w_att, w_ret, w_out, g_ffn, w_router_p,
           b_router_p, tm=512):
    T, D = x2d.shape
    per_b = S // tm
    d4, d16 = DILATIONS[1], DILATIONS[2]
    row = lambda w: pl.BlockSpec((tm, w), lambda i: (i, 0))
    full = lambda a: pl.BlockSpec(a.shape, lambda i: (0,) * a.ndim)
    gcol = lambda c: pl.BlockSpec((tm, COL_BLOCK), lambda i, c=c: (i, c))
    res = lambda d, w: pl.BlockSpec((None, d, tm // d, w), lambda i: (i // per_b, 0, i % per_b, 0))
    below = jnp.tril(jnp.ones((tm, tm), BF16), k=-1)
    slabs = lambda n: pltpu.VMEM((n, tm, LANES), F32)
    return pl.pallas_call(
        _merge_kernel,
        out_shape=(jax.ShapeDtypeStruct((T, D), F32), jax.ShapeDtypeStruct((T * ROW_TILE, LANES), F32),
                   jax.ShapeDtypeStruct((T, LANES), F32), jax.ShapeDtypeStruct((1, LANES), F32)),
        grid=(T // tm,),
        in_specs=[row(D), row(GROUP_WIDTH), res(d4, GROUP_WIDTH), res(d16, GROUP_WIDTH),
                  row(LANES), res(d4, LANES), res(d16, LANES),
                  row(RET_HEADS * RET_DV), gcol(gate_col0), gcol(gate_col0 + 1), gcol(gate_col0 + 2),
                  gcol(gate_col0 + 3), full(w_att), full(w_ret), full(w_out), full(g_ffn), full(w_router_p),
                  full(b_router_p), full(below)],
        out_specs=(row(D), pl.BlockSpec((tm * ROW_TILE, LANES), lambda i: (i, 0)), row(LANES),
                   pl.BlockSpec((1, LANES), lambda i: (0, 0))),
        scratch_shapes=[slabs(GROUP_WIDTH // LANES), slabs(GROUP_WIDTH // LANES), slabs(1), slabs(1)],
        compiler_params=_cparams(("arbitrary",), VMEM_LIMIT_BIG),
        name="merge",
    )(x2d, o1, o2, o3, l1, l2, l3, ret, proj, proj, proj, proj, w_att, w_ret, w_out, g_ffn, w_router_p,
      b_router_p, below)


def _row_copy(src, dst, src_row8, dst_row8, sem):
    return pltpu.make_async_copy(src.at[pl.ds(pl.multiple_of(src_row8, ROW_TILE), ROW_TILE)],
                                 dst.at[pl.ds(pl.multiple_of(dst_row8, ROW_TILE), ROW_TILE)], sem)


def _dispatch_kernel(pad_ref, pos_ref, hf_ref, xs_hbm, zero_ref, sem, *, tile):
    @pl.when(pl.program_id(0) == 0)
    def _():
        zero_ref[...] = jnp.zeros_like(zero_ref)
        total = 0
        for e in range(N_EXPERTS):
            first = pad_ref[e]

            def zero_row(j, carry, first=first):
                _row_copy(zero_ref, xs_hbm, 0, first + j * ROW_TILE, sem).start()
                return carry

            lax.fori_loop(0, pad_ref[N_EXPERTS + e], zero_row, 0)
            total = total + pad_ref[N_EXPERTS + e]

        def zero_wait(n, carry):
            _row_copy(zero_ref, xs_hbm, 0, 0, sem).wait()
            return carry

        lax.fori_loop(0, total, zero_wait, 0)

    def issue(i, carry):
        for k in range(TOP_K):
            _row_copy(hf_ref, xs_hbm, i * ROW_TILE, pos_ref[0, k, i], sem).start(priority=k % 2)
        return carry

    lax.fori_loop(0, tile, issue, 0, unroll=8)

    def drain(i, carry):
        for k in range(TOP_K):
            _row_copy(hf_ref, xs_hbm, 0, 0, sem).wait()
        return carry

    lax.fori_loop(0, tile, drain, 0, unroll=8)


def _dispatch(hf, pos8_tiles, pad_info, n_rows, tile):
    T = hf.shape[0] // ROW_TILE
    grid_spec = pltpu.PrefetchScalarGridSpec(
        num_scalar_prefetch=1,
        grid=(T // tile,),
        in_specs=[pl.BlockSpec((1, TOP_K, tile), lambda i, pad: (i, 0, 0), memory_space=pltpu.SMEM),
                  pl.BlockSpec((tile * ROW_TILE, LANES), lambda i, pad: (i, 0))],
        out_specs=pl.BlockSpec(memory_space=pl.ANY),
        scratch_shapes=[pltpu.VMEM((ROW_TILE, LANES), hf.dtype), pltpu.SemaphoreType.DMA(())],
    )
    return pl.pallas_call(
        functools.partial(_dispatch_kernel, tile=tile),
        out_shape=jax.ShapeDtypeStruct((n_rows * ROW_TILE, LANES), hf.dtype),
        grid_spec=grid_spec,
        compiler_params=_cparams(("arbitrary",)),
        name="dispatch",
    )(pad_info, pos8_tiles, hf)


def _expert_kernel(be_ref, nu_ref, x_ref, w1_ref, b1_ref, w2_ref, b2_ref, y_ref, w1b_ref, w2b_ref):
    i = pl.program_id(0)
    F = w2_ref.shape[0]

    @pl.when(i < nu_ref[0])
    def _():
        @pl.when((i == 0) | (be_ref[i] != be_ref[jnp.maximum(i - 1, 0)]))
        def _():
            w1b_ref[...] = w1_ref[...].astype(BF16)
            w2b_ref[...] = w2_ref[...].astype(BF16)

        x = _load_row_tiles(x_ref, EXPERT_BLOCK).astype(BF16)
        gu = jnp.dot(x, w1b_ref[...], preferred_element_type=F32) + b1_ref[...]
        gate = jnp.minimum(gu[:, :F], SWIGLU_LIMIT)
        up = jnp.clip(gu[:, F:], -SWIGLU_LIMIT, SWIGLU_LIMIT)
        act = (up + 1.0) * gate * jax.nn.sigmoid(SWIGLU_ALPHA * gate)
        _store_row_tiles(y_ref, jnp.dot(act.astype(BF16), w2b_ref[...], preferred_element_type=F32) + b2_ref[...])


def _experts(xs, n_blocks, blk_expert, n_used, w1, b1, w2, b2):
    E, D, F2 = w1.shape
    F = w2.shape[1]
    M = EXPERT_BLOCK
    blk = lambda i, be, nu: (jnp.minimum(i, nu[0] - 1), 0)
    grid_spec = pltpu.PrefetchScalarGridSpec(
        num_scalar_prefetch=2,
        grid=(n_blocks,),
        in_specs=[pl.BlockSpec((M * ROW_TILE, LANES), blk),
                  pl.BlockSpec((None, D, F2), lambda i, be, nu: (be[i], 0, 0)),
                  pl.BlockSpec((None, 1, F2), lambda i, be, nu: (be[i], 0, 0)),
                  pl.BlockSpec((None, F, D), lambda i, be, nu: (be[i], 0, 0)),
                  pl.BlockSpec((None, 1, D), lambda i, be, nu: (be[i], 0, 0))],
        out_specs=pl.BlockSpec((M * ROW_TILE, LANES), blk),
        scratch_shapes=[pltpu.VMEM((D, F2), BF16), pltpu.VMEM((F, D), BF16)],
    )
    return pl.pallas_call(
        _expert_kernel,
        out_shape=jax.ShapeDtypeStruct((n_blocks * M * ROW_TILE, LANES), F32),
        grid_spec=grid_spec,
        compiler_params=_cparams(("arbitrary",), VMEM_LIMIT_BIG),
        name="experts",
    )(blk_expert, n_used, xs, w1, b1.reshape(E, 1, F2), w2, b2.reshape(E, 1, D))


def _combine_kernel(pos_ref, pos_next_ref, route_ref, x1_ref, g_ref, ys_hbm, o_ref, buf_ref, sem, *, tile):
    step = pl.program_id(0)
    cur = lax.rem(step, 2)

    def gather(p_ref, slot):
        def issue(i, carry):
            for k in range(TOP_K):
                _row_copy(ys_hbm, buf_ref.at[slot, k], p_ref[0, k, i], i * ROW_TILE,
                          sem.at[slot]).start(priority=k % 2)
            return carry

        lax.fori_loop(0, tile, issue, 0, unroll=8)

    @pl.when(step == 0)
    def _():
        gather(pos_ref, 0)

    @pl.when(step + 1 < pl.num_programs(0))
    def _():
        gather(pos_next_ref, 1 - cur)

    def drain(i, carry):
        for k in range(TOP_K):
            _row_copy(ys_hbm, buf_ref.at[cur, k], 0, 0, sem.at[cur]).wait()
        return carry

    lax.fori_loop(0, tile, drain, 0, unroll=8)

    route = route_ref[...]
    acc = x1_ref[...]
    for k in range(TOP_K):
        acc = acc + route[:, ROUTE_GATE + k:ROUTE_GATE + k + 1] * _load_row_tiles(buf_ref.at[cur, k], tile)
    ms = jnp.mean(acc * acc, axis=-1, keepdims=True)
    o_ref[...] = acc * lax.rsqrt(ms + NORM_EPS) * g_ref[...]


def _combine(pos_tiles, route, x1, g_final, ys, tile):
    T, D = x1.shape
    nt = T // tile
    return pl.pallas_call(
        functools.partial(_combine_kernel, tile=tile),
        out_shape=jax.ShapeDtypeStruct((T, D), F32),
        grid=(nt,),
        in_specs=[pl.BlockSpec((1, TOP_K, tile), lambda i: (i, 0, 0), memory_space=pltpu.SMEM),
                  pl.BlockSpec((1, TOP_K, tile), lambda i: (jnp.minimum(i + 1, nt - 1), 0, 0),
                               memory_space=pltpu.SMEM),
                  pl.BlockSpec((tile, LANES), lambda i: (i, 0)),
                  pl.BlockSpec((tile, D), lambda i: (i, 0)),
                  pl.BlockSpec((1, D), lambda i: (0, 0)),
                  pl.BlockSpec(memory_space=pl.ANY)],
        out_specs=pl.BlockSpec((tile, D), lambda i: (i, 0)),
        scratch_shapes=[pltpu.VMEM((2, TOP_K, tile * ROW_TILE, LANES), F32), pltpu.SemaphoreType.DMA((2,))],
        compiler_params=_cparams(("arbitrary",)),
        name="combine",
    )(pos_tiles, pos_tiles, route, x1, g_final.reshape(1, D), ys)


def _layer(x2d, B, S, norm_mix_g, w_in, rel_bias, w_att, w_ret, w_out, norm_ffn_g, w_router, b_router,
           w1, b1, w2, b2):
    T, D = x2d.shape
    proj, a4, a16 = _inproj(x2d, norm_mix_g, w_in.astype(BF16), B, S)

    group_bias = [_band_bias(rel_bias[:, g * HEADS_PER_GROUP:(g + 1) * HEADS_PER_GROUP], d)
                  for g, d in enumerate(DILATIONS)]
    per_step = 8

    def attend(qkv, bias):
        blocks = min(qkv.shape[2] // N_STEPS, per_step)
        residues = min(qkv.shape[1], max(per_step // blocks, 1))
        return _attention_group(qkv, bias, blocks, residues)

    o1, l1 = attend(proj.reshape(B, 1, S, proj.shape[1]), group_bias[0])
    o2, l2 = attend(a4, group_bias[1])
    o3, l3 = attend(a16, group_bias[2])

    ret = _retention(proj, B, S, MAIN_RET_COL)

    w_router_p = jnp.zeros((D, LANES), BF16).at[:, :N_EXPERTS].set(w_router.astype(BF16))
    b_router_p = jnp.full((1, LANES), NEG_BIG, F32).at[0, :N_EXPERTS].set(b_router.astype(F32))
    x1, hf, route, counts = _merge(x2d, B, S, o1.reshape(T, GROUP_WIDTH), l1.reshape(T, LANES), o2, l2, o3, l3,
                                   ret, proj, MAIN_GATE_COL, w_att.astype(BF16), w_ret.astype(BF16),
                                   w_out.astype(BF16), norm_ffn_g.reshape(1, D), w_router_p, b_router_p)

    M = EXPERT_BLOCK
    n_blocks = (T * TOP_K) // M + N_EXPERTS
    cnt = counts[0, :N_EXPERTS].astype(jnp.int32)
    padded = ((cnt + M - 1) // M) * M
    pend = jnp.cumsum(padded)
    pstart = pend - padded
    idx = route[:, ROUTE_IDX:ROUTE_IDX + TOP_K].astype(jnp.int32)
    rank = route[:, ROUTE_RANK:ROUTE_RANK + TOP_K].astype(jnp.int32)
    experts = jnp.arange(N_EXPERTS, dtype=jnp.int32)
    pos = rank + jnp.sum(jnp.where(idx[..., None] == experts, pstart, 0), axis=-1)
    n_used = (pend[-1] // M).astype(jnp.int32)
    blk_row = jnp.minimum(jnp.arange(n_blocks, dtype=jnp.int32), jnp.maximum(n_used - 1, 0)) * M
    blk_expert = jnp.minimum(jnp.sum((pend[None, :] <= blk_row[:, None]).astype(jnp.int32), axis=-1),
                             N_EXPERTS - 1)

    assert D == ROW_TILE * LANES
    tile = ROW_MOVE_TILE
    pos_tiles = (pos * ROW_TILE).reshape(T // tile, tile, TOP_K).transpose(0, 2, 1)
    pad_info = jnp.concatenate([(pstart + cnt) * ROW_TILE, padded - cnt]).astype(jnp.int32)
    xs = _dispatch(hf, pos_tiles, pad_info, n_blocks * M, tile)
    ys = _experts(xs, n_blocks, blk_expert, n_used.reshape(1), w1, b1, w2, b2)
    return pos_tiles, route, x1, ys


def kernel(x, norm_mix_g, w_in, rel_bias, w_att_branch, w_ret_branch, w_out, norm_ffn_g, w_router, b_router,
           w1, b1, w2, b2, norm_final_g):
    B, S, D = x.shape
    depth = w_in.shape[0]
    assert depth == 1, "the combine stage applies the final norm; a deeper stack needs a separate norm pass"
    x2d = x.reshape(B * S, D)
    pos_tiles, route, x1, ys = _layer(x2d, B, S, norm_mix_g[0], w_in[0], rel_bias, w_att_branch[0],
                                      w_ret_branch[0], w_out[0], norm_ffn_g[0], w_router[0], b_router[0],
                                      w1[0], b1[0], w2[0], b2[0])
    out = _combine(pos_tiles, route, x1, norm_final_g, ys, tile=pos_tiles.shape[2])
    return out.reshape(B, S, D)
```

```python
import functools
import math

import numpy as np
import jax
import jax.numpy as jnp
from jax import lax
from jax.experimental import pallas as pl
from jax.experimental.pallas import tpu as pltpu

F32 = jnp.float32
BF16 = jnp.bfloat16

NORM_EPS = 1e-5

HEAD_DIM = 128
HEADS_PER_GROUP = 4
GROUP_WIDTH = HEADS_PER_GROUP * HEAD_DIM
DILATIONS = (1, 4, 16)
N_STEPS = 128
N_GROUPS = len(DILATIONS)
REL_BUCKETS = 32
REL_MAX_DIST = 2048

RET_HEADS = 4
RET_DK = 128
RET_DV = 256
RET_CHUNK = 128
ROPE_BASE = 10000.0

N_EXPERTS = 32
TOP_K = 4
SWIGLU_LIMIT = 7.0
SWIGLU_ALPHA = 1.702
EXPERT_BLOCK = 512
ROW_MOVE_TILE = 512
ISSUE_UNROLL = 16

LANES = 128
COL_BLOCK = 512
NEG_BIG = -1e30

VMEM_LIMIT = 48 * 1024 * 1024
VMEM_LIMIT_BIG = 56 * 1024 * 1024

N_ATT_CHUNKS = 3 * N_GROUPS
MAIN_RET_COL = 3
MAIN_GATE_COL = MAIN_RET_COL + (2 * RET_HEADS * RET_DK + 2 * RET_HEADS * RET_DV) // COL_BLOCK


def _cparams(sem, vmem=VMEM_LIMIT):
    return pltpu.CompilerParams(dimension_semantics=sem, vmem_limit_bytes=vmem)


def _inproj_kernel(x_ref, g_ref, w_ref, p4_ref, p16_ref, main_ref, a4_ref, a16_ref, h_ref, hp_ref):
    tm = x_ref.shape[0]
    x = x_ref[...]
    ms = jnp.mean(x * x, axis=-1, keepdims=True)
    h_ref[...] = (x * lax.rsqrt(ms + NORM_EPS) * g_ref[...]).astype(BF16)
    n_chunks = w_ref.shape[1] // COL_BLOCK

    def chunk(lhs_ref, c):
        return jnp.dot(lhs_ref[...], w_ref[:, c * COL_BLOCK:(c + 1) * COL_BLOCK], preferred_element_type=F32)

    for c in range(N_ATT_CHUNKS, n_chunks):
        mc = c - N_ATT_CHUNKS + MAIN_RET_COL
        main_ref[:, mc * COL_BLOCK:(mc + 1) * COL_BLOCK] = chunk(h_ref, c).astype(BF16)
    for part in range(3):
        main_ref[:, part * COL_BLOCK:(part + 1) * COL_BLOCK] = chunk(h_ref, part * N_GROUPS).astype(BF16)
    for group, (perm_ref, dest) in ((1, (p4_ref, a4_ref)), (2, (p16_ref, a16_ref))):
        d = DILATIONS[group]
        n = tm // d
        hp_ref[...] = jnp.dot(perm_ref[...], h_ref[...], preferred_element_type=F32).astype(BF16)
        for part in range(3):
            res = chunk(hp_ref, part * N_GROUPS + group)
            for r in range(d):
                dest[r, :, part * COL_BLOCK:(part + 1) * COL_BLOCK] = res[r * n:(r + 1) * n, :].astype(BF16)


def _residue_permutation(tm, d):
    n = tm // d
    p = np.zeros((tm, tm), np.float32)
    m, r = np.meshgrid(np.arange(n), np.arange(d), indexing="ij")
    p[(r * n + m).ravel(), (m * d + r).ravel()] = 1.0
    return jnp.asarray(p, BF16)


def _inproj(x2d, g, w_bf16, B, S, tm=512):
    T, D = x2d.shape
    n_main = (w_bf16.shape[1] // COL_BLOCK - N_ATT_CHUNKS + 3) * COL_BLOCK
    d4, d16 = DILATIONS[1], DILATIONS[2]
    per_b = S // tm
    return pl.pallas_call(
        _inproj_kernel,
        out_shape=(jax.ShapeDtypeStruct((T, n_main), BF16),
                   jax.ShapeDtypeStruct((B, d4, S // d4, 3 * GROUP_WIDTH), BF16),
                   jax.ShapeDtypeStruct((B, d16, S // d16, 3 * GROUP_WIDTH), BF16)),
        grid=(T // tm,),
        in_specs=[
            pl.BlockSpec((tm, D), lambda i: (i, 0)),
            pl.BlockSpec((1, D), lambda i: (0, 0)),
            pl.BlockSpec(w_bf16.shape, lambda i: (0, 0), pipeline_mode=pl.Buffered(1)),
            pl.BlockSpec((tm, tm), lambda i: (0, 0), pipeline_mode=pl.Buffered(1)),
            pl.BlockSpec((tm, tm), lambda i: (0, 0), pipeline_mode=pl.Buffered(1)),
        ],
        out_specs=(pl.BlockSpec((tm, n_main), lambda i: (i, 0)),
                   pl.BlockSpec((None, d4, tm // d4, 3 * GROUP_WIDTH), lambda i: (i // per_b, 0, i % per_b, 0)),
                   pl.BlockSpec((None, d16, tm // d16, 3 * GROUP_WIDTH), lambda i: (i // per_b, 0, i % per_b, 0))),
        scratch_shapes=[pltpu.VMEM((tm, D), BF16), pltpu.VMEM((tm, D), BF16)],
        compiler_params=_cparams(("arbitrary",), VMEM_LIMIT_BIG),
        name="inproj",
    )(x2d, g.reshape(1, D), w_bf16, _residue_permutation(tm, d4), _residue_permutation(tm, d16))


def _attn_block(q, kp, kc, vp, vc, bias_ref, has_prev):
    scale = HEAD_DIM ** -0.5
    heads = range(HEADS_PER_GROUP)
    hs = [slice(h * HEAD_DIM, (h + 1) * HEAD_DIM) for h in heads]
    rs = [slice(h * N_STEPS, (h + 1) * N_STEPS) for h in heads]
    k = jnp.concatenate([kp, kc], axis=0)
    v = jnp.concatenate([vp, vc], axis=0)
    dn = (((1,), (1,)), ((), ()))
    s = jnp.concatenate([lax.dot_general(q[:, hs[h]], k[:, hs[h]], dn, preferred_element_type=F32)
                         for h in heads], axis=0)
    s = s * scale + bias_ref[...]
    if has_prev is not True:
        col = lax.broadcasted_iota(jnp.int32, s.shape, 1)
        s = jnp.where((col >= N_STEPS) | has_prev, s, NEG_BIG)
    m = jnp.max(s, axis=-1, keepdims=True)
    p = jnp.exp(s - m)
    l = jnp.sum(p, axis=-1, keepdims=True)
    pb = p.astype(BF16)
    lse_rows = m + jnp.log(l)
    lane = lax.broadcasted_iota(jnp.int32, (N_STEPS, LANES), 1)
    outs = []
    lse = jnp.zeros((N_STEPS, LANES), F32)
    for h in heads:
        acc = jnp.dot(pb[rs[h], :], v[:, hs[h]], preferred_element_type=F32)
        outs.append(acc / l[rs[h], :])
        lse = jnp.where(lane == h, lse_rows[rs[h], :], lse)
    return jnp.concatenate(outs, axis=-1), lse


def _attn_kernel(*refs, blocks, residues, with_prev):
    if with_prev:
        q_ref, k_ref, v_ref, kprev_ref, vprev_ref, bias_ref, o_ref, lse_ref = refs
    else:
        q_ref, k_ref, v_ref, bias_ref, o_ref, lse_ref = refs
    n = pl.program_id(2)

    def run(rr, j, kp, vp, has_prev):
        rows = slice(j * N_STEPS, (j + 1) * N_STEPS)
        o, lse = _attn_block(q_ref[rr, rows, :], kp, k_ref[rr, rows, :], vp, v_ref[rr, rows, :], bias_ref,
                             has_prev)
        o_ref[rr, rows, :] = o.astype(o_ref.dtype)
        lse_ref[rr, rows, :] = lse

    for rr in range(residues):
        if with_prev:
            run(rr, 0, kprev_ref[rr], vprev_ref[rr], n > 0)
        else:
            run(rr, 0, k_ref[rr, :N_STEPS, :], v_ref[rr, :N_STEPS, :], False)
        for j in range(1, blocks):
            prev = slice((j - 1) * N_STEPS, j * N_STEPS)
            run(rr, j, k_ref[rr, prev, :], v_ref[rr, prev, :], True)


def _attention_group(qkv, bias, blocks, residues):
    B, d, L, _ = qkv.shape
    W = GROUP_WIDTH
    rows = blocks * N_STEPS
    steps = L // rows
    with_prev = steps > 1

    def cur(c):
        return pl.BlockSpec((None, residues, rows, W), lambda b, r, n: (b, r, n, c))

    def prev(c):
        return pl.BlockSpec((None, residues, N_STEPS, W),
                            lambda b, r, n: (b, r, jnp.maximum(n * blocks - 1, 0), c))

    in_specs = [cur(0), cur(1), cur(2)] + ([prev(1), prev(2)] if with_prev else [])
    in_specs.append(pl.BlockSpec((HEADS_PER_GROUP * N_STEPS, 2 * N_STEPS), lambda b, r, n: (0, 0)))
    operands = [qkv] * (5 if with_prev else 3) + [bias]
    return pl.pallas_call(
        functools.partial(_attn_kernel, blocks=blocks, residues=residues, with_prev=with_prev),
        out_shape=(jax.ShapeDtypeStruct((B, d, L, W), BF16), jax.ShapeDtypeStruct((B, d, L, LANES), F32)),
        grid=(B, d // residues, steps),
        in_specs=in_specs,
        out_specs=(pl.BlockSpec((None, residues, rows, W), lambda b, r, n: (b, r, n, 0)),
                   pl.BlockSpec((None, residues, rows, LANES), lambda b, r, n: (b, r, n, 0))),
        compiler_params=_cparams(("arbitrary", "arbitrary", "arbitrary")),
        name=f"attn_d{d}",
    )(*operands)


def _t5_bucket(dist):
    max_exact = REL_BUCKETS // 2
    d_f = jnp.maximum(dist, 1).astype(F32)
    large = max_exact + (jnp.log(d_f / max_exact) / math.log(REL_MAX_DIST / max_exact)
                         * (REL_BUCKETS - max_exact)).astype(jnp.int32)
    large = jnp.minimum(large, REL_BUCKETS - 1)
    return jnp.where(dist < max_exact, dist, large)


def _band_bias(rel_bias_g, dilation):
    qi = jnp.arange(N_STEPS)[:, None]
    kj = jnp.arange(2 * N_STEPS)[None, :]
    step_dist = qi + N_STEPS - kj
    band = (step_dist >= 0) & (step_dist <= N_STEPS)
    bucket = _t5_bucket(jnp.maximum(step_dist, 0) * dilation)
    table = rel_bias_g.astype(F32).T
    hit = bucket[None, None] == jnp.arange(REL_BUCKETS)[None, :, None, None]
    bias = jnp.sum(jnp.where(hit, table[:, :, None, None], 0.0), axis=1)
    return jnp.where(band[None], bias, NEG_BIG).reshape(HEADS_PER_GROUP * N_STEPS, 2 * N_STEPS)


def _ret_kernel(q_ref, k_ref, v0_ref, v1_ref, g0_ref, g1_ref, cos_ref, sin_ref, decay_ref, xi_ref, zeta_ref,
                o_ref, state_ref, *, g_chunk):
    n = pl.program_id(1)

    @pl.when(n == 0)
    def _():
        state_ref[...] = jnp.zeros_like(state_ref)

    C = RET_CHUNK
    n_sub = q_ref.shape[0] // C
    lane = lax.broadcasted_iota(jnp.int32, (C, RET_DK), 1)
    even = (lane % 2) == 0

    def rotary(t, rows):
        partner = jnp.where(even, pltpu.roll(t, RET_DK - 1, 1), pltpu.roll(t, 1, 1))
        return t * cos_ref[rows, :] + partner * sin_ref[rows, :]

    heads = range(RET_HEADS)
    ks = [slice(h * RET_DK, (h + 1) * RET_DK) for h in heads]
    vs = [slice((h % 2) * RET_DV, (h % 2 + 1) * RET_DV) for h in heads]
    v_refs = [v0_ref if h < 2 else v1_ref for h in heads]
    g_refs = [g0_ref if h < 2 else g1_ref for h in heads]
    states = [state_ref[h] for h in heads]
    for sub in range(n_sub):
        rows = slice(sub * C, (sub + 1) * C)
        qbs, kzs, inners, vals = [], [], [], []
        for h in heads:
            qr = rotary(q_ref[rows, ks[h]].astype(F32), rows)
            kr = rotary(k_ref[rows, ks[h]].astype(F32), rows) * (RET_DK ** -0.5)
            v = v_refs[h][rows, vs[h]]
            qb = qr.astype(BF16)
            scores = lax.dot_general(qb, kr.astype(BF16), (((1,), (1,)), ((), ())),
                                     preferred_element_type=F32) * decay_ref[h]
            inners.append(jnp.dot(scores.astype(BF16), v, preferred_element_type=F32))
            qbs.append(qb)
            kzs.append((kr * zeta_ref[h]).astype(BF16))
            vals.append(v)
        for h in heads:
            cross = jnp.dot(qbs[h], states[h].astype(BF16), preferred_element_type=F32) * xi_ref[h]
            states[h] = states[h] * g_chunk[h] + lax.dot_general(kzs[h], vals[h], (((0,), (0,)), ((), ())),
                                                                 preferred_element_type=F32)
            ret = inners[h] + cross
            mu = jnp.mean(ret, axis=-1, keepdims=True)
            cen = ret - mu
            var = jnp.mean(cen * cen, axis=-1, keepdims=True)
            normed = cen * lax.rsqrt(var + NORM_EPS)
            g = g_refs[h][rows, vs[h]].astype(F32)
            o_ref[rows, h * RET_DV:(h + 1) * RET_DV] = (g * jax.nn.sigmoid(g) * normed).astype(o_ref.dtype)
    for h in heads:
        state_ref[h] = states[h]


def _retention(proj, B, S, col0):
    T = B * S
    nc = S // RET_CHUNK
    C = RET_CHUNK
    log_g = np.log(1.0 - 2.0 ** (-5.0 - np.arange(RET_HEADS, dtype=np.float64)))
    idx = np.arange(C, dtype=np.float64)
    diff = idx[:, None] - idx[None, :]
    decay = np.where(diff >= 0, np.exp(np.maximum(diff, 0.0)[None] * log_g[:, None, None]), 0.0)
    xi = np.exp((idx + 1.0)[None, :] * log_g[:, None])
    zeta = np.exp((C - 1.0 - idx)[None, :] * log_g[:, None])
    g_chunk = tuple(float(v) for v in np.exp(C * log_g))
    xi_b = np.broadcast_to(xi[:, :, None], (RET_HEADS, C, RET_DV)).astype(np.float32)
    zeta_b = np.broadcast_to(zeta[:, :, None], (RET_HEADS, C, RET_DK)).astype(np.float32)

    inv = ROPE_BASE ** (-np.arange(0, RET_DK, 2, dtype=np.float64) / RET_DK)
    ang = np.arange(S, dtype=np.float64)[:, None] * inv[None]
    cos_t = np.repeat(np.cos(ang), 2, axis=1).astype(np.float32)
    sin_t = np.stack([-np.sin(ang), np.sin(ang)], axis=-1).reshape(S, RET_DK).astype(np.float32)

    rows = 2 * C if nc % 2 == 0 else C
    ns = S // rows

    def col(c):
        return pl.BlockSpec((rows, COL_BLOCK), lambda b, n, c=c: (b * ns + n, c))

    const3 = lambda shape: pl.BlockSpec(shape, lambda b, n: (0, 0, 0))
    return pl.pallas_call(
        functools.partial(_ret_kernel, g_chunk=g_chunk),
        out_shape=jax.ShapeDtypeStruct((T, RET_HEADS * RET_DV), BF16),
        grid=(B, ns),
        in_specs=[col(col0), col(col0 + 1), col(col0 + 2), col(col0 + 3), col(col0 + 4), col(col0 + 5),
                  pl.BlockSpec((rows, RET_DK), lambda b, n: (n, 0)),
                  pl.BlockSpec((rows, RET_DK), lambda b, n: (n, 0)),
                  const3((RET_HEADS, C, C)), const3((RET_HEADS, C, RET_DV)), const3((RET_HEADS, C, RET_DK))],
        out_specs=pl.BlockSpec((rows, RET_HEADS * RET_DV), lambda b, n: (b * ns + n, 0)),
        scratch_shapes=[pltpu.VMEM((RET_HEADS, RET_DK, RET_DV), F32)],
        compiler_params=_cparams(("arbitrary", "arbitrary")),
        name="retention",
    )(proj, proj, proj, proj, proj, proj, jnp.asarray(cos_t), jnp.asarray(sin_t),
      jnp.asarray(decay.astype(np.float32)), jnp.asarray(xi_b), jnp.asarray(zeta_b))


ROUTE_IDX, ROUTE_RANK, ROUTE_GATE = 0, 4, 8


ROW_TILE = 8


def _store_row_tiles(dst_ref, val):
    n = val.shape[0]
    for c in range(ROW_TILE):
        dst_ref[pl.ds(c, n, stride=ROW_TILE), :] = val[:, c * LANES:(c + 1) * LANES]


def _load_row_tiles(src_ref, n):
    return jnp.concatenate([src_ref[pl.ds(c, n, stride=ROW_TILE), :] for c in range(ROW_TILE)], axis=-1)


def _to_token_order(src_ref, dst_ref, d):
    n = src_ref.shape[1]
    for r in range(d):
        blk = src_ref[r].astype(F32)
        for s in range(dst_ref.shape[0]):
            dst_ref[s, pl.ds(r, n, stride=d), :] = blk[:, s * LANES:(s + 1) * LANES]


def _merge_kernel(x_ref, o1_ref, o2_ref, o3_ref, l1_ref, l2_ref, l3_ref, ret_ref,
                  ga0_ref, ga1_ref, gr0_ref, gr1_ref, watt_ref, wret_ref, wout_ref, gffn_ref, wr_ref, br_ref,
                  below_ref, x1_ref, hf_ref, route_ref, counts_ref, on2_ref, on3_ref, ln2_ref, ln3_ref):
    tm = x_ref.shape[0]
    _to_token_order(o2_ref, on2_ref, DILATIONS[1])
    _to_token_order(o3_ref, on3_ref, DILATIONS[2])
    _to_token_order(l2_ref, ln2_ref, DILATIONS[1])
    _to_token_order(l3_ref, ln3_ref, DILATIONS[2])
    lses = [l1_ref[...], ln2_ref[0], ln3_ref[0]]

    att_parts = []
    for h in range(HEADS_PER_GROUP):
        ls = [l[:, h:h + 1] for l in lses]
        m = jnp.maximum(jnp.maximum(ls[0], ls[1]), ls[2])
        ws = [jnp.exp(l - m) for l in ls]
        wsum = ws[0] + ws[1] + ws[2]
        hs = slice(h * HEAD_DIM, (h + 1) * HEAD_DIM)
        acc = (ws[0] / wsum) * o1_ref[:, hs].astype(F32)
        acc = acc + (ws[1] / wsum) * on2_ref[h]
        acc = acc + (ws[2] / wsum) * on3_ref[h]
        att_parts.append(acc)
    att = jnp.concatenate(att_parts, axis=-1).astype(BF16)

    a_proj = jnp.dot(att, watt_ref[...], preferred_element_type=F32)
    r_proj = jnp.dot(ret_ref[...], wret_ref[...], preferred_element_type=F32)
    gate_a = jnp.concatenate([ga0_ref[...], ga1_ref[...]], axis=-1).astype(F32)
    gate_r = jnp.concatenate([gr0_ref[...], gr1_ref[...]], axis=-1).astype(F32)
    merged = jax.nn.sigmoid(gate_a) * a_proj + jax.nn.sigmoid(gate_r) * r_proj
    x1 = x_ref[...] + jnp.dot(merged.astype(BF16), wout_ref[...], preferred_element_type=F32)
    x1_ref[...] = x1

    ms = jnp.mean(x1 * x1, axis=-1, keepdims=True)
    hf = x1 * lax.rsqrt(ms + NORM_EPS) * gffn_ref[...]
    _store_row_tiles(hf_ref, hf)

    logits = jnp.dot(hf.astype(BF16), wr_ref[...], preferred_element_type=F32) + br_ref[...]
    lane = lax.broadcasted_iota(jnp.int32, (tm, LANES), 1)
    lane_f = lane.astype(F32)
    work = logits
    vals, idxs, hits = [], [], []
    for _ in range(TOP_K):
        mk = jnp.max(work, axis=-1, keepdims=True)
        ik = jnp.min(jnp.where(work == mk, lane_f, float(LANES)), axis=-1, keepdims=True)
        hit = lane_f == ik
        work = jnp.where(hit, -jnp.inf, work)
        vals.append(mk)
        idxs.append(ik)
        hits.append(hit)
    es = [jnp.exp(v - vals[0]) for v in vals]
    esum = es[0] + es[1] + es[2] + es[3]
    gates = [e / esum for e in es]

    sel = jnp.zeros((tm, LANES), F32)
    for hit in hits:
        sel = jnp.where(hit, 1.0, sel)

    @pl.when(pl.program_id(0) == 0)
    def _():
        counts_ref[...] = jnp.zeros_like(counts_ref)

    rank = jnp.dot(below_ref[...], sel.astype(BF16), preferred_element_type=F32) + counts_ref[...]
    counts_ref[...] = counts_ref[...] + jnp.sum(sel, axis=0, keepdims=True)

    route = jnp.zeros((tm, LANES), F32)
    for k in range(TOP_K):
        rank_k = jnp.sum(jnp.where(hits[k], rank, 0.0), axis=-1, keepdims=True)
        route = jnp.where(lane == ROUTE_IDX + k, idxs[k], route)
        route = jnp.where(lane == ROUTE_RANK + k, rank_k, route)
        route = jnp.where(lane == ROUTE_GATE + k, gates[k], route)
    route_ref[...] = route


def _merge(x2d, B, S, o1, l1, o2, l2, o3, l3, ret, proj, gate_col0, w_att, w_ret, w_out, g_ffn, w_router_p,
           b_router_p, tm=512):
    T, D = x2d.shape
    per_b = S // tm
    d4, d16 = DILATIONS[1], DILATIONS[2]
    row = lambda w: pl.BlockSpec((tm, w), lambda i: (i, 0))
    full = lambda a: pl.BlockSpec(a.shape, lambda i: (0,) * a.ndim)
    gcol = lambda c: pl.BlockSpec((tm, COL_BLOCK), lambda i, c=c: (i, c))
    res = lambda d, w: pl.BlockSpec((None, d, tm // d, w), lambda i: (i // per_b, 0, i % per_b, 0))
    below = jnp.tril(jnp.ones((tm, tm), BF16), k=-1)
    slabs = lambda n: pltpu.VMEM((n, tm, LANES), F32)
    return pl.pallas_call(
        _merge_kernel,
        out_shape=(jax.ShapeDtypeStruct((T, D), F32), jax.ShapeDtypeStruct((T * ROW_TILE, LANES), F32),
                   jax.ShapeDtypeStruct((T, LANES), F32), jax.ShapeDtypeStruct((1, LANES), F32)),
        grid=(T // tm,),
        in_specs=[row(D), row(GROUP_WIDTH), res(d4, GROUP_WIDTH), res(d16, GROUP_WIDTH),
                  row(LANES), res(d4, LANES), res(d16, LANES),
                  row(RET_HEADS * RET_DV), gcol(gate_col0), gcol(gate_col0 + 1), gcol(gate_col0 + 2),
                  gcol(gate_col0 + 3), full(w_att), full(w_ret), full(w_out), full(g_ffn), full(w_router_p),
                  full(b_router_p), full(below)],
        out_specs=(row(D), pl.BlockSpec((tm * ROW_TILE, LANES), lambda i: (i, 0)), row(LANES),
                   pl.BlockSpec((1, LANES), lambda i: (0, 0))),
        scratch_shapes=[slabs(GROUP_WIDTH // LANES), slabs(GROUP_WIDTH // LANES), slabs(1), slabs(1)],
        compiler_params=_cparams(("arbitrary",), VMEM_LIMIT_BIG),
        name="merge",
    )(x2d, o1, o2, o3, l1, l2, l3, ret, proj, proj, proj, proj, w_att, w_ret, w_out, g_ffn, w_router_p,
      b_router_p, below)


def _row_copy(src, dst, src_row8, dst_row8, sem):
    return pltpu.make_async_copy(src.at[pl.ds(pl.multiple_of(src_row8, ROW_TILE), ROW_TILE)],
                                 dst.at[pl.ds(pl.multiple_of(dst_row8, ROW_TILE), ROW_TILE)], sem)


def _dispatch_kernel(pad_ref, pos_ref, hf_ref, xs_hbm, zero_ref, sem, *, tile):
    @pl.when(pl.program_id(0) == 0)
    def _():
        zero_ref[...] = jnp.zeros_like(zero_ref)
        total = 0
        for e in range(N_EXPERTS):
            first = pad_ref[e]

            def zero_row(j, carry, first=first):
                _row_copy(zero_ref, xs_hbm, 0, first + j * ROW_TILE, sem).start()
                return carry

            lax.fori_loop(0, pad_ref[N_EXPERTS + e], zero_row, 0)
            total = total + pad_ref[N_EXPERTS + e]

        def zero_wait(n, carry):
            _row_copy(zero_ref, xs_hbm, 0, 0, sem).wait()
            return carry

        lax.fori_loop(0, total, zero_wait, 0)

    def issue(i, carry):
        for k in range(TOP_K):
            _row_copy(hf_ref, xs_hbm, i * ROW_TILE, pos_ref[0, k, i], sem).start(priority=k % 2)
        return carry

    lax.fori_loop(0, tile, issue, 0, unroll=ISSUE_UNROLL)

    def drain(i, carry):
        for k in range(TOP_K):
            _row_copy(hf_ref, xs_hbm, 0, 0, sem).wait()
        return carry

    lax.fori_loop(0, tile, drain, 0, unroll=8)


def _dispatch(hf, pos8_tiles, pad_info, n_rows, tile):
    T = hf.shape[0] // ROW_TILE
    grid_spec = pltpu.PrefetchScalarGridSpec(
        num_scalar_prefetch=1,
        grid=(T // tile,),
        in_specs=[pl.BlockSpec((1, TOP_K, tile), lambda i, pad: (i, 0, 0), memory_space=pltpu.SMEM),
                  pl.BlockSpec((tile * ROW_TILE, LANES), lambda i, pad: (i, 0))],
        out_specs=pl.BlockSpec(memory_space=pl.ANY),
        scratch_shapes=[pltpu.VMEM((ROW_TILE, LANES), hf.dtype), pltpu.SemaphoreType.DMA(())],
    )
    return pl.pallas_call(
        functools.partial(_dispatch_kernel, tile=tile),
        out_shape=jax.ShapeDtypeStruct((n_rows * ROW_TILE, LANES), hf.dtype),
        grid_spec=grid_spec,
        compiler_params=_cparams(("arbitrary",)),
        name="dispatch",
    )(pad_info, pos8_tiles, hf)


def _expert_kernel(be_ref, nu_ref, x_ref, w1_ref, b1_ref, w2_ref, b2_ref, y_ref, w1b_ref, w2b_ref):
    i = pl.program_id(0)
    F = w2_ref.shape[0]

    @pl.when(i < nu_ref[0])
    def _():
        @pl.when((i == 0) | (be_ref[i] != be_ref[jnp.maximum(i - 1, 0)]))
        def _():
            w1b_ref[...] = w1_ref[...].astype(BF16)
            w2b_ref[...] = w2_ref[...].astype(BF16)

        x = _load_row_tiles(x_ref, EXPERT_BLOCK).astype(BF16)
        gu = jnp.dot(x, w1b_ref[...], preferred_element_type=F32) + b1_ref[...]
        gate = jnp.minimum(gu[:, :F], SWIGLU_LIMIT)
        up = jnp.clip(gu[:, F:], -SWIGLU_LIMIT, SWIGLU_LIMIT)
        act = (up + 1.0) * gate * jax.nn.sigmoid(SWIGLU_ALPHA * gate)
        _store_row_tiles(y_ref, jnp.dot(act.astype(BF16), w2b_ref[...], preferred_element_type=F32) + b2_ref[...])


def _experts(xs, n_blocks, blk_expert, n_used, w1, b1, w2, b2):
    E, D, F2 = w1.shape
    F = w2.shape[1]
    M = EXPERT_BLOCK
    blk = lambda i, be, nu: (jnp.minimum(i, nu[0] - 1), 0)
    grid_spec = pltpu.PrefetchScalarGridSpec(
        num_scalar_prefetch=2,
        grid=(n_blocks,),
        in_specs=[pl.BlockSpec((M * ROW_TILE, LANES), blk),
                  pl.BlockSpec((None, D, F2), lambda i, be, nu: (be[i], 0, 0)),
                  pl.BlockSpec((None, 1, F2), lambda i, be, nu: (be[i], 0, 0)),
                  pl.BlockSpec((None, F, D), lambda i, be, nu: (be[i], 0, 0)),
                  pl.BlockSpec((None, 1, D), lambda i, be, nu: (be[i], 0, 0))],
        out_specs=pl.BlockSpec((M * ROW_TILE, LANES), blk),
        scratch_shapes=[pltpu.VMEM((D, F2), BF16), pltpu.VMEM((F, D), BF16)],
    )
    return pl.pallas_call(
        _expert_kernel,
        out_shape=jax.ShapeDtypeStruct((n_blocks * M * ROW_TILE, LANES), F32),
        grid_spec=grid_spec,
        compiler_params=_cparams(("arbitrary",), VMEM_LIMIT_BIG),
        name="experts",
    )(blk_expert, n_used, xs, w1, b1.reshape(E, 1, F2), w2, b2.reshape(E, 1, D))


def _combine_kernel(pos_ref, pos_next_ref, route_ref, x1_ref, g_ref, ys_hbm, o_ref, buf_ref, sem, *, tile):
    step = pl.program_id(0)
    cur = lax.rem(step, 2)

    def gather(p_ref, slot):
        def issue(i, carry):
            for k in range(TOP_K):
                _row_copy(ys_hbm, buf_ref.at[slot, k], p_ref[0, k, i], i * ROW_TILE,
                          sem.at[slot]).start(priority=k % 2)
            return carry

        lax.fori_loop(0, tile, issue, 0, unroll=ISSUE_UNROLL)

    nxt = 1 - cur

    @pl.when(step == 0)
    def _():
        gather(pos_ref, 0)

    def drain(slot):
        def body(i, carry):
            for k in range(TOP_K):
                _row_copy(ys_hbm, buf_ref.at[slot, k], 0, 0, sem.at[slot]).wait()
            return carry

        lax.fori_loop(0, tile, body, 0, unroll=8)

    drain(cur)

    G = ISSUE_UNROLL

    def body(j, carry):
        r0 = pl.multiple_of(j * G, G)
        rows = pl.ds(r0, G)
        route = route_ref[rows, :]
        acc = x1_ref[rows, :]
        for k in range(TOP_K):
            y = jnp.concatenate([buf_ref[cur, k, pl.ds(r0 * ROW_TILE + c, G, stride=ROW_TILE), :]
                                 for c in range(ROW_TILE)], axis=-1)
            acc = acc + route[:, ROUTE_GATE + k:ROUTE_GATE + k + 1] * y
        ms = jnp.mean(acc * acc, axis=-1, keepdims=True)
        o_ref[rows, :] = acc * lax.rsqrt(ms + NORM_EPS) * g_ref[...]
        for t in range(G):
            i = r0 + t
            for k in range(TOP_K):
                _row_copy(ys_hbm, buf_ref.at[nxt, k], pos_next_ref[0, k, i], i * ROW_TILE,
                          sem.at[nxt]).start(priority=k % 2)
        return carry

    lax.fori_loop(0, tile // G, body, 0)

    @pl.when(step + 1 == pl.num_programs(0))
    def _():
        drain(nxt)


def _combine(pos_tiles, route, x1, g_final, ys, tile):
    T, D = x1.shape
    nt = T // tile
    return pl.pallas_call(
        functools.partial(_combine_kernel, tile=tile),
        out_shape=jax.ShapeDtypeStruct((T, D), F32),
        grid=(nt,),
        in_specs=[pl.BlockSpec((1, TOP_K, tile), lambda i: (i, 0, 0), memory_space=pltpu.SMEM),
                  pl.BlockSpec((1, TOP_K, tile), lambda i: (jnp.minimum(i + 1, nt - 1), 0, 0),
                               memory_space=pltpu.SMEM),
                  pl.BlockSpec((tile, LANES), lambda i: (i, 0)),
                  pl.BlockSpec((tile, D), lambda i: (i, 0)),
                  pl.BlockSpec((1, D), lambda i: (0, 0)),
                  pl.BlockSpec(memory_space=pl.ANY)],
        out_specs=pl.BlockSpec((tile, D), lambda i: (i, 0)),
        scratch_shapes=[pltpu.VMEM((2, TOP_K, tile * ROW_TILE, LANES), F32), pltpu.SemaphoreType.DMA((2,))],
        compiler_params=_cparams(("arbitrary",)),
        name="combine",
    )(pos_tiles, pos_tiles, route, x1, g_final.reshape(1, D), ys)


def _layer(x2d, B, S, norm_mix_g, w_in, rel_bias, w_att, w_ret, w_out, norm_ffn_g, w_router, b_router,
           w1, b1, w2, b2):
    T, D = x2d.shape
    proj, a4, a16 = _inproj(x2d, norm_mix_g, w_in.astype(BF16), B, S)

    group_bias = [_band_bias(rel_bias[:, g * HEADS_PER_GROUP:(g + 1) * HEADS_PER_GROUP], d)
                  for g, d in enumerate(DILATIONS)]
    per_step = 8

    def attend(qkv, bias):
        blocks = min(qkv.shape[2] // N_STEPS, per_step)
        residues = min(qkv.shape[1], max(per_step // blocks, 1))
        return _attention_group(qkv, bias, blocks, residues)

    o1, l1 = attend(proj.reshape(B, 1, S, proj.shape[1]), group_bias[0])
    o2, l2 = attend(a4, group_bias[1])
    o3, l3 = attend(a16, group_bias[2])

    ret = _retention(proj, B, S, MAIN_RET_COL)

    w_router_p = jnp.zeros((D, LANES), BF16).at[:, :N_EXPERTS].set(w_router.astype(BF16))
    b_router_p = jnp.full((1, LANES), NEG_BIG, F32).at[0, :N_EXPERTS].set(b_router.astype(F32))
    x1, hf, route, counts = _merge(x2d, B, S, o1.reshape(T, GROUP_WIDTH), l1.reshape(T, LANES), o2, l2, o3, l3,
                                   ret, proj, MAIN_GATE_COL, w_att.astype(BF16), w_ret.astype(BF16),
                                   w_out.astype(BF16), norm_ffn_g.reshape(1, D), w_router_p, b_router_p)

    M = EXPERT_BLOCK
    n_blocks = (T * TOP_K) // M + N_EXPERTS
    cnt = counts[0, :N_EXPERTS].astype(jnp.int32)
    padded = ((cnt + M - 1) // M) * M
    pend = jnp.cumsum(padded)
    pstart = pend - padded
    idx = route[:, ROUTE_IDX:ROUTE_IDX + TOP_K].astype(jnp.int32)
    rank = route[:, ROUTE_RANK:ROUTE_RANK + TOP_K].astype(jnp.int32)
    experts = jnp.arange(N_EXPERTS, dtype=jnp.int32)
    pos = rank + jnp.sum(jnp.where(idx[..., None] == experts, pstart, 0), axis=-1)
    n_used = (pend[-1] // M).astype(jnp.int32)
    blk_row = jnp.minimum(jnp.arange(n_blocks, dtype=jnp.int32), jnp.maximum(n_used - 1, 0)) * M
    blk_expert = jnp.minimum(jnp.sum((pend[None, :] <= blk_row[:, None]).astype(jnp.int32), axis=-1),
                             N_EXPERTS - 1)

    assert D == ROW_TILE * LANES
    tile = ROW_MOVE_TILE
    pos_tiles = (pos * ROW_TILE).reshape(T // tile, tile, TOP_K).transpose(0, 2, 1)
    pad_info = jnp.concatenate([(pstart + cnt) * ROW_TILE, padded - cnt]).astype(jnp.int32)
    xs = _dispatch(hf, pos_tiles, pad_info, n_blocks * M, tile)
    ys = _experts(xs, n_blocks, blk_expert, n_used.reshape(1), w1, b1, w2, b2)
    return pos_tiles, route, x1, ys


def kernel(x, norm_mix_g, w_in, rel_bias, w_att_branch, w_ret_branch, w_out, norm_ffn_g, w_router, b_router,
           w1, b1, w2, b2, norm_final_g):
    B, S, D = x.shape
    depth = w_in.shape[0]
    assert depth == 1, "the combine stage applies the final norm; a deeper stack needs a separate norm pass"
    x2d = x.reshape(B * S, D)
    pos_tiles, route, x1, ys = _layer(x2d, B, S, norm_mix_g[0], w_in[0], rel_bias, w_att_branch[0],
                                      w_ret_branch[0], w_out[0], norm_ffn_g[0], w_router[0], b_router[0],
                                      w1[0], b1[0], w2[0], b2[0])
    out = _combine(pos_tiles, route, x1, norm_final_g, ys, tile=pos_tiles.shape[2])
    return out.reshape(B, S, D)
```

```python
import functools
import math

import numpy as np
import jax
import jax.numpy as jnp
from jax import lax
from jax.experimental import pallas as pl
from jax.experimental.pallas import tpu as pltpu

F32 = jnp.float32
BF16 = jnp.bfloat16

NORM_EPS = 1e-5

HEAD_DIM = 128
HEADS_PER_GROUP = 4
GROUP_WIDTH = HEADS_PER_GROUP * HEAD_DIM
DILATIONS = (1, 4, 16)
N_STEPS = 128
N_GROUPS = len(DILATIONS)
REL_BUCKETS = 32
REL_MAX_DIST = 2048

RET_HEADS = 4
RET_DK = 128
RET_DV = 256
RET_CHUNK = 128
ROPE_BASE = 10000.0

N_EXPERTS = 32
TOP_K = 4
SWIGLU_LIMIT = 7.0
SWIGLU_ALPHA = 1.702
EXPERT_BLOCK = 512
ROW_MOVE_TILE = 512
ISSUE_UNROLL = 16
DRAIN_UNROLL = 8
ATTN_BLOCKS_PER_STEP = 8
RET_CHUNKS_PER_STEP = 2

LANES = 128
COL_BLOCK = 512
NEG_BIG = -1e30

VMEM_LIMIT = 48 * 1024 * 1024
VMEM_LIMIT_BIG = 56 * 1024 * 1024

N_ATT_CHUNKS = 3 * N_GROUPS
MAIN_RET_COL = 3
MAIN_GATE_COL = MAIN_RET_COL + (2 * RET_HEADS * RET_DK + 2 * RET_HEADS * RET_DV) // COL_BLOCK


def _cparams(sem, vmem=VMEM_LIMIT):
    return pltpu.CompilerParams(dimension_semantics=sem, vmem_limit_bytes=vmem)


def _inproj_kernel(x_ref, g_ref, w_ref, p4_ref, p16_ref, main_ref, a4_ref, a16_ref, h_ref, hp_ref):
    tm = x_ref.shape[0]
    x = x_ref[...]
    ms = jnp.mean(x * x, axis=-1, keepdims=True)
    h_ref[...] = (x * lax.rsqrt(ms + NORM_EPS) * g_ref[...]).astype(BF16)
    n_chunks = w_ref.shape[1] // COL_BLOCK

    def chunk(lhs_ref, c):
        return jnp.dot(lhs_ref[...], w_ref[:, c * COL_BLOCK:(c + 1) * COL_BLOCK], preferred_element_type=F32)

    for c in range(N_ATT_CHUNKS, n_chunks):
        mc = c - N_ATT_CHUNKS + MAIN_RET_COL
        main_ref[:, mc * COL_BLOCK:(mc + 1) * COL_BLOCK] = chunk(h_ref, c).astype(BF16)
    for part in range(3):
        main_ref[:, part * COL_BLOCK:(part + 1) * COL_BLOCK] = chunk(h_ref, part * N_GROUPS).astype(BF16)
    for group, (perm_ref, dest) in ((1, (p4_ref, a4_ref)), (2, (p16_ref, a16_ref))):
        d = DILATIONS[group]
        n = tm // d
        hp_ref[...] = jnp.dot(perm_ref[...], h_ref[...], preferred_element_type=F32).astype(BF16)
        for part in range(3):
            res = chunk(hp_ref, part * N_GROUPS + group)
            for r in range(d):
                dest[r, :, part * COL_BLOCK:(part + 1) * COL_BLOCK] = res[r * n:(r + 1) * n, :].astype(BF16)


def _residue_permutation(tm, d):
    n = tm // d
    p = np.zeros((tm, tm), np.float32)
    m, r = np.meshgrid(np.arange(n), np.arange(d), indexing="ij")
    p[(r * n + m).ravel(), (m * d + r).ravel()] = 1.0
    return jnp.asarray(p, BF16)


def _inproj(x2d, g, w_bf16, B, S, tm=512):
    T, D = x2d.shape
    n_main = (w_bf16.shape[1] // COL_BLOCK - N_ATT_CHUNKS + 3) * COL_BLOCK
    d4, d16 = DILATIONS[1], DILATIONS[2]
    per_b = S // tm
    return pl.pallas_call(
        _inproj_kernel,
        out_shape=(jax.ShapeDtypeStruct((T, n_main), BF16),
                   jax.ShapeDtypeStruct((B, d4, S // d4, 3 * GROUP_WIDTH), BF16),
                   jax.ShapeDtypeStruct((B, d16, S // d16, 3 * GROUP_WIDTH), BF16)),
        grid=(T // tm,),
        in_specs=[
            pl.BlockSpec((tm, D), lambda i: (i, 0)),
            pl.BlockSpec((1, D), lambda i: (0, 0)),
            pl.BlockSpec(w_bf16.shape, lambda i: (0, 0), pipeline_mode=pl.Buffered(1)),
            pl.BlockSpec((tm, tm), lambda i: (0, 0), pipeline_mode=pl.Buffered(1)),
            pl.BlockSpec((tm, tm), lambda i: (0, 0), pipeline_mode=pl.Buffered(1)),
        ],
        out_specs=(pl.BlockSpec((tm, n_main), lambda i: (i, 0)),
                   pl.BlockSpec((None, d4, tm // d4, 3 * GROUP_WIDTH), lambda i: (i // per_b, 0, i % per_b, 0)),
                   pl.BlockSpec((None, d16, tm // d16, 3 * GROUP_WIDTH), lambda i: (i // per_b, 0, i % per_b, 0))),
        scratch_shapes=[pltpu.VMEM((tm, D), BF16), pltpu.VMEM((tm, D), BF16)],
        compiler_params=_cparams(("arbitrary",), VMEM_LIMIT_BIG),
        name="inproj",
    )(x2d, g.reshape(1, D), w_bf16, _residue_permutation(tm, d4), _residue_permutation(tm, d16))


def _attn_block(q, kp, kc, vp, vc, bias_ref, has_prev):
    scale = HEAD_DIM ** -0.5
    heads = range(HEADS_PER_GROUP)
    hs = [slice(h * HEAD_DIM, (h + 1) * HEAD_DIM) for h in heads]
    rs = [slice(h * N_STEPS, (h + 1) * N_STEPS) for h in heads]
    k = jnp.concatenate([kp, kc], axis=0)
    v = jnp.concatenate([vp, vc], axis=0)
    dn = (((1,), (1,)), ((), ()))
    s = jnp.concatenate([lax.dot_general(q[:, hs[h]], k[:, hs[h]], dn, preferred_element_type=F32)
                         for h in heads], axis=0)
    s = s * scale + bias_ref[...]
    if has_prev is not True:
        col = lax.broadcasted_iota(jnp.int32, s.shape, 1)
        s = jnp.where((col >= N_STEPS) | has_prev, s, NEG_BIG)
    m = jnp.max(s, axis=-1, keepdims=True)
    p = jnp.exp(s - m)
    l = jnp.sum(p, axis=-1, keepdims=True)
    pb = p.astype(BF16)
    lse_rows = m + jnp.log(l)
    lane = lax.broadcasted_iota(jnp.int32, (N_STEPS, LANES), 1)
    outs = []
    lse = jnp.zeros((N_STEPS, LANES), F32)
    for h in heads:
        acc = jnp.dot(pb[rs[h], :], v[:, hs[h]], preferred_element_type=F32)
        outs.append(acc / l[rs[h], :])
        lse = jnp.where(lane == h, lse_rows[rs[h], :], lse)
    return jnp.concatenate(outs, axis=-1), lse


def _attn_kernel(*refs, blocks, residues, with_prev):
    if with_prev:
        q_ref, k_ref, v_ref, kprev_ref, vprev_ref, bias_ref, o_ref, lse_ref = refs
    else:
        q_ref, k_ref, v_ref, bias_ref, o_ref, lse_ref = refs
    n = pl.program_id(2)

    def run(rr, j, kp, vp, has_prev):
        rows = slice(j * N_STEPS, (j + 1) * N_STEPS)
        o, lse = _attn_block(q_ref[rr, rows, :], kp, k_ref[rr, rows, :], vp, v_ref[rr, rows, :], bias_ref,
                             has_prev)
        o_ref[rr, rows, :] = o.astype(o_ref.dtype)
        lse_ref[rr, rows, :] = lse

    for rr in range(residues):
        if with_prev:
            run(rr, 0, kprev_ref[rr], vprev_ref[rr], n > 0)
        else:
            run(rr, 0, k_ref[rr, :N_STEPS, :], v_ref[rr, :N_STEPS, :], False)
        for j in range(1, blocks):
            prev = slice((j - 1) * N_STEPS, j * N_STEPS)
            run(rr, j, k_ref[rr, prev, :], v_ref[rr, prev, :], True)


def _attention_group(qkv, bias, blocks, residues):
    B, d, L, _ = qkv.shape
    W = GROUP_WIDTH
    rows = blocks * N_STEPS
    steps = L // rows
    with_prev = steps > 1

    def cur(c):
        return pl.BlockSpec((None, residues, rows, W), lambda b, r, n: (b, r, n, c))

    def prev(c):
        return pl.BlockSpec((None, residues, N_STEPS, W),
                            lambda b, r, n: (b, r, jnp.maximum(n * blocks - 1, 0), c))

    in_specs = [cur(0), cur(1), cur(2)] + ([prev(1), prev(2)] if with_prev else [])
    in_specs.append(pl.BlockSpec((HEADS_PER_GROUP * N_STEPS, 2 * N_STEPS), lambda b, r, n: (0, 0)))
    operands = [qkv] * (5 if with_prev else 3) + [bias]
    return pl.pallas_call(
        functools.partial(_attn_kernel, blocks=blocks, residues=residues, with_prev=with_prev),
        out_shape=(jax.ShapeDtypeStruct((B, d, L, W), BF16), jax.ShapeDtypeStruct((B, d, L, LANES), F32)),
        grid=(B, d // residues, steps),
        in_specs=in_specs,
        out_specs=(pl.BlockSpec((None, residues, rows, W), lambda b, r, n: (b, r, n, 0)),
                   pl.BlockSpec((None, residues, rows, LANES), lambda b, r, n: (b, r, n, 0))),
        compiler_params=_cparams(("arbitrary", "arbitrary", "arbitrary")),
        name=f"attn_d{d}",
    )(*operands)


def _t5_bucket(dist):
    max_exact = REL_BUCKETS // 2
    d_f = jnp.maximum(dist, 1).astype(F32)
    large = max_exact + (jnp.log(d_f / max_exact) / math.log(REL_MAX_DIST / max_exact)
                         * (REL_BUCKETS - max_exact)).astype(jnp.int32)
    large = jnp.minimum(large, REL_BUCKETS - 1)
    return jnp.where(dist < max_exact, dist, large)


def _band_bias(rel_bias_g, dilation):
    qi = jnp.arange(N_STEPS)[:, None]
    kj = jnp.arange(2 * N_STEPS)[None, :]
    step_dist = qi + N_STEPS - kj
    band = (step_dist >= 0) & (step_dist <= N_STEPS)
    bucket = _t5_bucket(jnp.maximum(step_dist, 0) * dilation)
    table = rel_bias_g.astype(F32).T
    hit = bucket[None, None] == jnp.arange(REL_BUCKETS)[None, :, None, None]
    bias = jnp.sum(jnp.where(hit, table[:, :, None, None], 0.0), axis=1)
    return jnp.where(band[None], bias, NEG_BIG).reshape(HEADS_PER_GROUP * N_STEPS, 2 * N_STEPS)


def _ret_kernel(q_ref, k_ref, v0_ref, v1_ref, g0_ref, g1_ref, cos_ref, sin_ref, decay_ref, xi_ref, zeta_ref,
                o_ref, state_ref, *, g_chunk):
    n = pl.program_id(1)

    @pl.when(n == 0)
    def _():
        state_ref[...] = jnp.zeros_like(state_ref)

    C = RET_CHUNK
    n_sub = q_ref.shape[0] // C
    lane = lax.broadcasted_iota(jnp.int32, (C, RET_DK), 1)
    even = (lane % 2) == 0

    def rotary(t, rows):
        partner = jnp.where(even, pltpu.roll(t, RET_DK - 1, 1), pltpu.roll(t, 1, 1))
        return t * cos_ref[rows, :] + partner * sin_ref[rows, :]

    heads = range(RET_HEADS)
    ks = [slice(h * RET_DK, (h + 1) * RET_DK) for h in heads]
    vs = [slice((h % 2) * RET_DV, (h % 2 + 1) * RET_DV) for h in heads]
    v_refs = [v0_ref if h < 2 else v1_ref for h in heads]
    g_refs = [g0_ref if h < 2 else g1_ref for h in heads]
    states = [state_ref[h] for h in heads]
    for sub in range(n_sub):
        rows = slice(sub * C, (sub + 1) * C)
        qbs, kzs, inners, vals = [], [], [], []
        for h in heads:
            qr = rotary(q_ref[rows, ks[h]].astype(F32), rows)
            kr = rotary(k_ref[rows, ks[h]].astype(F32), rows) * (RET_DK ** -0.5)
            v = v_refs[h][rows, vs[h]]
            qb = qr.astype(BF16)
            scores = lax.dot_general(qb, kr.astype(BF16), (((1,), (1,)), ((), ())),
                                     preferred_element_type=F32) * decay_ref[h]
            inners.append(jnp.dot(scores.astype(BF16), v, preferred_element_type=F32))
            qbs.append(qb)
            kzs.append((kr * zeta_ref[h]).astype(BF16))
            vals.append(v)
        for h in heads:
            cross = jnp.dot(qbs[h], states[h].astype(BF16), preferred_element_type=F32) * xi_ref[h]
            states[h] = states[h] * g_chunk[h] + lax.dot_general(kzs[h], vals[h], (((0,), (0,)), ((), ())),
                                                                 preferred_element_type=F32)
            ret = inners[h] + cross
            mu = jnp.mean(ret, axis=-1, keepdims=True)
            cen = ret - mu
            var = jnp.mean(cen * cen, axis=-1, keepdims=True)
            normed = cen * lax.rsqrt(var + NORM_EPS)
            g = g_refs[h][rows, vs[h]].astype(F32)
            o_ref[rows, h * RET_DV:(h + 1) * RET_DV] = (g * jax.nn.sigmoid(g) * normed).astype(o_ref.dtype)
    for h in heads:
        state_ref[h] = states[h]


def _retention(proj, B, S, col0):
    T = B * S
    nc = S // RET_CHUNK
    C = RET_CHUNK
    log_g = np.log(1.0 - 2.0 ** (-5.0 - np.arange(RET_HEADS, dtype=np.float64)))
    idx = np.arange(C, dtype=np.float64)
    diff = idx[:, None] - idx[None, :]
    decay = np.where(diff >= 0, np.exp(np.maximum(diff, 0.0)[None] * log_g[:, None, None]), 0.0)
    xi = np.exp((idx + 1.0)[None, :] * log_g[:, None])
    zeta = np.exp((C - 1.0 - idx)[None, :] * log_g[:, None])
    g_chunk = tuple(float(v) for v in np.exp(C * log_g))
    xi_b = np.broadcast_to(xi[:, :, None], (RET_HEADS, C, RET_DV)).astype(np.float32)
    zeta_b = np.broadcast_to(zeta[:, :, None], (RET_HEADS, C, RET_DK)).astype(np.float32)

    inv = ROPE_BASE ** (-np.arange(0, RET_DK, 2, dtype=np.float64) / RET_DK)
    ang = np.arange(S, dtype=np.float64)[:, None] * inv[None]
    cos_t = np.repeat(np.cos(ang), 2, axis=1).astype(np.float32)
    sin_t = np.stack([-np.sin(ang), np.sin(ang)], axis=-1).reshape(S, RET_DK).astype(np.float32)

    per_step = max(n for n in range(1, RET_CHUNKS_PER_STEP + 1) if nc % n == 0)
    rows = per_step * C
    ns = S // rows

    def col(c):
        return pl.BlockSpec((rows, COL_BLOCK), lambda b, n, c=c: (b * ns + n, c))

    const3 = lambda shape: pl.BlockSpec(shape, lambda b, n: (0, 0, 0))
    return pl.pallas_call(
        functools.partial(_ret_kernel, g_chunk=g_chunk),
        out_shape=jax.ShapeDtypeStruct((T, RET_HEADS * RET_DV), BF16),
        grid=(B, ns),
        in_specs=[col(col0), col(col0 + 1), col(col0 + 2), col(col0 + 3), col(col0 + 4), col(col0 + 5),
                  pl.BlockSpec((rows, RET_DK), lambda b, n: (n, 0)),
                  pl.BlockSpec((rows, RET_DK), lambda b, n: (n, 0)),
                  const3((RET_HEADS, C, C)), const3((RET_HEADS, C, RET_DV)), const3((RET_HEADS, C, RET_DK))],
        out_specs=pl.BlockSpec((rows, RET_HEADS * RET_DV), lambda b, n: (b * ns + n, 0)),
        scratch_shapes=[pltpu.VMEM((RET_HEADS, RET_DK, RET_DV), F32)],
        compiler_params=_cparams(("arbitrary", "arbitrary")),
        name="retention",
    )(proj, proj, proj, proj, proj, proj, jnp.asarray(cos_t), jnp.asarray(sin_t),
      jnp.asarray(decay.astype(np.float32)), jnp.asarray(xi_b), jnp.asarray(zeta_b))


ROUTE_IDX, ROUTE_RANK, ROUTE_GATE = 0, 4, 8


ROW_TILE = 8


def _store_row_tiles(dst_ref, val):
    n = val.shape[0]
    for c in range(ROW_TILE):
        dst_ref[pl.ds(c, n, stride=ROW_TILE), :] = val[:, c * LANES:(c + 1) * LANES]


def _load_row_tiles(src_ref, n):
    return jnp.concatenate([src_ref[pl.ds(c, n, stride=ROW_TILE), :] for c in range(ROW_TILE)], axis=-1)


def _to_token_order(src_ref, dst_ref, d):
    n = src_ref.shape[1]
    for r in range(d):
        blk = src_ref[r].astype(F32)
        for s in range(dst_ref.shape[0]):
            dst_ref[s, pl.ds(r, n, stride=d), :] = blk[:, s * LANES:(s + 1) * LANES]


def _merge_kernel(x_ref, o1_ref, o2_ref, o3_ref, l1_ref, l2_ref, l3_ref, ret_ref,
                  ga0_ref, ga1_ref, gr0_ref, gr1_ref, watt_ref, wret_ref, wout_ref, gffn_ref, wr_ref, br_ref,
                  below_ref, x1_ref, hf_ref, route_ref, counts_ref, on2_ref, on3_ref, ln2_ref, ln3_ref):
    tm = x_ref.shape[0]
    _to_token_order(o2_ref, on2_ref, DILATIONS[1])
    _to_token_order(o3_ref, on3_ref, DILATIONS[2])
    _to_token_order(l2_ref, ln2_ref, DILATIONS[1])
    _to_token_order(l3_ref, ln3_ref, DILATIONS[2])
    lses = [l1_ref[...], ln2_ref[0], ln3_ref[0]]

    att_parts = []
    for h in range(HEADS_PER_GROUP):
        ls = [l[:, h:h + 1] for l in lses]
        m = jnp.maximum(jnp.maximum(ls[0], ls[1]), ls[2])
        ws = [jnp.exp(l - m) for l in ls]
        wsum = ws[0] + ws[1] + ws[2]
        hs = slice(h * HEAD_DIM, (h + 1) * HEAD_DIM)
        acc = (ws[0] / wsum) * o1_ref[:, hs].astype(F32)
        acc = acc + (ws[1] / wsum) * on2_ref[h]
        acc = acc + (ws[2] / wsum) * on3_ref[h]
        att_parts.append(acc)
    att = jnp.concatenate(att_parts, axis=-1).astype(BF16)

    a_proj = jnp.dot(att, watt_ref[...], preferred_element_type=F32)
    r_proj = jnp.dot(ret_ref[...], wret_ref[...], preferred_element_type=F32)
    gate_a = jnp.concatenate([ga0_ref[...], ga1_ref[...]], axis=-1).astype(F32)
    gate_r = jnp.concatenate([gr0_ref[...], gr1_ref[...]], axis=-1).astype(F32)
    merged = jax.nn.sigmoid(gate_a) * a_proj + jax.nn.sigmoid(gate_r) * r_proj
    x1 = x_ref[...] + jnp.dot(merged.astype(BF16), wout_ref[...], preferred_element_type=F32)
    x1_ref[...] = x1

    ms = jnp.mean(x1 * x1, axis=-1, keepdims=True)
    hf = x1 * lax.rsqrt(ms + NORM_EPS) * gffn_ref[...]
    _store_row_tiles(hf_ref, hf)

    logits = jnp.dot(hf.astype(BF16), wr_ref[...], preferred_element_type=F32) + br_ref[...]
    lane = lax.broadcasted_iota(jnp.int32, (tm, LANES), 1)
    lane_f = lane.astype(F32)
    work = logits
    vals, idxs, hits = [], [], []
    for _ in range(TOP_K):
        mk = jnp.max(work, axis=-1, keepdims=True)
        ik = jnp.min(jnp.where(work == mk, lane_f, float(LANES)), axis=-1, keepdims=True)
        hit = lane_f == ik
        work = jnp.where(hit, -jnp.inf, work)
        vals.append(mk)
        idxs.append(ik)
        hits.append(hit)
    es = [jnp.exp(v - vals[0]) for v in vals]
    esum = es[0] + es[1] + es[2] + es[3]
    gates = [e / esum for e in es]

    sel = jnp.zeros((tm, LANES), F32)
    for hit in hits:
        sel = jnp.where(hit, 1.0, sel)

    @pl.when(pl.program_id(0) == 0)
    def _():
        counts_ref[...] = jnp.zeros_like(counts_ref)

    rank = jnp.dot(below_ref[...], sel.astype(BF16), preferred_element_type=F32) + counts_ref[...]
    counts_ref[...] = counts_ref[...] + jnp.sum(sel, axis=0, keepdims=True)

    route = jnp.zeros((tm, LANES), F32)
    for k in range(TOP_K):
        rank_k = jnp.sum(jnp.where(hits[k], rank, 0.0), axis=-1, keepdims=True)
        route = jnp.where(lane == ROUTE_IDX + k, idxs[k], route)
        route = jnp.where(lane == ROUTE_RANK + k, rank_k, route)
        route = jnp.where(lane == ROUTE_GATE + k, gates[k], route)
    route_ref[...] = route


def _merge(x2d, B, S, o1, l1, o2, l2, o3, l3, ret, proj, gate_col0, w_att, w_ret, w_out, g_ffn, w_router_p,
           b_router_p, tm=512):
    T, D = x2d.shape
    per_b = S // tm
    d4, d16 = DILATIONS[1], DILATIONS[2]
    row = lambda w: pl.BlockSpec((tm, w), lambda i: (i, 0))
    full = lambda a: pl.BlockSpec(a.shape, lambda i: (0,) * a.ndim)
    gcol = lambda c: pl.BlockSpec((tm, COL_BLOCK), lambda i, c=c: (i, c))
    res = lambda d, w: pl.BlockSpec((None, d, tm // d, w), lambda i: (i // per_b, 0, i % per_b, 0))
    below = jnp.tril(jnp.ones((tm, tm), BF16), k=-1)
    slabs = lambda n: pltpu.VMEM((n, tm, LANES), F32)
    return pl.pallas_call(
        _merge_kernel,
        out_shape=(jax.ShapeDtypeStruct((T, D), F32), jax.ShapeDtypeStruct((T * ROW_TILE, LANES), F32),
                   jax.ShapeDtypeStruct((T, LANES), F32), jax.ShapeDtypeStruct((1, LANES), F32)),
        grid=(T // tm,),
        in_specs=[row(D), row(GROUP_WIDTH), res(d4, GROUP_WIDTH), res(d16, GROUP_WIDTH),
                  row(LANES), res(d4, LANES), res(d16, LANES),
                  row(RET_HEADS * RET_DV), gcol(gate_col0), gcol(gate_col0 + 1), gcol(gate_col0 + 2),
                  gcol(gate_col0 + 3), full(w_att), full(w_ret), full(w_out), full(g_ffn), full(w_router_p),
                  full(b_router_p), full(below)],
        out_specs=(row(D), pl.BlockSpec((tm * ROW_TILE, LANES), lambda i: (i, 0)), row(LANES),
                   pl.BlockSpec((1, LANES), lambda i: (0, 0))),
        scratch_shapes=[slabs(GROUP_WIDTH // LANES), slabs(GROUP_WIDTH // LANES), slabs(1), slabs(1)],
        compiler_params=_cparams(("arbitrary",), VMEM_LIMIT_BIG),
        name="merge",
    )(x2d, o1, o2, o3, l1, l2, l3, ret, proj, proj, proj, proj, w_att, w_ret, w_out, g_ffn, w_router_p,
      b_router_p, below)


def _row_copy(src, dst, src_row8, dst_row8, sem):
    return pltpu.make_async_copy(src.at[pl.ds(pl.multiple_of(src_row8, ROW_TILE), ROW_TILE)],
                                 dst.at[pl.ds(pl.multiple_of(dst_row8, ROW_TILE), ROW_TILE)], sem)


def _dispatch_kernel(pad_ref, pos_ref, hf_ref, xs_hbm, zero_ref, sem, *, tile):
    @pl.when(pl.program_id(0) == 0)
    def _():
        zero_ref[...] = jnp.zeros_like(zero_ref)
        total = 0
        for e in range(N_EXPERTS):
            first = pad_ref[e]

            def zero_row(j, carry, first=first):
                _row_copy(zero_ref, xs_hbm, 0, first + j * ROW_TILE, sem).start()
                return carry

            lax.fori_loop(0, pad_ref[N_EXPERTS + e], zero_row, 0)
            total = total + pad_ref[N_EXPERTS + e]

        def zero_wait(n, carry):
            _row_copy(zero_ref, xs_hbm, 0, 0, sem).wait()
            return carry

        lax.fori_loop(0, total, zero_wait, 0)

    def issue(i, carry):
        for k in range(TOP_K):
            _row_copy(hf_ref, xs_hbm, i * ROW_TILE, pos_ref[0, k, i], sem).start(priority=k % 2)
        return carry

    lax.fori_loop(0, tile, issue, 0, unroll=ISSUE_UNROLL)

    def drain(i, carry):
        for k in range(TOP_K):
            _row_copy(hf_ref, xs_hbm, 0, 0, sem).wait()
        return carry

    lax.fori_loop(0, tile, drain, 0, unroll=DRAIN_UNROLL)


def _dispatch(hf, pos8_tiles, pad_info, n_rows, tile):
    T = hf.shape[0] // ROW_TILE
    grid_spec = pltpu.PrefetchScalarGridSpec(
        num_scalar_prefetch=1,
        grid=(T // tile,),
        in_specs=[pl.BlockSpec((1, TOP_K, tile), lambda i, pad: (i, 0, 0), memory_space=pltpu.SMEM),
                  pl.BlockSpec((tile * ROW_TILE, LANES), lambda i, pad: (i, 0))],
        out_specs=pl.BlockSpec(memory_space=pl.ANY),
        scratch_shapes=[pltpu.VMEM((ROW_TILE, LANES), hf.dtype), pltpu.SemaphoreType.DMA(())],
    )
    return pl.pallas_call(
        functools.partial(_dispatch_kernel, tile=tile),
        out_shape=jax.ShapeDtypeStruct((n_rows * ROW_TILE, LANES), hf.dtype),
        grid_spec=grid_spec,
        compiler_params=_cparams(("arbitrary",)),
        name="dispatch",
    )(pad_info, pos8_tiles, hf)


def _expert_kernel(be_ref, nu_ref, x_ref, w1_ref, b1_ref, w2_ref, b2_ref, y_ref, w1b_ref, w2b_ref):
    i = pl.program_id(0)
    F = w2_ref.shape[0]

    @pl.when(i < nu_ref[0])
    def _():
        @pl.when((i == 0) | (be_ref[i] != be_ref[jnp.maximum(i - 1, 0)]))
        def _():
            w1b_ref[...] = w1_ref[...].astype(BF16)
            w2b_ref[...] = w2_ref[...].astype(BF16)

        x = _load_row_tiles(x_ref, EXPERT_BLOCK).astype(BF16)
        gu = jnp.dot(x, w1b_ref[...], preferred_element_type=F32) + b1_ref[...]
        gate = jnp.minimum(gu[:, :F], SWIGLU_LIMIT)
        up = jnp.clip(gu[:, F:], -SWIGLU_LIMIT, SWIGLU_LIMIT)
        act = (up + 1.0) * gate * jax.nn.sigmoid(SWIGLU_ALPHA * gate)
        _store_row_tiles(y_ref, jnp.dot(act.astype(BF16), w2b_ref[...], preferred_element_type=F32) + b2_ref[...])


def _experts(xs, n_blocks, blk_expert, n_used, w1, b1, w2, b2):
    E, D, F2 = w1.shape
    F = w2.shape[1]
    M = EXPERT_BLOCK
    blk = lambda i, be, nu: (jnp.minimum(i, nu[0] - 1), 0)
    grid_spec = pltpu.PrefetchScalarGridSpec(
        num_scalar_prefetch=2,
        grid=(n_blocks,),
        in_specs=[pl.BlockSpec((M * ROW_TILE, LANES), blk),
                  pl.BlockSpec((None, D, F2), lambda i, be, nu: (be[i], 0, 0)),
                  pl.BlockSpec((None, 1, F2), lambda i, be, nu: (be[i], 0, 0)),
                  pl.BlockSpec((None, F, D), lambda i, be, nu: (be[i], 0, 0)),
                  pl.BlockSpec((None, 1, D), lambda i, be, nu: (be[i], 0, 0))],
        out_specs=pl.BlockSpec((M * ROW_TILE, LANES), blk),
        scratch_shapes=[pltpu.VMEM((D, F2), BF16), pltpu.VMEM((F, D), BF16)],
    )
    return pl.pallas_call(
        _expert_kernel,
        out_shape=jax.ShapeDtypeStruct((n_blocks * M * ROW_TILE, LANES), F32),
        grid_spec=grid_spec,
        compiler_params=_cparams(("arbitrary",), VMEM_LIMIT_BIG),
        name="experts",
    )(blk_expert, n_used, xs, w1, b1.reshape(E, 1, F2), w2, b2.reshape(E, 1, D))


def _combine_kernel(pos_ref, pos_next_ref, route_ref, x1_ref, g_ref, ys_hbm, o_ref, buf_ref, sem, *, tile):
    step = pl.program_id(0)
    cur = lax.rem(step, 2)

    def gather(p_ref, slot):
        def issue(i, carry):
            for k in range(TOP_K):
                _row_copy(ys_hbm, buf_ref.at[slot, k], p_ref[0, k, i], i * ROW_TILE,
                          sem.at[slot]).start(priority=k % 2)
            return carry

        lax.fori_loop(0, tile, issue, 0, unroll=ISSUE_UNROLL)

    nxt = 1 - cur

    @pl.when(step == 0)
    def _():
        gather(pos_ref, 0)

    def drain(slot):
        def body(i, carry):
            for k in range(TOP_K):
                _row_copy(ys_hbm, buf_ref.at[slot, k], 0, 0, sem.at[slot]).wait()
            return carry

        lax.fori_loop(0, tile, body, 0, unroll=DRAIN_UNROLL)

    @pl.when(step + 1 < pl.num_programs(0))
    def _():
        gather(pos_next_ref, nxt)

    drain(cur)

    route = route_ref[...]
    acc = x1_ref[...]
    for k in range(TOP_K):
        acc = acc + route[:, ROUTE_GATE + k:ROUTE_GATE + k + 1] * _load_row_tiles(buf_ref.at[cur, k], tile)
    ms = jnp.mean(acc * acc, axis=-1, keepdims=True)
    o_ref[...] = acc * lax.rsqrt(ms + NORM_EPS) * g_ref[...]


def _combine(pos_tiles, route, x1, g_final, ys, tile):
    T, D = x1.shape
    nt = T // tile
    return pl.pallas_call(
        functools.partial(_combine_kernel, tile=tile),
        out_shape=jax.ShapeDtypeStruct((T, D), F32),
        grid=(nt,),
        in_specs=[pl.BlockSpec((1, TOP_K, tile), lambda i: (i, 0, 0), memory_space=pltpu.SMEM),
                  pl.BlockSpec((1, TOP_K, tile), lambda i: (jnp.minimum(i + 1, nt - 1), 0, 0),
                               memory_space=pltpu.SMEM),
                  pl.BlockSpec((tile, LANES), lambda i: (i, 0)),
                  pl.BlockSpec((tile, D), lambda i: (i, 0)),
                  pl.BlockSpec((1, D), lambda i: (0, 0)),
                  pl.BlockSpec(memory_space=pl.ANY)],
        out_specs=pl.BlockSpec((tile, D), lambda i: (i, 0)),
        scratch_shapes=[pltpu.VMEM((2, TOP_K, tile * ROW_TILE, LANES), F32), pltpu.SemaphoreType.DMA((2,))],
        compiler_params=_cparams(("arbitrary",)),
        name="combine",
    )(pos_tiles, pos_tiles, route, x1, g_final.reshape(1, D), ys)


def _layer(x2d, B, S, norm_mix_g, w_in, rel_bias, w_att, w_ret, w_out, norm_ffn_g, w_router, b_router,
           w1, b1, w2, b2):
    T, D = x2d.shape
    proj, a4, a16 = _inproj(x2d, norm_mix_g, w_in.astype(BF16), B, S)

    group_bias = [_band_bias(rel_bias[:, g * HEADS_PER_GROUP:(g + 1) * HEADS_PER_GROUP], d)
                  for g, d in enumerate(DILATIONS)]
    per_step = ATTN_BLOCKS_PER_STEP

    def attend(qkv, bias):
        blocks = min(qkv.shape[2] // N_STEPS, per_step)
        residues = min(qkv.shape[1], max(per_step // blocks, 1))
        return _attention_group(qkv, bias, blocks, residues)

    o1, l1 = attend(proj.reshape(B, 1, S, proj.shape[1]), group_bias[0])
    o2, l2 = attend(a4, group_bias[1])
    o3, l3 = attend(a16, group_bias[2])

    ret = _retention(proj, B, S, MAIN_RET_COL)

    w_router_p = jnp.zeros((D, LANES), BF16).at[:, :N_EXPERTS].set(w_router.astype(BF16))
    b_router_p = jnp.full((1, LANES), NEG_BIG, F32).at[0, :N_EXPERTS].set(b_router.astype(F32))
    x1, hf, route, counts = _merge(x2d, B, S, o1.reshape(T, GROUP_WIDTH), l1.reshape(T, LANES), o2, l2, o3, l3,
                                   ret, proj, MAIN_GATE_COL, w_att.astype(BF16), w_ret.astype(BF16),
                                   w_out.astype(BF16), norm_ffn_g.reshape(1, D), w_router_p, b_router_p)

    M = EXPERT_BLOCK
    n_blocks = (T * TOP_K) // M + N_EXPERTS
    cnt = counts[0, :N_EXPERTS].astype(jnp.int32)
    padded = ((cnt + M - 1) // M) * M
    pend = jnp.cumsum(padded)
    pstart = pend - padded
    idx = route[:, ROUTE_IDX:ROUTE_IDX + TOP_K].astype(jnp.int32)
    rank = route[:, ROUTE_RANK:ROUTE_RANK + TOP_K].astype(jnp.int32)
    experts = jnp.arange(N_EXPERTS, dtype=jnp.int32)
    pos = rank + jnp.sum(jnp.where(idx[..., None] == experts, pstart, 0), axis=-1)
    n_used = (pend[-1] // M).astype(jnp.int32)
    blk_row = jnp.minimum(jnp.arange(n_blocks, dtype=jnp.int32), jnp.maximum(n_used - 1, 0)) * M
    blk_expert = jnp.minimum(jnp.sum((pend[None, :] <= blk_row[:, None]).astype(jnp.int32), axis=-1),
                             N_EXPERTS - 1)

    assert D == ROW_TILE * LANES
    tile = ROW_MOVE_TILE
    pos_tiles = (pos * ROW_TILE).reshape(T // tile, tile, TOP_K).transpose(0, 2, 1)
    pad_info = jnp.concatenate([(pstart + cnt) * ROW_TILE, padded - cnt]).astype(jnp.int32)
    xs = _dispatch(hf, pos_tiles, pad_info, n_blocks * M, tile)
    ys = _experts(xs, n_blocks, blk_expert, n_used.reshape(1), w1, b1, w2, b2)
    return pos_tiles, route, x1, ys


def kernel(x, norm_mix_g, w_in, rel_bias, w_att_branch, w_ret_branch, w_out, norm_ffn_g, w_router, b_router,
           w1, b1, w2, b2, norm_final_g):
    B, S, D = x.shape
    depth = w_in.shape[0]
    assert depth == 1, "the combine stage applies the final norm; a deeper stack needs a separate norm pass"
    x2d = x.reshape(B * S, D)
    pos_tiles, route, x1, ys = _layer(x2d, B, S, norm_mix_g[0], w_in[0], rel_bias, w_att_branch[0],
                                      w_ret_branch[0], w_out[0], norm_ffn_g[0], w_router[0], b_router[0],
                                      w1[0], b1[0], w2[0], b2[0])
    out = _combine(pos_tiles, route, x1, norm_final_g, ys, tile=pos_tiles.shape[2])
    return out.reshape(B, S, D)
```

```python
import functools
import math

import numpy as np
import jax
import jax.numpy as jnp
from jax import lax
from jax.experimental import pallas as pl
from jax.experimental.pallas import tpu as pltpu

F32 = jnp.float32
BF16 = jnp.bfloat16

NORM_EPS = 1e-5

HEAD_DIM = 128
HEADS_PER_GROUP = 4
GROUP_WIDTH = HEADS_PER_GROUP * HEAD_DIM
DILATIONS = (1, 4, 16)
N_STEPS = 128
N_GROUPS = len(DILATIONS)
REL_BUCKETS = 32
REL_MAX_DIST = 2048

RET_HEADS = 4
RET_DK = 128
RET_DV = 256
RET_CHUNK = 128
ROPE_BASE = 10000.0

N_EXPERTS = 32
TOP_K = 4
SWIGLU_LIMIT = 7.0
SWIGLU_ALPHA = 1.702
EXPERT_BLOCK = 512
ROW_MOVE_TILE = 512
ISSUE_UNROLL = 16
DRAIN_UNROLL = 8
ATTN_BLOCKS_PER_STEP = 16
RET_CHUNKS_PER_STEP = 2

LANES = 128
COL_BLOCK = 512
NEG_BIG = -1e30

VMEM_LIMIT = 48 * 1024 * 1024
VMEM_LIMIT_BIG = 56 * 1024 * 1024

N_ATT_CHUNKS = 3 * N_GROUPS
MAIN_RET_COL = 3
MAIN_GATE_COL = MAIN_RET_COL + (2 * RET_HEADS * RET_DK + 2 * RET_HEADS * RET_DV) // COL_BLOCK


def _cparams(sem, vmem=VMEM_LIMIT):
    return pltpu.CompilerParams(dimension_semantics=sem, vmem_limit_bytes=vmem)


def _inproj_kernel(x_ref, g_ref, w_ref, p4_ref, p16_ref, main_ref, a4_ref, a16_ref, h_ref, hp_ref):
    tm = x_ref.shape[0]
    x = x_ref[...]
    ms = jnp.mean(x * x, axis=-1, keepdims=True)
    h_ref[...] = (x * lax.rsqrt(ms + NORM_EPS) * g_ref[...]).astype(BF16)
    n_chunks = w_ref.shape[1] // COL_BLOCK

    def chunk(lhs_ref, c):
        return jnp.dot(lhs_ref[...], w_ref[:, c * COL_BLOCK:(c + 1) * COL_BLOCK], preferred_element_type=F32)

    for c in range(N_ATT_CHUNKS, n_chunks):
        mc = c - N_ATT_CHUNKS + MAIN_RET_COL
        main_ref[:, mc * COL_BLOCK:(mc + 1) * COL_BLOCK] = chunk(h_ref, c).astype(BF16)
    for part in range(3):
        main_ref[:, part * COL_BLOCK:(part + 1) * COL_BLOCK] = chunk(h_ref, part * N_GROUPS).astype(BF16)
    for group, (perm_ref, dest) in ((1, (p4_ref, a4_ref)), (2, (p16_ref, a16_ref))):
        d = DILATIONS[group]
        n = tm // d
        hp_ref[...] = jnp.dot(perm_ref[...], h_ref[...], preferred_element_type=F32).astype(BF16)
        for part in range(3):
            res = chunk(hp_ref, part * N_GROUPS + group)
            for r in range(d):
                dest[r, :, part * COL_BLOCK:(part + 1) * COL_BLOCK] = res[r * n:(r + 1) * n, :].astype(BF16)


def _residue_permutation(tm, d):
    n = tm // d
    p = np.zeros((tm, tm), np.float32)
    m, r = np.meshgrid(np.arange(n), np.arange(d), indexing="ij")
    p[(r * n + m).ravel(), (m * d + r).ravel()] = 1.0
    return jnp.asarray(p, BF16)


def _inproj(x2d, g, w_bf16, B, S, tm=512):
    T, D = x2d.shape
    n_main = (w_bf16.shape[1] // COL_BLOCK - N_ATT_CHUNKS + 3) * COL_BLOCK
    d4, d16 = DILATIONS[1], DILATIONS[2]
    per_b = S // tm
    return pl.pallas_call(
        _inproj_kernel,
        out_shape=(jax.ShapeDtypeStruct((T, n_main), BF16),
                   jax.ShapeDtypeStruct((B, d4, S // d4, 3 * GROUP_WIDTH), BF16),
                   jax.ShapeDtypeStruct((B, d16, S // d16, 3 * GROUP_WIDTH), BF16)),
        grid=(T // tm,),
        in_specs=[
            pl.BlockSpec((tm, D), lambda i: (i, 0)),
            pl.BlockSpec((1, D), lambda i: (0, 0)),
            pl.BlockSpec(w_bf16.shape, lambda i: (0, 0), pipeline_mode=pl.Buffered(1)),
            pl.BlockSpec((tm, tm), lambda i: (0, 0), pipeline_mode=pl.Buffered(1)),
            pl.BlockSpec((tm, tm), lambda i: (0, 0), pipeline_mode=pl.Buffered(1)),
        ],
        out_specs=(pl.BlockSpec((tm, n_main), lambda i: (i, 0)),
                   pl.BlockSpec((None, d4, tm // d4, 3 * GROUP_WIDTH), lambda i: (i // per_b, 0, i % per_b, 0)),
                   pl.BlockSpec((None, d16, tm // d16, 3 * GROUP_WIDTH), lambda i: (i // per_b, 0, i % per_b, 0))),
        scratch_shapes=[pltpu.VMEM((tm, D), BF16), pltpu.VMEM((tm, D), BF16)],
        compiler_params=_cparams(("arbitrary",), VMEM_LIMIT_BIG),
        name="inproj",
    )(x2d, g.reshape(1, D), w_bf16, _residue_permutation(tm, d4), _residue_permutation(tm, d16))


def _attn_block(q, kp, kc, vp, vc, bias_ref, has_prev):
    scale = HEAD_DIM ** -0.5
    heads = range(HEADS_PER_GROUP)
    hs = [slice(h * HEAD_DIM, (h + 1) * HEAD_DIM) for h in heads]
    rs = [slice(h * N_STEPS, (h + 1) * N_STEPS) for h in heads]
    k = jnp.concatenate([kp, kc], axis=0)
    v = jnp.concatenate([vp, vc], axis=0)
    dn = (((1,), (1,)), ((), ()))
    s = jnp.concatenate([lax.dot_general(q[:, hs[h]], k[:, hs[h]], dn, preferred_element_type=F32)
                         for h in heads], axis=0)
    s = s * scale + bias_ref[...]
    if has_prev is not True:
        col = lax.broadcasted_iota(jnp.int32, s.shape, 1)
        s = jnp.where((col >= N_STEPS) | has_prev, s, NEG_BIG)
    m = jnp.max(s, axis=-1, keepdims=True)
    p = jnp.exp(s - m)
    l = jnp.sum(p, axis=-1, keepdims=True)
    pb = p.astype(BF16)
    lse_rows = m + jnp.log(l)
    lane = lax.broadcasted_iota(jnp.int32, (N_STEPS, LANES), 1)
    outs = []
    lse = jnp.zeros((N_STEPS, LANES), F32)
    for h in heads:
        acc = jnp.dot(pb[rs[h], :], v[:, hs[h]], preferred_element_type=F32)
        outs.append(acc / l[rs[h], :])
        lse = jnp.where(lane == h, lse_rows[rs[h], :], lse)
    return jnp.concatenate(outs, axis=-1), lse


def _attn_kernel(*refs, blocks, residues, with_prev):
    if with_prev:
        q_ref, k_ref, v_ref, kprev_ref, vprev_ref, bias_ref, o_ref, lse_ref = refs
    else:
        q_ref, k_ref, v_ref, bias_ref, o_ref, lse_ref = refs
    n = pl.program_id(2)

    def run(rr, j, kp, vp, has_prev):
        rows = slice(j * N_STEPS, (j + 1) * N_STEPS)
        o, lse = _attn_block(q_ref[rr, rows, :], kp, k_ref[rr, rows, :], vp, v_ref[rr, rows, :], bias_ref,
                             has_prev)
        o_ref[rr, rows, :] = o.astype(o_ref.dtype)
        lse_ref[rr, rows, :] = lse

    for rr in range(residues):
        if with_prev:
            run(rr, 0, kprev_ref[rr], vprev_ref[rr], n > 0)
        else:
            run(rr, 0, k_ref[rr, :N_STEPS, :], v_ref[rr, :N_STEPS, :], False)
        for j in range(1, blocks):
            prev = slice((j - 1) * N_STEPS, j * N_STEPS)
            run(rr, j, k_ref[rr, prev, :], v_ref[rr, prev, :], True)


def _attention_group(qkv, bias, blocks, residues):
    B, d, L, _ = qkv.shape
    W = GROUP_WIDTH
    rows = blocks * N_STEPS
    steps = L // rows
    with_prev = steps > 1

    def cur(c):
        return pl.BlockSpec((None, residues, rows, W), lambda b, r, n: (b, r, n, c))

    def prev(c):
        return pl.BlockSpec((None, residues, N_STEPS, W),
                            lambda b, r, n: (b, r, jnp.maximum(n * blocks - 1, 0), c))

    in_specs = [cur(0), cur(1), cur(2)] + ([prev(1), prev(2)] if with_prev else [])
    in_specs.append(pl.BlockSpec((HEADS_PER_GROUP * N_STEPS, 2 * N_STEPS), lambda b, r, n: (0, 0)))
    operands = [qkv] * (5 if with_prev else 3) + [bias]
    return pl.pallas_call(
        functools.partial(_attn_kernel, blocks=blocks, residues=residues, with_prev=with_prev),
        out_shape=(jax.ShapeDtypeStruct((B, d, L, W), BF16), jax.ShapeDtypeStruct((B, d, L, LANES), F32)),
        grid=(B, d // residues, steps),
        in_specs=in_specs,
        out_specs=(pl.BlockSpec((None, residues, rows, W), lambda b, r, n: (b, r, n, 0)),
                   pl.BlockSpec((None, residues, rows, LANES), lambda b, r, n: (b, r, n, 0))),
        compiler_params=_cparams(("arbitrary", "arbitrary", "arbitrary")),
        name=f"attn_d{d}",
    )(*operands)


def _t5_bucket(dist):
    max_exact = REL_BUCKETS // 2
    d_f = jnp.maximum(dist, 1).astype(F32)
    large = max_exact + (jnp.log(d_f / max_exact) / math.log(REL_MAX_DIST / max_exact)
                         * (REL_BUCKETS - max_exact)).astype(jnp.int32)
    large = jnp.minimum(large, REL_BUCKETS - 1)
    return jnp.where(dist < max_exact, dist, large)


def _band_bias(rel_bias_g, dilation):
    qi = jnp.arange(N_STEPS)[:, None]
    kj = jnp.arange(2 * N_STEPS)[None, :]
    step_dist = qi + N_STEPS - kj
    band = (step_dist >= 0) & (step_dist <= N_STEPS)
    bucket = _t5_bucket(jnp.maximum(step_dist, 0) * dilation)
    table = rel_bias_g.astype(F32).T
    hit = bucket[None, None] == jnp.arange(REL_BUCKETS)[None, :, None, None]
    bias = jnp.sum(jnp.where(hit, table[:, :, None, None], 0.0), axis=1)
    return jnp.where(band[None], bias, NEG_BIG).reshape(HEADS_PER_GROUP * N_STEPS, 2 * N_STEPS)


def _ret_kernel(q_ref, k_ref, v0_ref, v1_ref, g0_ref, g1_ref, cos_ref, sin_ref, decay_ref, xi_ref, zeta_ref,
                o_ref, state_ref, *, g_chunk):
    n = pl.program_id(1)

    @pl.when(n == 0)
    def _():
        state_ref[...] = jnp.zeros_like(state_ref)

    C = RET_CHUNK
    n_sub = q_ref.shape[0] // C
    lane = lax.broadcasted_iota(jnp.int32, (C, RET_DK), 1)
    even = (lane % 2) == 0

    def rotary(t, rows):
        partner = jnp.where(even, pltpu.roll(t, RET_DK - 1, 1), pltpu.roll(t, 1, 1))
        return t * cos_ref[rows, :] + partner * sin_ref[rows, :]

    heads = range(RET_HEADS)
    ks = [slice(h * RET_DK, (h + 1) * RET_DK) for h in heads]
    vs = [slice((h % 2) * RET_DV, (h % 2 + 1) * RET_DV) for h in heads]
    v_refs = [v0_ref if h < 2 else v1_ref for h in heads]
    g_refs = [g0_ref if h < 2 else g1_ref for h in heads]
    states = [state_ref[h] for h in heads]
    for sub in range(n_sub):
        rows = slice(sub * C, (sub + 1) * C)
        qbs, kzs, inners, vals = [], [], [], []
        for h in heads:
            qr = rotary(q_ref[rows, ks[h]].astype(F32), rows)
            kr = rotary(k_ref[rows, ks[h]].astype(F32), rows) * (RET_DK ** -0.5)
            v = v_refs[h][rows, vs[h]]
            qb = qr.astype(BF16)
            scores = lax.dot_general(qb, kr.astype(BF16), (((1,), (1,)), ((), ())),
                                     preferred_element_type=F32) * decay_ref[h]
            inners.append(jnp.dot(scores.astype(BF16), v, preferred_element_type=F32))
            qbs.append(qb)
            kzs.append((kr * zeta_ref[h]).astype(BF16))
            vals.append(v)
        for h in heads:
            cross = jnp.dot(qbs[h], states[h].astype(BF16), preferred_element_type=F32) * xi_ref[h]
            states[h] = states[h] * g_chunk[h] + lax.dot_general(kzs[h], vals[h], (((0,), (0,)), ((), ())),
                                                                 preferred_element_type=F32)
            ret = inners[h] + cross
            mu = jnp.mean(ret, axis=-1, keepdims=True)
            cen = ret - mu
            var = jnp.mean(cen * cen, axis=-1, keepdims=True)
            normed = cen * lax.rsqrt(var + NORM_EPS)
            g = g_refs[h][rows, vs[h]].astype(F32)
            o_ref[rows, h * RET_DV:(h + 1) * RET_DV] = (g * jax.nn.sigmoid(g) * normed).astype(o_ref.dtype)
    for h in heads:
        state_ref[h] = states[h]


def _retention(proj, B, S, col0):
    T = B * S
    nc = S // RET_CHUNK
    C = RET_CHUNK
    log_g = np.log(1.0 - 2.0 ** (-5.0 - np.arange(RET_HEADS, dtype=np.float64)))
    idx = np.arange(C, dtype=np.float64)
    diff = idx[:, None] - idx[None, :]
    decay = np.where(diff >= 0, np.exp(np.maximum(diff, 0.0)[None] * log_g[:, None, None]), 0.0)
    xi = np.exp((idx + 1.0)[None, :] * log_g[:, None])
    zeta = np.exp((C - 1.0 - idx)[None, :] * log_g[:, None])
    g_chunk = tuple(float(v) for v in np.exp(C * log_g))
    xi_b = np.broadcast_to(xi[:, :, None], (RET_HEADS, C, RET_DV)).astype(np.float32)
    zeta_b = np.broadcast_to(zeta[:, :, None], (RET_HEADS, C, RET_DK)).astype(np.float32)

    inv = ROPE_BASE ** (-np.arange(0, RET_DK, 2, dtype=np.float64) / RET_DK)
    ang = np.arange(S, dtype=np.float64)[:, None] * inv[None]
    cos_t = np.repeat(np.cos(ang), 2, axis=1).astype(np.float32)
    sin_t = np.stack([-np.sin(ang), np.sin(ang)], axis=-1).reshape(S, RET_DK).astype(np.float32)

    per_step = max(n for n in range(1, RET_CHUNKS_PER_STEP + 1) if nc % n == 0)
    rows = per_step * C
    ns = S // rows

    def col(c):
        return pl.BlockSpec((rows, COL_BLOCK), lambda b, n, c=c: (b * ns + n, c))

    const3 = lambda shape: pl.BlockSpec(shape, lambda b, n: (0, 0, 0))
    return pl.pallas_call(
        functools.partial(_ret_kernel, g_chunk=g_chunk),
        out_shape=jax.ShapeDtypeStruct((T, RET_HEADS * RET_DV), BF16),
        grid=(B, ns),
        in_specs=[col(col0), col(col0 + 1), col(col0 + 2), col(col0 + 3), col(col0 + 4), col(col0 + 5),
                  pl.BlockSpec((rows, RET_DK), lambda b, n: (n, 0)),
                  pl.BlockSpec((rows, RET_DK), lambda b, n: (n, 0)),
                  const3((RET_HEADS, C, C)), const3((RET_HEADS, C, RET_DV)), const3((RET_HEADS, C, RET_DK))],
        out_specs=pl.BlockSpec((rows, RET_HEADS * RET_DV), lambda b, n: (b * ns + n, 0)),
        scratch_shapes=[pltpu.VMEM((RET_HEADS, RET_DK, RET_DV), F32)],
        compiler_params=_cparams(("arbitrary", "arbitrary")),
        name="retention",
    )(proj, proj, proj, proj, proj, proj, jnp.asarray(cos_t), jnp.asarray(sin_t),
      jnp.asarray(decay.astype(np.float32)), jnp.asarray(xi_b), jnp.asarray(zeta_b))


ROUTE_IDX, ROUTE_RANK, ROUTE_GATE = 0, 4, 8


ROW_TILE = 8


def _store_row_tiles(dst_ref, val):
    n = val.shape[0]
    for c in range(ROW_TILE):
        dst_ref[pl.ds(c, n, stride=ROW_TILE), :] = val[:, c * LANES:(c + 1) * LANES]


def _load_row_tiles(src_ref, n):
    return jnp.concatenate([src_ref[pl.ds(c, n, stride=ROW_TILE), :] for c in range(ROW_TILE)], axis=-1)


def _to_token_order(src_ref, dst_ref, d):
    n = src_ref.shape[1]
    for r in range(d):
        blk = src_ref[r].astype(F32)
        for s in range(dst_ref.shape[0]):
            dst_ref[s, pl.ds(r, n, stride=d), :] = blk[:, s * LANES:(s + 1) * LANES]


def _merge_kernel(x_ref, o1_ref, o2_ref, o3_ref, l1_ref, l2_ref, l3_ref, ret_ref,
                  ga0_ref, ga1_ref, gr0_ref, gr1_ref, watt_ref, wret_ref, wout_ref, gffn_ref, wr_ref, br_ref,
                  below_ref, x1_ref, hf_ref, route_ref, counts_ref, on2_ref, on3_ref, ln2_ref, ln3_ref):
    tm = x_ref.shape[0]
    _to_token_order(o2_ref, on2_ref, DILATIONS[1])
    _to_token_order(o3_ref, on3_ref, DILATIONS[2])
    _to_token_order(l2_ref, ln2_ref, DILATIONS[1])
    _to_token_order(l3_ref, ln3_ref, DILATIONS[2])
    lses = [l1_ref[...], ln2_ref[0], ln3_ref[0]]

    att_parts = []
    for h in range(HEADS_PER_GROUP):
        ls = [l[:, h:h + 1] for l in lses]
        m = jnp.maximum(jnp.maximum(ls[0], ls[1]), ls[2])
        ws = [jnp.exp(l - m) for l in ls]
        wsum = ws[0] + ws[1] + ws[2]
        hs = slice(h * HEAD_DIM, (h + 1) * HEAD_DIM)
        acc = (ws[0] / wsum) * o1_ref[:, hs].astype(F32)
        acc = acc + (ws[1] / wsum) * on2_ref[h]
        acc = acc + (ws[2] / wsum) * on3_ref[h]
        att_parts.append(acc)
    att = jnp.concatenate(att_parts, axis=-1).astype(BF16)

    a_proj = jnp.dot(att, watt_ref[...], preferred_element_type=F32)
    r_proj = jnp.dot(ret_ref[...], wret_ref[...], preferred_element_type=F32)
    gate_a = jnp.concatenate([ga0_ref[...], ga1_ref[...]], axis=-1).astype(F32)
    gate_r = jnp.concatenate([gr0_ref[...], gr1_ref[...]], axis=-1).astype(F32)
    merged = jax.nn.sigmoid(gate_a) * a_proj + jax.nn.sigmoid(gate_r) * r_proj
    x1 = x_ref[...] + jnp.dot(merged.astype(BF16), wout_ref[...], preferred_element_type=F32)
    x1_ref[...] = x1

    ms = jnp.mean(x1 * x1, axis=-1, keepdims=True)
    hf = x1 * lax.rsqrt(ms + NORM_EPS) * gffn_ref[...]
    _store_row_tiles(hf_ref, hf)

    logits = jnp.dot(hf.astype(BF16), wr_ref[...], preferred_element_type=F32) + br_ref[...]
    lane = lax.broadcasted_iota(jnp.int32, (tm, LANES), 1)
    lane_f = lane.astype(F32)
    work = logits
    vals, idxs, hits = [], [], []
    for _ in range(TOP_K):
        mk = jnp.max(work, axis=-1, keepdims=True)
        ik = jnp.min(jnp.where(work == mk, lane_f, float(LANES)), axis=-1, keepdims=True)
        hit = lane_f == ik
        work = jnp.where(hit, -jnp.inf, work)
        vals.append(mk)
        idxs.append(ik)
        hits.append(hit)
    es = [jnp.exp(v - vals[0]) for v in vals]
    esum = es[0] + es[1] + es[2] + es[3]
    gates = [e / esum for e in es]

    sel = jnp.zeros((tm, LANES), F32)
    for hit in hits:
        sel = jnp.where(hit, 1.0, sel)

    @pl.when(pl.program_id(0) == 0)
    def _():
        counts_ref[...] = jnp.zeros_like(counts_ref)

    rank = jnp.dot(below_ref[...], sel.astype(BF16), preferred_element_type=F32) + counts_ref[...]
    counts_ref[...] = counts_ref[...] + jnp.sum(sel, axis=0, keepdims=True)

    route = jnp.zeros((tm, LANES), F32)
    for k in range(TOP_K):
        rank_k = jnp.sum(jnp.where(hits[k], rank, 0.0), axis=-1, keepdims=True)
        route = jnp.where(lane == ROUTE_IDX + k, idxs[k], route)
        route = jnp.where(lane == ROUTE_RANK + k, rank_k, route)
        route = jnp.where(lane == ROUTE_GATE + k, gates[k], route)
    route_ref[...] = route


def _merge(x2d, B, S, o1, l1, o2, l2, o3, l3, ret, proj, gate_col0, w_att, w_ret, w_out, g_ffn, w_router_p,
           b_router_p, tm=512):
    T, D = x2d.shape
    per_b = S // tm
    d4, d16 = DILATIONS[1], DILATIONS[2]
    row = lambda w: pl.BlockSpec((tm, w), lambda i: (i, 0))
    full = lambda a: pl.BlockSpec(a.shape, lambda i: (0,) * a.ndim)
    gcol = lambda c: pl.BlockSpec((tm, COL_BLOCK), lambda i, c=c: (i, c))
    res = lambda d, w: pl.BlockSpec((None, d, tm // d, w), lambda i: (i // per_b, 0, i % per_b, 0))
    below = jnp.tril(jnp.ones((tm, tm), BF16), k=-1)
    slabs = lambda n: pltpu.VMEM((n, tm, LANES), F32)
    return pl.pallas_call(
        _merge_kernel,
        out_shape=(jax.ShapeDtypeStruct((T, D), F32), jax.ShapeDtypeStruct((T * ROW_TILE, LANES), F32),
                   jax.ShapeDtypeStruct((T, LANES), F32), jax.ShapeDtypeStruct((1, LANES), F32)),
        grid=(T // tm,),
        in_specs=[row(D), row(GROUP_WIDTH), res(d4, GROUP_WIDTH), res(d16, GROUP_WIDTH),
                  row(LANES), res(d4, LANES), res(d16, LANES),
                  row(RET_HEADS * RET_DV), gcol(gate_col0), gcol(gate_col0 + 1), gcol(gate_col0 + 2),
                  gcol(gate_col0 + 3), full(w_att), full(w_ret), full(w_out), full(g_ffn), full(w_router_p),
                  full(b_router_p), full(below)],
        out_specs=(row(D), pl.BlockSpec((tm * ROW_TILE, LANES), lambda i: (i, 0)), row(LANES),
                   pl.BlockSpec((1, LANES), lambda i: (0, 0))),
        scratch_shapes=[slabs(GROUP_WIDTH // LANES), slabs(GROUP_WIDTH // LANES), slabs(1), slabs(1)],
        compiler_params=_cparams(("arbitrary",), VMEM_LIMIT_BIG),
        name="merge",
    )(x2d, o1, o2, o3, l1, l2, l3, ret, proj, proj, proj, proj, w_att, w_ret, w_out, g_ffn, w_router_p,
      b_router_p, below)


def _row_copy(src, dst, src_row8, dst_row8, sem):
    return pltpu.make_async_copy(src.at[pl.ds(pl.multiple_of(src_row8, ROW_TILE), ROW_TILE)],
                                 dst.at[pl.ds(pl.multiple_of(dst_row8, ROW_TILE), ROW_TILE)], sem)


def _dispatch_kernel(pad_ref, pos_ref, hf_ref, xs_hbm, zero_ref, sem, *, tile):
    @pl.when(pl.program_id(0) == 0)
    def _():
        zero_ref[...] = jnp.zeros_like(zero_ref)
        total = 0
        for e in range(N_EXPERTS):
            first = pad_ref[e]

            def zero_row(j, carry, first=first):
                _row_copy(zero_ref, xs_hbm, 0, first + j * ROW_TILE, sem).start()
                return carry

            lax.fori_loop(0, pad_ref[N_EXPERTS + e], zero_row, 0)
            total = total + pad_ref[N_EXPERTS + e]

        def zero_wait(n, carry):
            _row_copy(zero_ref, xs_hbm, 0, 0, sem).wait()
            return carry

        lax.fori_loop(0, total, zero_wait, 0)

    def issue(i, carry):
        for k in range(TOP_K):
            _row_copy(hf_ref, xs_hbm, i * ROW_TILE, pos_ref[0, k, i], sem).start(priority=k % 2)
        return carry

    lax.fori_loop(0, tile, issue, 0, unroll=ISSUE_UNROLL)

    def drain(i, carry):
        for k in range(TOP_K):
            _row_copy(hf_ref, xs_hbm, 0, 0, sem).wait()
        return carry

    lax.fori_loop(0, tile, drain, 0, unroll=DRAIN_UNROLL)


def _dispatch(hf, pos8_tiles, pad_info, n_rows, tile):
    T = hf.shape[0] // ROW_TILE
    grid_spec = pltpu.PrefetchScalarGridSpec(
        num_scalar_prefetch=1,
        grid=(T // tile,),
        in_specs=[pl.BlockSpec((1, TOP_K, tile), lambda i, pad: (i, 0, 0), memory_space=pltpu.SMEM),
                  pl.BlockSpec((tile * ROW_TILE, LANES), lambda i, pad: (i, 0))],
        out_specs=pl.BlockSpec(memory_space=pl.ANY),
        scratch_shapes=[pltpu.VMEM((ROW_TILE, LANES), hf.dtype), pltpu.SemaphoreType.DMA(())],
    )
    return pl.pallas_call(
        functools.partial(_dispatch_kernel, tile=tile),
        out_shape=jax.ShapeDtypeStruct((n_rows * ROW_TILE, LANES), hf.dtype),
        grid_spec=grid_spec,
        compiler_params=_cparams(("arbitrary",)),
        name="dispatch",
    )(pad_info, pos8_tiles, hf)


def _expert_kernel(be_ref, nu_ref, x_ref, w1_ref, b1_ref, w2_ref, b2_ref, y_ref, w1b_ref, w2b_ref):
    i = pl.program_id(0)
    F = w2_ref.shape[0]

    @pl.when(i < nu_ref[0])
    def _():
        @pl.when((i == 0) | (be_ref[i] != be_ref[jnp.maximum(i - 1, 0)]))
        def _():
            w1b_ref[...] = w1_ref[...].astype(BF16)
            w2b_ref[...] = w2_ref[...].astype(BF16)

        x = _load_row_tiles(x_ref, EXPERT_BLOCK).astype(BF16)
        gu = jnp.dot(x, w1b_ref[...], preferred_element_type=F32) + b1_ref[...]
        gate = jnp.minimum(gu[:, :F], SWIGLU_LIMIT)
        up = jnp.clip(gu[:, F:], -SWIGLU_LIMIT, SWIGLU_LIMIT)
        act = (up + 1.0) * gate * jax.nn.sigmoid(SWIGLU_ALPHA * gate)
        _store_row_tiles(y_ref, jnp.dot(act.astype(BF16), w2b_ref[...], preferred_element_type=F32) + b2_ref[...])


def _experts(xs, n_blocks, blk_expert, n_used, w1, b1, w2, b2):
    E, D, F2 = w1.shape
    F = w2.shape[1]
    M = EXPERT_BLOCK
    blk = lambda i, be, nu: (jnp.minimum(i, nu[0] - 1), 0)
    grid_spec = pltpu.PrefetchScalarGridSpec(
        num_scalar_prefetch=2,
        grid=(n_blocks,),
        in_specs=[pl.BlockSpec((M * ROW_TILE, LANES), blk),
                  pl.BlockSpec((None, D, F2), lambda i, be, nu: (be[i], 0, 0)),
                  pl.BlockSpec((None, 1, F2), lambda i, be, nu: (be[i], 0, 0)),
                  pl.BlockSpec((None, F, D), lambda i, be, nu: (be[i], 0, 0)),
                  pl.BlockSpec((None, 1, D), lambda i, be, nu: (be[i], 0, 0))],
        out_specs=pl.BlockSpec((M * ROW_TILE, LANES), blk),
        scratch_shapes=[pltpu.VMEM((D, F2), BF16), pltpu.VMEM((F, D), BF16)],
    )
    return pl.pallas_call(
        _expert_kernel,
        out_shape=jax.ShapeDtypeStruct((n_blocks * M * ROW_TILE, LANES), F32),
        grid_spec=grid_spec,
        compiler_params=_cparams(("arbitrary",), VMEM_LIMIT_BIG),
        name="experts",
    )(blk_expert, n_used, xs, w1, b1.reshape(E, 1, F2), w2, b2.reshape(E, 1, D))


def _combine_kernel(pos_ref, pos_next_ref, route_ref, x1_ref, g_ref, ys_hbm, o_ref, buf_ref, sem, *, tile):
    step = pl.program_id(0)
    cur = lax.rem(step, 2)

    def gather(p_ref, slot):
        def issue(i, carry):
            for k in range(TOP_K):
                _row_copy(ys_hbm, buf_ref.at[slot, k], p_ref[0, k, i], i * ROW_TILE,
                          sem.at[slot]).start(priority=k % 2)
            return carry

        lax.fori_loop(0, tile, issue, 0, unroll=ISSUE_UNROLL)

    nxt = 1 - cur

    @pl.when(step == 0)
    def _():
        gather(pos_ref, 0)

    def drain(slot):
        def body(i, carry):
            for k in range(TOP_K):
                _row_copy(ys_hbm, buf_ref.at[slot, k], 0, 0, sem.at[slot]).wait()
            return carry

        lax.fori_loop(0, tile, body, 0, unroll=DRAIN_UNROLL)

    @pl.when(step + 1 < pl.num_programs(0))
    def _():
        gather(pos_next_ref, nxt)

    drain(cur)

    route = route_ref[...]
    acc = x1_ref[...]
    for k in range(TOP_K):
        acc = acc + route[:, ROUTE_GATE + k:ROUTE_GATE + k + 1] * _load_row_tiles(buf_ref.at[cur, k], tile)
    ms = jnp.mean(acc * acc, axis=-1, keepdims=True)
    o_ref[...] = acc * lax.rsqrt(ms + NORM_EPS) * g_ref[...]


def _combine(pos_tiles, route, x1, g_final, ys, tile):
    T, D = x1.shape
    nt = T // tile
    return pl.pallas_call(
        functools.partial(_combine_kernel, tile=tile),
        out_shape=jax.ShapeDtypeStruct((T, D), F32),
        grid=(nt,),
        in_specs=[pl.BlockSpec((1, TOP_K, tile), lambda i: (i, 0, 0), memory_space=pltpu.SMEM),
                  pl.BlockSpec((1, TOP_K, tile), lambda i: (jnp.minimum(i + 1, nt - 1), 0, 0),
                               memory_space=pltpu.SMEM),
                  pl.BlockSpec((tile, LANES), lambda i: (i, 0)),
                  pl.BlockSpec((tile, D), lambda i: (i, 0)),
                  pl.BlockSpec((1, D), lambda i: (0, 0)),
                  pl.BlockSpec(memory_space=pl.ANY)],
        out_specs=pl.BlockSpec((tile, D), lambda i: (i, 0)),
        scratch_shapes=[pltpu.VMEM((2, TOP_K, tile * ROW_TILE, LANES), F32), pltpu.SemaphoreType.DMA((2,))],
        compiler_params=_cparams(("arbitrary",)),
        name="combine",
    )(pos_tiles, pos_tiles, route, x1, g_final.reshape(1, D), ys)


def _layer(x2d, B, S, norm_mix_g, w_in, rel_bias, w_att, w_ret, w_out, norm_ffn_g, w_router, b_router,
           w1, b1, w2, b2):
    T, D = x2d.shape
    proj, a4, a16 = _inproj(x2d, norm_mix_g, w_in.astype(BF16), B, S)

    group_bias = [_band_bias(rel_bias[:, g * HEADS_PER_GROUP:(g + 1) * HEADS_PER_GROUP], d)
                  for g, d in enumerate(DILATIONS)]
    per_step = ATTN_BLOCKS_PER_STEP

    def attend(qkv, bias):
        blocks = min(qkv.shape[2] // N_STEPS, per_step)
        residues = min(qkv.shape[1], max(per_step // blocks, 1))
        return _attention_group(qkv, bias, blocks, residues)

    o1, l1 = attend(proj.reshape(B, 1, S, proj.shape[1]), group_bias[0])
    o2, l2 = attend(a4, group_bias[1])
    o3, l3 = attend(a16, group_bias[2])

    ret = _retention(proj, B, S, MAIN_RET_COL)

    w_router_p = jnp.zeros((D, LANES), BF16).at[:, :N_EXPERTS].set(w_router.astype(BF16))
    b_router_p = jnp.full((1, LANES), NEG_BIG, F32).at[0, :N_EXPERTS].set(b_router.astype(F32))
    x1, hf, route, counts = _merge(x2d, B, S, o1.reshape(T, GROUP_WIDTH), l1.reshape(T, LANES), o2, l2, o3, l3,
                                   ret, proj, MAIN_GATE_COL, w_att.astype(BF16), w_ret.astype(BF16),
                                   w_out.astype(BF16), norm_ffn_g.reshape(1, D), w_router_p, b_router_p)

    M = EXPERT_BLOCK
    n_blocks = (T * TOP_K) // M + N_EXPERTS
    cnt = counts[0, :N_EXPERTS].astype(jnp.int32)
    padded = ((cnt + M - 1) // M) * M
    pend = jnp.cumsum(padded)
    pstart = pend - padded
    idx = route[:, ROUTE_IDX:ROUTE_IDX + TOP_K].astype(jnp.int32)
    rank = route[:, ROUTE_RANK:ROUTE_RANK + TOP_K].astype(jnp.int32)
    experts = jnp.arange(N_EXPERTS, dtype=jnp.int32)
    pos = rank + jnp.sum(jnp.where(idx[..., None] == experts, pstart, 0), axis=-1)
    n_used = (pend[-1] // M).astype(jnp.int32)
    blk_row = jnp.minimum(jnp.arange(n_blocks, dtype=jnp.int32), jnp.maximum(n_used - 1, 0)) * M
    blk_expert = jnp.minimum(jnp.sum((pend[None, :] <= blk_row[:, None]).astype(jnp.int32), axis=-1),
                             N_EXPERTS - 1)

    assert D == ROW_TILE * LANES
    tile = ROW_MOVE_TILE
    pos_tiles = (pos * ROW_TILE).reshape(T // tile, tile, TOP_K).transpose(0, 2, 1)
    pad_info = jnp.concatenate([(pstart + cnt) * ROW_TILE, padded - cnt]).astype(jnp.int32)
    xs = _dispatch(hf, pos_tiles, pad_info, n_blocks * M, tile)
    ys = _experts(xs, n_blocks, blk_expert, n_used.reshape(1), w1, b1, w2, b2)
    return pos_tiles, route, x1, ys


def kernel(x, norm_mix_g, w_in, rel_bias, w_att_branch, w_ret_branch, w_out, norm_ffn_g, w_router, b_router,
           w1, b1, w2, b2, norm_final_g):
    B, S, D = x.shape
    depth = w_in.shape[0]
    assert depth == 1, "the combine stage applies the final norm; a deeper stack needs a separate norm pass"
    x2d = x.reshape(B * S, D)
    pos_tiles, route, x1, ys = _layer(x2d, B, S, norm_mix_g[0], w_in[0], rel_bias, w_att_branch[0],
                                      w_ret_branch[0], w_out[0], norm_ffn_g[0], w_router[0], b_router[0],
                                      w1[0], b1[0], w2[0], b2[0])
    out = _combine(pos_tiles, route, x1, norm_final_g, ys, tile=pos_tiles.shape[2])
    return out.reshape(B, S, D)
```

```python
import functools
import math

import numpy as np
import jax
import jax.numpy as jnp
from jax import lax
from jax.experimental import pallas as pl
from jax.experimental.pallas import tpu as pltpu

F32 = jnp.float32
BF16 = jnp.bfloat16

NORM_EPS = 1e-5

HEAD_DIM = 128
HEADS_PER_GROUP = 4
GROUP_WIDTH = HEADS_PER_GROUP * HEAD_DIM
DILATIONS = (1, 4, 16)
N_STEPS = 128
N_GROUPS = len(DILATIONS)
REL_BUCKETS = 32
REL_MAX_DIST = 2048

RET_HEADS = 4
RET_DK = 128
RET_DV = 256
RET_CHUNK = 128
ROPE_BASE = 10000.0

N_EXPERTS = 32
TOP_K = 4
SWIGLU_LIMIT = 7.0
SWIGLU_ALPHA = 1.702
EXPERT_BLOCK = 512
ROW_MOVE_TILE = 512
ISSUE_UNROLL = 16
DRAIN_UNROLL = 8
ATTN_BLOCKS_PER_STEP = 16
RET_CHUNKS_PER_STEP = 2

LANES = 128
COL_BLOCK = 512
NEG_BIG = -1e30

VMEM_LIMIT = 48 * 1024 * 1024
VMEM_LIMIT_BIG = 56 * 1024 * 1024

N_ATT_CHUNKS = 3 * N_GROUPS
MAIN_RET_COL = 3
MAIN_GATE_COL = MAIN_RET_COL + (2 * RET_HEADS * RET_DK + 2 * RET_HEADS * RET_DV) // COL_BLOCK


def _cparams(sem, vmem=VMEM_LIMIT):
    return pltpu.CompilerParams(dimension_semantics=sem, vmem_limit_bytes=vmem)


def _inproj_kernel(x_ref, g_ref, w_ref, p4_ref, p16_ref, main_ref, a4_ref, a16_ref, h_ref, hp_ref):
    tm = x_ref.shape[0]
    x = x_ref[...]
    ms = jnp.mean(x * x, axis=-1, keepdims=True)
    h_ref[...] = (x * lax.rsqrt(ms + NORM_EPS) * g_ref[...]).astype(BF16)
    n_chunks = w_ref.shape[1] // COL_BLOCK

    def chunk(lhs_ref, c):
        return jnp.dot(lhs_ref[...], w_ref[:, c * COL_BLOCK:(c + 1) * COL_BLOCK], preferred_element_type=F32)

    for c in range(N_ATT_CHUNKS, n_chunks):
        mc = c - N_ATT_CHUNKS + MAIN_RET_COL
        main_ref[:, mc * COL_BLOCK:(mc + 1) * COL_BLOCK] = chunk(h_ref, c).astype(BF16)
    for part in range(3):
        main_ref[:, part * COL_BLOCK:(part + 1) * COL_BLOCK] = chunk(h_ref, part * N_GROUPS).astype(BF16)
    for group, (perm_ref, dest) in ((1, (p4_ref, a4_ref)), (2, (p16_ref, a16_ref))):
        d = DILATIONS[group]
        n = tm // d
        hp_ref[...] = jnp.dot(perm_ref[...], h_ref[...], preferred_element_type=F32).astype(BF16)
        for part in range(3):
            res = chunk(hp_ref, part * N_GROUPS + group)
            for r in range(d):
                dest[r, :, part * COL_BLOCK:(part + 1) * COL_BLOCK] = res[r * n:(r + 1) * n, :].astype(BF16)


def _residue_permutation(tm, d):
    n = tm // d
    p = np.zeros((tm, tm), np.float32)
    m, r = np.meshgrid(np.arange(n), np.arange(d), indexing="ij")
    p[(r * n + m).ravel(), (m * d + r).ravel()] = 1.0
    return jnp.asarray(p, BF16)


def _inproj(x2d, g, w_bf16, B, S, tm=512):
    T, D = x2d.shape
    n_main = (w_bf16.shape[1] // COL_BLOCK - N_ATT_CHUNKS + 3) * COL_BLOCK
    d4, d16 = DILATIONS[1], DILATIONS[2]
    per_b = S // tm
    return pl.pallas_call(
        _inproj_kernel,
        out_shape=(jax.ShapeDtypeStruct((T, n_main), BF16),
                   jax.ShapeDtypeStruct((B, d4, S // d4, 3 * GROUP_WIDTH), BF16),
                   jax.ShapeDtypeStruct((B, d16, S // d16, 3 * GROUP_WIDTH), BF16)),
        grid=(T // tm,),
        in_specs=[
            pl.BlockSpec((tm, D), lambda i: (i, 0)),
            pl.BlockSpec((1, D), lambda i: (0, 0)),
            pl.BlockSpec(w_bf16.shape, lambda i: (0, 0), pipeline_mode=pl.Buffered(1)),
            pl.BlockSpec((tm, tm), lambda i: (0, 0), pipeline_mode=pl.Buffered(1)),
            pl.BlockSpec((tm, tm), lambda i: (0, 0), pipeline_mode=pl.Buffered(1)),
        ],
        out_specs=(pl.BlockSpec((tm, n_main), lambda i: (i, 0)),
                   pl.BlockSpec((None, d4, tm // d4, 3 * GROUP_WIDTH), lambda i: (i // per_b, 0, i % per_b, 0)),
                   pl.BlockSpec((None, d16, tm // d16, 3 * GROUP_WIDTH), lambda i: (i // per_b, 0, i % per_b, 0))),
        scratch_shapes=[pltpu.VMEM((tm, D), BF16), pltpu.VMEM((tm, D), BF16)],
        compiler_params=_cparams(("arbitrary",), VMEM_LIMIT_BIG),
        name="inproj",
    )(x2d, g.reshape(1, D), w_bf16, _residue_permutation(tm, d4), _residue_permutation(tm, d16))


def _attn_block(q, kp, kc, vp, vc, bias_ref, has_prev):
    scale = HEAD_DIM ** -0.5
    heads = range(HEADS_PER_GROUP)
    hs = [slice(h * HEAD_DIM, (h + 1) * HEAD_DIM) for h in heads]
    rs = [slice(h * N_STEPS, (h + 1) * N_STEPS) for h in heads]
    k = jnp.concatenate([kp, kc], axis=0)
    v = jnp.concatenate([vp, vc], axis=0)
    dn = (((1,), (1,)), ((), ()))
    s = jnp.concatenate([lax.dot_general(q[:, hs[h]], k[:, hs[h]], dn, preferred_element_type=F32)
                         for h in heads], axis=0)
    s = s * scale + bias_ref[...]
    if has_prev is not True:
        col = lax.broadcasted_iota(jnp.int32, s.shape, 1)
        s = jnp.where((col >= N_STEPS) | has_prev, s, NEG_BIG)
    m = jnp.max(s, axis=-1, keepdims=True)
    p = jnp.exp(s - m)
    l = jnp.sum(p, axis=-1, keepdims=True)
    pb = p.astype(BF16)
    lse_rows = m + jnp.log(l)
    lane = lax.broadcasted_iota(jnp.int32, (N_STEPS, LANES), 1)
    outs = []
    lse = jnp.zeros((N_STEPS, LANES), F32)
    for h in heads:
        acc = jnp.dot(pb[rs[h], :], v[:, hs[h]], preferred_element_type=F32)
        outs.append(acc / l[rs[h], :])
        lse = jnp.where(lane == h, lse_rows[rs[h], :], lse)
    return jnp.concatenate(outs, axis=-1), lse


def _attn_kernel(*refs, blocks, residues, with_prev):
    if with_prev:
        q_ref, k_ref, v_ref, kprev_ref, vprev_ref, bias_ref, o_ref, lse_ref = refs
    else:
        q_ref, k_ref, v_ref, bias_ref, o_ref, lse_ref = refs
    n = pl.program_id(2)

    def run(rr, j, kp, vp, has_prev):
        rows = slice(j * N_STEPS, (j + 1) * N_STEPS)
        o, lse = _attn_block(q_ref[rr, rows, :], kp, k_ref[rr, rows, :], vp, v_ref[rr, rows, :], bias_ref,
                             has_prev)
        o_ref[rr, rows, :] = o.astype(o_ref.dtype)
        lse_ref[rr, rows, :] = lse

    for rr in range(residues):
        if with_prev:
            run(rr, 0, kprev_ref[rr], vprev_ref[rr], n > 0)
        else:
            run(rr, 0, k_ref[rr, :N_STEPS, :], v_ref[rr, :N_STEPS, :], False)
        for j in range(1, blocks):
            prev = slice((j - 1) * N_STEPS, j * N_STEPS)
            run(rr, j, k_ref[rr, prev, :], v_ref[rr, prev, :], True)


def _attention_group(qkv, bias, blocks, residues):
    B, d, L, _ = qkv.shape
    W = GROUP_WIDTH
    rows = blocks * N_STEPS
    steps = L // rows
    with_prev = steps > 1

    def cur(c):
        return pl.BlockSpec((None, residues, rows, W), lambda b, r, n: (b, r, n, c))

    def prev(c):
        return pl.BlockSpec((None, residues, N_STEPS, W),
                            lambda b, r, n: (b, r, jnp.maximum(n * blocks - 1, 0), c))

    in_specs = [cur(0), cur(1), cur(2)] + ([prev(1), prev(2)] if with_prev else [])
    in_specs.append(pl.BlockSpec((HEADS_PER_GROUP * N_STEPS, 2 * N_STEPS), lambda b, r, n: (0, 0)))
    operands = [qkv] * (5 if with_prev else 3) + [bias]
    return pl.pallas_call(
        functools.partial(_attn_kernel, blocks=blocks, residues=residues, with_prev=with_prev),
        out_shape=(jax.ShapeDtypeStruct((B, d, L, W), BF16), jax.ShapeDtypeStruct((B, d, L, LANES), F32)),
        grid=(B, d // residues, steps),
        in_specs=in_specs,
        out_specs=(pl.BlockSpec((None, residues, rows, W), lambda b, r, n: (b, r, n, 0)),
                   pl.BlockSpec((None, residues, rows, LANES), lambda b, r, n: (b, r, n, 0))),
        compiler_params=_cparams(("arbitrary", "arbitrary", "arbitrary")),
        name=f"attn_d{d}",
    )(*operands)


def _t5_bucket(dist):
    max_exact = REL_BUCKETS // 2
    d_f = jnp.maximum(dist, 1).astype(F32)
    large = max_exact + (jnp.log(d_f / max_exact) / math.log(REL_MAX_DIST / max_exact)
                         * (REL_BUCKETS - max_exact)).astype(jnp.int32)
    large = jnp.minimum(large, REL_BUCKETS - 1)
    return jnp.where(dist < max_exact, dist, large)


def _band_bias(rel_bias_g, dilation):
    qi = jnp.arange(N_STEPS)[:, None]
    kj = jnp.arange(2 * N_STEPS)[None, :]
    step_dist = qi + N_STEPS - kj
    band = (step_dist >= 0) & (step_dist <= N_STEPS)
    bucket = _t5_bucket(jnp.maximum(step_dist, 0) * dilation)
    table = rel_bias_g.astype(F32).T
    hit = bucket[None, None] == jnp.arange(REL_BUCKETS)[None, :, None, None]
    bias = jnp.sum(jnp.where(hit, table[:, :, None, None], 0.0), axis=1)
    return jnp.where(band[None], bias, NEG_BIG).reshape(HEADS_PER_GROUP * N_STEPS, 2 * N_STEPS)


def _ret_kernel(q_ref, k_ref, v0_ref, v1_ref, g0_ref, g1_ref, cos_ref, sin_ref, decay_ref, xi_ref, zeta_ref,
                o_ref, state_ref, *, g_chunk):
    n = pl.program_id(1)

    @pl.when(n == 0)
    def _():
        state_ref[...] = jnp.zeros_like(state_ref)

    C = RET_CHUNK
    n_sub = q_ref.shape[0] // C
    lane = lax.broadcasted_iota(jnp.int32, (C, RET_DK), 1)
    even = (lane % 2) == 0

    def rotary(t, rows):
        partner = jnp.where(even, pltpu.roll(t, RET_DK - 1, 1), pltpu.roll(t, 1, 1))
        return t * cos_ref[rows, :] + partner * sin_ref[rows, :]

    heads = range(RET_HEADS)
    ks = [slice(h * RET_DK, (h + 1) * RET_DK) for h in heads]
    vs = [slice((h % 2) * RET_DV, (h % 2 + 1) * RET_DV) for h in heads]
    v_refs = [v0_ref if h < 2 else v1_ref for h in heads]
    g_refs = [g0_ref if h < 2 else g1_ref for h in heads]
    states = [state_ref[h] for h in heads]
    for sub in range(n_sub):
        rows = slice(sub * C, (sub + 1) * C)
        qbs, kzs, inners, vals = [], [], [], []
        for h in heads:
            qr = rotary(q_ref[rows, ks[h]].astype(F32), rows)
            kr = rotary(k_ref[rows, ks[h]].astype(F32), rows) * (RET_DK ** -0.5)
            v = v_refs[h][rows, vs[h]]
            qb = qr.astype(BF16)
            scores = lax.dot_general(qb, kr.astype(BF16), (((1,), (1,)), ((), ())),
                                     preferred_element_type=F32) * decay_ref[h]
            inners.append(jnp.dot(scores.astype(BF16), v, preferred_element_type=F32))
            qbs.append(qb)
            kzs.append((kr * zeta_ref[h]).astype(BF16))
            vals.append(v)
        for h in heads:
            cross = jnp.dot(qbs[h], states[h].astype(BF16), preferred_element_type=F32) * xi_ref[h]
            states[h] = states[h] * g_chunk[h] + lax.dot_general(kzs[h], vals[h], (((0,), (0,)), ((), ())),
                                                                 preferred_element_type=F32)
            ret = inners[h] + cross
            mu = jnp.mean(ret, axis=-1, keepdims=True)
            cen = ret - mu
            var = jnp.mean(cen * cen, axis=-1, keepdims=True)
            normed = cen * lax.rsqrt(var + NORM_EPS)
            g = g_refs[h][rows, vs[h]].astype(F32)
            o_ref[rows, h * RET_DV:(h + 1) * RET_DV] = (g * jax.nn.sigmoid(g) * normed).astype(o_ref.dtype)
    for h in heads:
        state_ref[h] = states[h]


def _retention(proj, B, S, col0):
    T = B * S
    nc = S // RET_CHUNK
    C = RET_CHUNK
    log_g = np.log(1.0 - 2.0 ** (-5.0 - np.arange(RET_HEADS, dtype=np.float64)))
    idx = np.arange(C, dtype=np.float64)
    diff = idx[:, None] - idx[None, :]
    decay = np.where(diff >= 0, np.exp(np.maximum(diff, 0.0)[None] * log_g[:, None, None]), 0.0)
    xi = np.exp((idx + 1.0)[None, :] * log_g[:, None])
    zeta = np.exp((C - 1.0 - idx)[None, :] * log_g[:, None])
    g_chunk = tuple(float(v) for v in np.exp(C * log_g))
    xi_b = np.broadcast_to(xi[:, :, None], (RET_HEADS, C, RET_DV)).astype(np.float32)
    zeta_b = np.broadcast_to(zeta[:, :, None], (RET_HEADS, C, RET_DK)).astype(np.float32)

    inv = ROPE_BASE ** (-np.arange(0, RET_DK, 2, dtype=np.float64) / RET_DK)
    ang = np.arange(S, dtype=np.float64)[:, None] * inv[None]
    cos_t = np.repeat(np.cos(ang), 2, axis=1).astype(np.float32)
    sin_t = np.stack([-np.sin(ang), np.sin(ang)], axis=-1).reshape(S, RET_DK).astype(np.float32)

    per_step = max(n for n in range(1, RET_CHUNKS_PER_STEP + 1) if nc % n == 0)
    rows = per_step * C
    ns = S // rows

    def col(c):
        return pl.BlockSpec((rows, COL_BLOCK), lambda b, n, c=c: (b * ns + n, c))

    const3 = lambda shape: pl.BlockSpec(shape, lambda b, n: (0, 0, 0))
    return pl.pallas_call(
        functools.partial(_ret_kernel, g_chunk=g_chunk),
        out_shape=jax.ShapeDtypeStruct((T, RET_HEADS * RET_DV), BF16),
        grid=(B, ns),
        in_specs=[col(col0), col(col0 + 1), col(col0 + 2), col(col0 + 3), col(col0 + 4), col(col0 + 5),
                  pl.BlockSpec((rows, RET_DK), lambda b, n: (n, 0)),
                  pl.BlockSpec((rows, RET_DK), lambda b, n: (n, 0)),
                  const3((RET_HEADS, C, C)), const3((RET_HEADS, C, RET_DV)), const3((RET_HEADS, C, RET_DK))],
        out_specs=pl.BlockSpec((rows, RET_HEADS * RET_DV), lambda b, n: (b * ns + n, 0)),
        scratch_shapes=[pltpu.VMEM((RET_HEADS, RET_DK, RET_DV), F32)],
        compiler_params=_cparams(("arbitrary", "arbitrary")),
        name="retention",
    )(proj, proj, proj, proj, proj, proj, jnp.asarray(cos_t), jnp.asarray(sin_t),
      jnp.asarray(decay.astype(np.float32)), jnp.asarray(xi_b), jnp.asarray(zeta_b))


ROUTE_IDX, ROUTE_RANK, ROUTE_GATE = 0, 4, 8


ROW_TILE = 8


def _store_row_tiles(dst_ref, val):
    n = val.shape[0]
    for c in range(ROW_TILE):
        dst_ref[pl.ds(c, n, stride=ROW_TILE), :] = val[:, c * LANES:(c + 1) * LANES]


def _load_row_tiles(src_ref, n):
    return jnp.concatenate([src_ref[pl.ds(c, n, stride=ROW_TILE), :] for c in range(ROW_TILE)], axis=-1)


def _to_token_order(src_ref, dst_ref, d):
    n = src_ref.shape[1]
    for r in range(d):
        blk = src_ref[r].astype(F32)
        for s in range(dst_ref.shape[0]):
            dst_ref[s, pl.ds(r, n, stride=d), :] = blk[:, s * LANES:(s + 1) * LANES]


def _merge_kernel(x_ref, o1_ref, o2_ref, o3_ref, l1_ref, l2_ref, l3_ref, ret_ref,
                  ga0_ref, ga1_ref, gr0_ref, gr1_ref, watt_ref, wret_ref, wout_ref, gffn_ref, wr_ref, br_ref,
                  below_ref, x1_ref, hf_ref, route_ref, counts_ref, on2_ref, on3_ref, ln2_ref, ln3_ref):
    tm = x_ref.shape[0]
    _to_token_order(o2_ref, on2_ref, DILATIONS[1])
    _to_token_order(o3_ref, on3_ref, DILATIONS[2])
    _to_token_order(l2_ref, ln2_ref, DILATIONS[1])
    _to_token_order(l3_ref, ln3_ref, DILATIONS[2])
    lses = [l1_ref[...], ln2_ref[0], ln3_ref[0]]

    att_parts = []
    for h in range(HEADS_PER_GROUP):
        ls = [l[:, h:h + 1] for l in lses]
        m = jnp.maximum(jnp.maximum(ls[0], ls[1]), ls[2])
        ws = [jnp.exp(l - m) for l in ls]
        wsum = ws[0] + ws[1] + ws[2]
        hs = slice(h * HEAD_DIM, (h + 1) * HEAD_DIM)
        acc = (ws[0] / wsum) * o1_ref[:, hs].astype(F32)
        acc = acc + (ws[1] / wsum) * on2_ref[h]
        acc = acc + (ws[2] / wsum) * on3_ref[h]
        att_parts.append(acc)
    att = jnp.concatenate(att_parts, axis=-1).astype(BF16)

    a_proj = jnp.dot(att, watt_ref[...], preferred_element_type=F32)
    r_proj = jnp.dot(ret_ref[...], wret_ref[...], preferred_element_type=F32)
    gate_a = jnp.concatenate([ga0_ref[...], ga1_ref[...]], axis=-1).astype(F32)
    gate_r = jnp.concatenate([gr0_ref[...], gr1_ref[...]], axis=-1).astype(F32)
    merged = jax.nn.sigmoid(gate_a) * a_proj + jax.nn.sigmoid(gate_r) * r_proj
    x1 = x_ref[...] + jnp.dot(merged.astype(BF16), wout_ref[...], preferred_element_type=F32)
    x1_ref[...] = x1

    ms = jnp.mean(x1 * x1, axis=-1, keepdims=True)
    hf = x1 * lax.rsqrt(ms + NORM_EPS) * gffn_ref[...]
    _store_row_tiles(hf_ref, hf)

    logits = jnp.dot(hf.astype(BF16), wr_ref[...], preferred_element_type=F32) + br_ref[...]
    lane = lax.broadcasted_iota(jnp.int32, (tm, LANES), 1)
    lane_f = lane.astype(F32)
    work = logits
    vals, idxs, hits = [], [], []
    for _ in range(TOP_K):
        mk = jnp.max(work, axis=-1, keepdims=True)
        ik = jnp.min(jnp.where(work == mk, lane_f, float(LANES)), axis=-1, keepdims=True)
        hit = lane_f == ik
        work = jnp.where(hit, -jnp.inf, work)
        vals.append(mk)
        idxs.append(ik)
        hits.append(hit)
    es = [jnp.exp(v - vals[0]) for v in vals]
    esum = es[0] + es[1] + es[2] + es[3]
    gates = [e / esum for e in es]

    sel = jnp.zeros((tm, LANES), F32)
    for hit in hits:
        sel = jnp.where(hit, 1.0, sel)

    @pl.when(pl.program_id(0) == 0)
    def _():
        counts_ref[...] = jnp.zeros_like(counts_ref)

    rank = jnp.dot(below_ref[...], sel.astype(BF16), preferred_element_type=F32) + counts_ref[...]
    counts_ref[...] = counts_ref[...] + jnp.sum(sel, axis=0, keepdims=True)

    route = jnp.zeros((tm, LANES), F32)
    for k in range(TOP_K):
        rank_k = jnp.sum(jnp.where(hits[k], rank, 0.0), axis=-1, keepdims=True)
        route = jnp.where(lane == ROUTE_IDX + k, idxs[k], route)
        route = jnp.where(lane == ROUTE_RANK + k, rank_k, route)
        route = jnp.where(lane == ROUTE_GATE + k, gates[k], route)
    route_ref[...] = route


def _merge(x2d, B, S, o1, l1, o2, l2, o3, l3, ret, proj, gate_col0, w_att, w_ret, w_out, g_ffn, w_router_p,
           b_router_p, tm=512):
    T, D = x2d.shape
    per_b = S // tm
    d4, d16 = DILATIONS[1], DILATIONS[2]
    row = lambda w: pl.BlockSpec((tm, w), lambda i: (i, 0))
    full = lambda a: pl.BlockSpec(a.shape, lambda i: (0,) * a.ndim)
    gcol = lambda c: pl.BlockSpec((tm, COL_BLOCK), lambda i, c=c: (i, c))
    res = lambda d, w: pl.BlockSpec((None, d, tm // d, w), lambda i: (i // per_b, 0, i % per_b, 0))
    below = jnp.tril(jnp.ones((tm, tm), BF16), k=-1)
    slabs = lambda n: pltpu.VMEM((n, tm, LANES), F32)
    return pl.pallas_call(
        _merge_kernel,
        out_shape=(jax.ShapeDtypeStruct((T, D), F32), jax.ShapeDtypeStruct((T * ROW_TILE, LANES), F32),
                   jax.ShapeDtypeStruct((T, LANES), F32), jax.ShapeDtypeStruct((1, LANES), F32)),
        grid=(T // tm,),
        in_specs=[row(D), row(GROUP_WIDTH), res(d4, GROUP_WIDTH), res(d16, GROUP_WIDTH),
                  row(LANES), res(d4, LANES), res(d16, LANES),
                  row(RET_HEADS * RET_DV), gcol(gate_col0), gcol(gate_col0 + 1), gcol(gate_col0 + 2),
                  gcol(gate_col0 + 3), full(w_att), full(w_ret), full(w_out), full(g_ffn), full(w_router_p),
                  full(b_router_p), full(below)],
        out_specs=(row(D), pl.BlockSpec((tm * ROW_TILE, LANES), lambda i: (i, 0)), row(LANES),
                   pl.BlockSpec((1, LANES), lambda i: (0, 0))),
        scratch_shapes=[slabs(GROUP_WIDTH // LANES), slabs(GROUP_WIDTH // LANES), slabs(1), slabs(1)],
        compiler_params=_cparams(("arbitrary",), VMEM_LIMIT_BIG),
        name="merge",
    )(x2d, o1, o2, o3, l1, l2, l3, ret, proj, proj, proj, proj, w_att, w_ret, w_out, g_ffn, w_router_p,
      b_router_p, below)


def _row_copy(src, dst, src_row8, dst_row8, sem):
    return pltpu.make_async_copy(src.at[pl.ds(pl.multiple_of(src_row8, ROW_TILE), ROW_TILE)],
                                 dst.at[pl.ds(pl.multiple_of(dst_row8, ROW_TILE), ROW_TILE)], sem)


def _dispatch_kernel(pad_ref, pos_ref, hf_ref, xs_hbm, zero_ref, sem, *, tile):
    @pl.when(pl.program_id(0) == 0)
    def _():
        zero_ref[...] = jnp.zeros_like(zero_ref)
        total = 0
        for e in range(N_EXPERTS):
            first = pad_ref[e]

            def zero_row(j, carry, first=first):
                _row_copy(zero_ref, xs_hbm, 0, first + j * ROW_TILE, sem).start()
                return carry

            lax.fori_loop(0, pad_ref[N_EXPERTS + e], zero_row, 0)
            total = total + pad_ref[N_EXPERTS + e]

        def zero_wait(n, carry):
            _row_copy(zero_ref, xs_hbm, 0, 0, sem).wait()
            return carry

        lax.fori_loop(0, total, zero_wait, 0)

    def issue(i, carry):
        for k in range(TOP_K):
            _row_copy(hf_ref, xs_hbm, i * ROW_TILE, pos_ref[0, k, i], sem).start(priority=k % 2)
        return carry

    lax.fori_loop(0, tile, issue, 0, unroll=ISSUE_UNROLL)

    def drain(i, carry):
        for k in range(TOP_K):
            _row_copy(hf_ref, xs_hbm, 0, 0, sem).wait()
        return carry

    lax.fori_loop(0, tile, drain, 0, unroll=DRAIN_UNROLL)


def _dispatch(hf, pos8_tiles, pad_info, n_rows, tile):
    T = hf.shape[0] // ROW_TILE
    grid_spec = pltpu.PrefetchScalarGridSpec(
        num_scalar_prefetch=1,
        grid=(T // tile,),
        in_specs=[pl.BlockSpec((1, TOP_K, tile), lambda i, pad: (i, 0, 0), memory_space=pltpu.SMEM),
                  pl.BlockSpec((tile * ROW_TILE, LANES), lambda i, pad: (i, 0))],
        out_specs=pl.BlockSpec(memory_space=pl.ANY),
        scratch_shapes=[pltpu.VMEM((ROW_TILE, LANES), hf.dtype), pltpu.SemaphoreType.DMA(())],
    )
    return pl.pallas_call(
        functools.partial(_dispatch_kernel, tile=tile),
        out_shape=jax.ShapeDtypeStruct((n_rows * ROW_TILE, LANES), hf.dtype),
        grid_spec=grid_spec,
        compiler_params=_cparams(("arbitrary",)),
        name="dispatch",
    )(pad_info, pos8_tiles, hf)


def _expert_kernel(be_ref, nu_ref, x_ref, w1_ref, b1_ref, w2_ref, b2_ref, y_ref, w1b_ref, w2b_ref):
    i = pl.program_id(0)
    F = w2_ref.shape[0]

    @pl.when(i < nu_ref[0])
    def _():
        @pl.when((i == 0) | (be_ref[i] != be_ref[jnp.maximum(i - 1, 0)]))
        def _():
            w1b_ref[...] = w1_ref[...].astype(BF16)
            w2b_ref[...] = w2_ref[...].astype(BF16)

        x = _load_row_tiles(x_ref, EXPERT_BLOCK).astype(BF16)
        acts = []
        cw = F // 4
        for c in range(4):
            gc = slice(c * cw, (c + 1) * cw)
            uc = slice(F + c * cw, F + (c + 1) * cw)
            g = jnp.dot(x, w1b_ref[:, gc], preferred_element_type=F32) + b1_ref[:, gc]
            u = jnp.dot(x, w1b_ref[:, uc], preferred_element_type=F32) + b1_ref[:, uc]
            gate = jnp.minimum(g, SWIGLU_LIMIT)
            up = jnp.clip(u, -SWIGLU_LIMIT, SWIGLU_LIMIT)
            acts.append(((up + 1.0) * gate * jax.nn.sigmoid(SWIGLU_ALPHA * gate)).astype(BF16))
        act = jnp.concatenate(acts, axis=-1)
        _store_row_tiles(y_ref, jnp.dot(act, w2b_ref[...], preferred_element_type=F32) + b2_ref[...])


def _experts(xs, n_blocks, blk_expert, n_used, w1, b1, w2, b2):
    E, D, F2 = w1.shape
    F = w2.shape[1]
    M = EXPERT_BLOCK
    blk = lambda i, be, nu: (jnp.minimum(i, nu[0] - 1), 0)
    grid_spec = pltpu.PrefetchScalarGridSpec(
        num_scalar_prefetch=2,
        grid=(n_blocks,),
        in_specs=[pl.BlockSpec((M * ROW_TILE, LANES), blk),
                  pl.BlockSpec((None, D, F2), lambda i, be, nu: (be[i], 0, 0)),
                  pl.BlockSpec((None, 1, F2), lambda i, be, nu: (be[i], 0, 0)),
                  pl.BlockSpec((None, F, D), lambda i, be, nu: (be[i], 0, 0)),
                  pl.BlockSpec((None, 1, D), lambda i, be, nu: (be[i], 0, 0))],
        out_specs=pl.BlockSpec((M * ROW_TILE, LANES), blk),
        scratch_shapes=[pltpu.VMEM((D, F2), BF16), pltpu.VMEM((F, D), BF16)],
    )
    return pl.pallas_call(
        _expert_kernel,
        out_shape=jax.ShapeDtypeStruct((n_blocks * M * ROW_TILE, LANES), F32),
        grid_spec=grid_spec,
        compiler_params=_cparams(("arbitrary",), VMEM_LIMIT_BIG),
        name="experts",
    )(blk_expert, n_used, xs, w1, b1.reshape(E, 1, F2), w2, b2.reshape(E, 1, D))


def _combine_kernel(pos_ref, pos_next_ref, route_ref, x1_ref, g_ref, ys_hbm, o_ref, buf_ref, sem, *, tile):
    step = pl.program_id(0)
    cur = lax.rem(step, 2)

    def gather(p_ref, slot):
        def issue(i, carry):
            for k in range(TOP_K):
                _row_copy(ys_hbm, buf_ref.at[slot, k], p_ref[0, k, i], i * ROW_TILE,
                          sem.at[slot]).start(priority=k % 2)
            return carry

        lax.fori_loop(0, tile, issue, 0, unroll=ISSUE_UNROLL)

    nxt = 1 - cur

    @pl.when(step == 0)
    def _():
        gather(pos_ref, 0)

    def drain(slot):
        def body(i, carry):
            for k in range(TOP_K):
                _row_copy(ys_hbm, buf_ref.at[slot, k], 0, 0, sem.at[slot]).wait()
            return carry

        lax.fori_loop(0, tile, body, 0, unroll=DRAIN_UNROLL)

    @pl.when(step + 1 < pl.num_programs(0))
    def _():
        gather(pos_next_ref, nxt)

    drain(cur)

    route = route_ref[...]
    acc = x1_ref[...]
    for k in range(TOP_K):
        acc = acc + route[:, ROUTE_GATE + k:ROUTE_GATE + k + 1] * _load_row_tiles(buf_ref.at[cur, k], tile)
    ms = jnp.mean(acc * acc, axis=-1, keepdims=True)
    o_ref[...] = acc * lax.rsqrt(ms + NORM_EPS) * g_ref[...]


def _combine(pos_tiles, route, x1, g_final, ys, tile):
    T, D = x1.shape
    nt = T // tile
    return pl.pallas_call(
        functools.partial(_combine_kernel, tile=tile),
        out_shape=jax.ShapeDtypeStruct((T, D), F32),
        grid=(nt,),
        in_specs=[pl.BlockSpec((1, TOP_K, tile), lambda i: (i, 0, 0), memory_space=pltpu.SMEM),
                  pl.BlockSpec((1, TOP_K, tile), lambda i: (jnp.minimum(i + 1, nt - 1), 0, 0),
                               memory_space=pltpu.SMEM),
                  pl.BlockSpec((tile, LANES), lambda i: (i, 0)),
                  pl.BlockSpec((tile, D), lambda i: (i, 0)),
                  pl.BlockSpec((1, D), lambda i: (0, 0)),
                  pl.BlockSpec(memory_space=pl.ANY)],
        out_specs=pl.BlockSpec((tile, D), lambda i: (i, 0)),
        scratch_shapes=[pltpu.VMEM((2, TOP_K, tile * ROW_TILE, LANES), F32), pltpu.SemaphoreType.DMA((2,))],
        compiler_params=_cparams(("arbitrary",)),
        name="combine",
    )(pos_tiles, pos_tiles, route, x1, g_final.reshape(1, D), ys)


def _layer(x2d, B, S, norm_mix_g, w_in, rel_bias, w_att, w_ret, w_out, norm_ffn_g, w_router, b_router,
           w1, b1, w2, b2):
    T, D = x2d.shape
    proj, a4, a16 = _inproj(x2d, norm_mix_g, w_in.astype(BF16), B, S)

    group_bias = [_band_bias(rel_bias[:, g * HEADS_PER_GROUP:(g + 1) * HEADS_PER_GROUP], d)
                  for g, d in enumerate(DILATIONS)]
    per_step = ATTN_BLOCKS_PER_STEP

    def attend(qkv, bias):
        blocks = min(qkv.shape[2] // N_STEPS, per_step)
        residues = min(qkv.shape[1], max(per_step // blocks, 1))
        return _attention_group(qkv, bias, blocks, residues)

    o1, l1 = attend(proj.reshape(B, 1, S, proj.shape[1]), group_bias[0])
    o2, l2 = attend(a4, group_bias[1])
    o3, l3 = attend(a16, group_bias[2])

    ret = _retention(proj, B, S, MAIN_RET_COL)

    w_router_p = jnp.zeros((D, LANES), BF16).at[:, :N_EXPERTS].set(w_router.astype(BF16))
    b_router_p = jnp.full((1, LANES), NEG_BIG, F32).at[0, :N_EXPERTS].set(b_router.astype(F32))
    x1, hf, route, counts = _merge(x2d, B, S, o1.reshape(T, GROUP_WIDTH), l1.reshape(T, LANES), o2, l2, o3, l3,
                                   ret, proj, MAIN_GATE_COL, w_att.astype(BF16), w_ret.astype(BF16),
                                   w_out.astype(BF16), norm_ffn_g.reshape(1, D), w_router_p, b_router_p)

    M = EXPERT_BLOCK
    n_blocks = (T * TOP_K) // M + N_EXPERTS
    cnt = counts[0, :N_EXPERTS].astype(jnp.int32)
    padded = ((cnt + M - 1) // M) * M
    pend = jnp.cumsum(padded)
    pstart = pend - padded
    idx = route[:, ROUTE_IDX:ROUTE_IDX + TOP_K].astype(jnp.int32)
    rank = route[:, ROUTE_RANK:ROUTE_RANK + TOP_K].astype(jnp.int32)
    experts = jnp.arange(N_EXPERTS, dtype=jnp.int32)
    pos = rank + jnp.sum(jnp.where(idx[..., None] == experts, pstart, 0), axis=-1)
    n_used = (pend[-1] // M).astype(jnp.int32)
    blk_row = jnp.minimum(jnp.arange(n_blocks, dtype=jnp.int32), jnp.maximum(n_used - 1, 0)) * M
    blk_expert = jnp.minimum(jnp.sum((pend[None, :] <= blk_row[:, None]).astype(jnp.int32), axis=-1),
                             N_EXPERTS - 1)

    assert D == ROW_TILE * LANES
    tile = ROW_MOVE_TILE
    pos_tiles = (pos * ROW_TILE).reshape(T // tile, tile, TOP_K).transpose(0, 2, 1)
    pad_info = jnp.concatenate([(pstart + cnt) * ROW_TILE, padded - cnt]).astype(jnp.int32)
    xs = _dispatch(hf, pos_tiles, pad_info, n_blocks * M, tile)
    ys = _experts(xs, n_blocks, blk_expert, n_used.reshape(1), w1, b1, w2, b2)
    return pos_tiles, route, x1, ys


def kernel(x, norm_mix_g, w_in, rel_bias, w_att_branch, w_ret_branch, w_out, norm_ffn_g, w_router, b_router,
           w1, b1, w2, b2, norm_final_g):
    B, S, D = x.shape
    depth = w_in.shape[0]
    assert depth == 1, "the combine stage applies the final norm; a deeper stack needs a separate norm pass"
    x2d = x.reshape(B * S, D)
    pos_tiles, route, x1, ys = _layer(x2d, B, S, norm_mix_g[0], w_in[0], rel_bias, w_att_branch[0],
                                      w_ret_branch[0], w_out[0], norm_ffn_g[0], w_router[0], b_router[0],
                                      w1[0], b1[0], w2[0], b2[0])
    out = _combine(pos_tiles, route, x1, norm_final_g, ys, tile=pos_tiles.shape[2])
    return out.reshape(B, S, D)
```
